```python
import math
import jax, jax.numpy as jnp
from jax import lax
import numpy as np

D_MODEL = 2048
BATCH = 8
SEQ = 2048
DEPTH = 4
DEC_BATCH = 32
DEC_SEQ = 64
PAST_LEN = 4096

CHUNK = 64
N_EVEN = (DEPTH + 1) // 2
N_ODD = DEPTH // 2
MLA_HEADS = 8
Q_RANK = 512
KV_RANK = 512
NOPE_DIM = 128
ROPE_DIM = 64
V_DIM = 128
ROPE_THETA = 10000.0
MLA_SCALE = (NOPE_DIM + ROPE_DIM) ** -0.5
MLA_QBLK = 128
MASK_NEG = -1e30
POOL_WINDOWS = (2, 4, 8, 16)
POOL_GROUPS = len(POOL_WINDOWS)
POOL_WIDTH = D_MODEL - MLA_HEADS * V_DIM
POOL_GROUP_DIM = POOL_WIDTH // POOL_GROUPS
POOL_KEEP = max(POOL_WINDOWS) - 1
IN_EVEN = Q_RANK + KV_RANK + ROPE_DIM + POOL_WIDTH
C_HEADS = 16
HEAD_F = 128
HEAD_I = D_MODEL // C_HEADS
C_FDIM = C_HEADS * HEAD_F
C_IDIM = C_HEADS * HEAD_I
IN_ODD = 2 * C_FDIM + 2 * C_IDIM
HGRN_BLK = 32
F_MIN = 1e-30
D_FF = 5632
CONV_W = 3
EPS = 1e-6

kernel_name = "mla_pool_hgrn2_convffn_stream_step"

F32 = jnp.float32


def rmsnorm(x, g):
    xf = x.astype(F32)
    y = xf * lax.rsqrt(jnp.mean(xf * xf, axis=-1, keepdims=True) + EPS)
    return (y * g.astype(F32)).astype(x.dtype)


def rope_angles(pos):
    inv = ROPE_THETA ** (-jnp.arange(0, ROPE_DIM, 2, dtype=F32) / ROPE_DIM)
    ang = pos.astype(F32)[:, None] * inv[None, :]
    return jnp.cos(ang), jnp.sin(ang)


def apply_rope(x, cos, sin):
    xf = x.astype(F32)
    x1, x2 = jnp.split(xf, 2, axis=-1)
    return jnp.concatenate([x1 * cos - x2 * sin, x1 * sin + x2 * cos], axis=-1).astype(x.dtype)


def mla_attend(q_lat, q_pe, lat, k_pe, q_pos, k_pos):
    s = (jnp.einsum('bqhr,bkr->bhqk', q_lat, lat) +
         jnp.einsum('bqhe,bke->bhqk', q_pe, k_pe)).astype(F32) * MLA_SCALE
    visible = (k_pos[None, :] // CHUNK) <= (q_pos[:, None] // CHUNK)
    s = jnp.where(visible[None, None], s, MASK_NEG)
    p = jax.nn.softmax(s, axis=-1).astype(lat.dtype)
    return jnp.einsum('bhqk,bkr->bqhr', p, lat)


def pool_mix(z, z_past, w_pool, pool_scale):
    B, L, C = z.shape
    ext = z if z_past is None else jnp.concatenate([z_past.astype(z.dtype), z], axis=1)
    P = ext.shape[1] - L
    cs = jnp.concatenate([jnp.zeros((B, 1, C), F32), jnp.cumsum(ext.astype(F32), axis=1)], axis=1)
    hi = np.arange(P + 1, P + L + 1)
    outs = []
    for gi, w in enumerate(POOL_WINDOWS):
        lo = np.maximum(hi - w, 0)
        sl = slice(gi * POOL_GROUP_DIM, (gi + 1) * POOL_GROUP_DIM)
        cnt = jnp.asarray(hi - lo, F32)[None, :, None]
        mean = (cs[:, hi, sl] - cs[:, lo, sl]) / cnt
        outs.append(mean - z[..., sl].astype(F32))
    p = jnp.stack(outs, axis=2).astype(z.dtype)
    y = jnp.einsum('blgc,gcd->blgd', p, w_pool).reshape(B, L, C) * pool_scale
    return y, ext


def even_mixer(u, pos, lat_past, kpe_past, pool_past, w_in, g_qa, w_qb, g_kva, w_uk, w_uv,
               w_pool, pool_scale, w_out):
    B, L, _ = u.shape
    h = u @ w_in
    o1, o2, o3 = Q_RANK, Q_RANK + KV_RANK, Q_RANK + KV_RANK + ROPE_DIM
    c_q, c_kv, k_pe, z = h[..., :o1], h[..., o1:o2], h[..., o2:o3], h[..., o3:]
    q = (rmsnorm(c_q, g_qa) @ w_qb).reshape(B, L, MLA_HEADS, NOPE_DIM + ROPE_DIM)
    cos, sin = rope_angles(pos)
    q_pe = apply_rope(q[..., NOPE_DIM:], cos[:, None], sin[:, None])
    q_lat = jnp.einsum('blhd,rhd->blhr', q[..., :NOPE_DIM], w_uk)
    lat = rmsnorm(c_kv, g_kva)
    k_pe = apply_rope(k_pe, cos, sin)
    if lat_past is None:
        keys_lat, keys_pe, k_pos = lat, k_pe, pos
    else:
        keys_lat = jnp.concatenate([lat_past.astype(lat.dtype), lat], axis=1)
        keys_pe = jnp.concatenate([kpe_past.astype(k_pe.dtype), k_pe], axis=1)
        k_pos = jnp.arange(lat_past.shape[1] + L)
    if L > MLA_QBLK:
        nb = L // MLA_QBLK
        def blk(a):
            return a.reshape((B, nb, MLA_QBLK) + a.shape[2:]).swapaxes(0, 1)
        o = lax.map(lambda t: mla_attend(t[0], t[1], keys_lat, keys_pe, t[2], k_pos),
                    (blk(q_lat), blk(q_pe), pos.reshape(nb, MLA_QBLK)))
        o_lat = o.swapaxes(0, 1).reshape(B, L, MLA_HEADS, KV_RANK)
    else:
        o_lat = mla_attend(q_lat, q_pe, keys_lat, keys_pe, pos, k_pos)
    y_mla = jnp.einsum('blhr,rhd->blhd', o_lat, w_uv).reshape(B, L, MLA_HEADS * V_DIM)
    y_pool, z_ext = pool_mix(z, pool_past, w_pool, pool_scale)
    y = jnp.concatenate([y_mla, y_pool], axis=-1) @ w_out
    return y, lat, k_pe, z_ext[:, -POOL_KEEP:]


def gla_chunkwise(q, k, v, log_f, S0):
    B, L, H, Dk = q.shape
    Dv = v.shape[-1]
    blk = math.gcd(L, HGRN_BLK)
    n = L // blk
    causal = np.tril(np.ones((blk, blk), dtype=bool))

    def blocks(a):
        return a.reshape(B, n, blk, H, a.shape[-1]).swapaxes(0, 1)

    def step(S, inp):
        qb, kb, vb, gb = inp
        b = jnp.cumsum(gb, axis=1)
        o_inter = jnp.einsum('bthk,bhkv->bthv', qb * jnp.exp(b), S)
        diff = b[:, :, None] - b[:, None, :]
        decay = jnp.exp(jnp.where(causal[None, :, :, None, None], diff, MASK_NEG))
        A = jnp.einsum('bthk,btshk,bshk->bhts', qb, decay, kb)
        o_intra = jnp.einsum('bhts,bshv->bthv', A, vb)
        b_last = b[:, -1]
        S_new = jnp.exp(b_last)[..., None] * S + jnp.einsum(
            'bshk,bshv->bhkv', kb * jnp.exp(b_last[:, None] - b), vb)
        return S_new, o_inter + o_intra

    S, o = lax.scan(step, S0, (blocks(q), blocks(k), blocks(v), blocks(log_f)))
    return o.swapaxes(0, 1).reshape(B, L, H, Dv), S


def odd_mixer(u, S0, lb, w_in, g_onorm, w_out):
    B, L, _ = u.shape
    h = u @ w_in
    q = jax.nn.silu(h[..., :C_FDIM]).astype(F32)
    fz = h[..., C_FDIM:2 * C_FDIM].astype(F32)
    v = h[..., 2 * C_FDIM:2 * C_FDIM + C_IDIM].astype(F32)
    g = h[..., 2 * C_FDIM + C_IDIM:]
    f = lb + (1.0 - lb) * jax.nn.sigmoid(fz)
    log_f = jnp.log(jnp.maximum(f, F_MIN))
    k = 1.0 - f
    o, S = gla_chunkwise(q.reshape(B, L, C_HEADS, HEAD_F), k.reshape(B, L, C_HEADS, HEAD_F),
                         v.reshape(B, L, C_HEADS, HEAD_I), log_f.reshape(B, L, C_HEADS, HEAD_F),
                         S0.astype(F32))
    o = rmsnorm(o, g_onorm).reshape(B, L, C_IDIM).astype(u.dtype) * jax.nn.silu(g)
    return o @ w_out, S


def conv_ffn(u, conv_past, w_up, conv_w, conv_b, w_down):
    L = u.shape[1]
    h = u @ w_up
    ext = jnp.concatenate([conv_past.astype(h.dtype), h], axis=1)
    c = conv_b + sum(ext[:, j:j + L] * conv_w[j] for j in range(CONV_W))
    a, b = jnp.split(c, 2, axis=-1)
    return (jax.nn.silu(a) * b) @ w_down, ext[:, -(CONV_W - 1):]


def setup_inputs(seed: int = 0) -> dict:
    key = jax.random.key(seed)
    ks = jax.random.split(key, 32)
    nrm = lambda k, shape, s: jax.random.normal(k, shape, F32) * s
    return {
        "x_prompt": nrm(ks[0], (BATCH, SEQ, D_MODEL), 1.0),
        "x_sample": nrm(ks[1], (DEC_BATCH, DEC_SEQ, D_MODEL), 1.0),
        "cache_mla_latent": nrm(ks[2], (N_EVEN, DEC_BATCH, PAST_LEN, KV_RANK), 1.0),
        "cache_mla_krope": nrm(ks[3], (N_EVEN, DEC_BATCH, PAST_LEN, ROPE_DIM), 1.0),
        "state_pool": nrm(ks[4], (N_EVEN, DEC_BATCH, POOL_KEEP, POOL_WIDTH), 1.0),
        "state_hgrn": nrm(ks[5], (N_ODD, DEC_BATCH, C_HEADS, HEAD_F, HEAD_I), 0.5),
        "state_ffn_conv": nrm(ks[6], (DEPTH, DEC_BATCH, CONV_W - 1, 2 * D_FF), 1.0),
        "g_mix": 1.0 + nrm(ks[7], (DEPTH, D_MODEL), 0.05),
        "g_ffn": 1.0 + nrm(ks[8], (DEPTH, D_MODEL), 0.05),
        "g_final": 1.0 + nrm(ks[9], (D_MODEL,), 0.05),
        "w_in_a": nrm(ks[10], (N_EVEN, D_MODEL, IN_EVEN), D_MODEL ** -0.5),
        "g_qa": 1.0 + nrm(ks[11], (N_EVEN, Q_RANK), 0.05),
        "w_qb": nrm(ks[12], (N_EVEN, Q_RANK, MLA_HEADS * (NOPE_DIM + ROPE_DIM)), Q_RANK ** -0.5),
        "g_kva": 1.0 + nrm(ks[13], (N_EVEN, KV_RANK), 0.05),
        "w_uk": nrm(ks[14], (N_EVEN, KV_RANK, MLA_HEADS, NOPE_DIM), KV_RANK ** -0.5),
        "w_uv": nrm(ks[15], (N_EVEN, KV_RANK, MLA_HEADS, V_DIM), KV_RANK ** -0.5),
        "w_pool": nrm(ks[16], (N_EVEN, POOL_GROUPS, POOL_GROUP_DIM, POOL_GROUP_DIM), POOL_GROUP_DIM ** -0.5),
        "pool_scale": 1.0 + nrm(ks[17], (N_EVEN, POOL_WIDTH), 0.1),
        "w_out_a": nrm(ks[18], (N_EVEN, MLA_HEADS * V_DIM + POOL_WIDTH, D_MODEL), D_MODEL ** -0.5),
        "w_in_c": nrm(ks[19], (N_ODD, D_MODEL, IN_ODD), D_MODEL ** -0.5),
        "lb_param": nrm(ks[20], (N_ODD, C_FDIM), 0.5),
        "g_onorm": 1.0 + nrm(ks[21], (N_ODD, HEAD_I), 0.05),
        "w_out_c": nrm(ks[22], (N_ODD, C_IDIM, D_MODEL), C_IDIM ** -0.5),
        "w_up": nrm(ks[23], (DEPTH, D_MODEL, 2 * D_FF), D_MODEL ** -0.5),
        "conv_w": nrm(ks[24], (DEPTH, CONV_W, 2 * D_FF), CONV_W ** -0.5),
        "conv_b": nrm(ks[25], (DEPTH, 2 * D_FF), 0.02),
        "w_down": nrm(ks[26], (DEPTH, D_FF, D_MODEL), D_FF ** -0.5),
    }


def reference(x_prompt, x_sample, cache_mla_latent, cache_mla_krope, state_pool, state_hgrn,
              state_ffn_conv, g_mix, g_ffn, g_final, w_in_a, g_qa, w_qb, g_kva, w_uk, w_uv,
              w_pool, pool_scale, w_out_a, w_in_c, lb_param, g_onorm, w_out_c, w_up, conv_w,
              conv_b, w_down):
    Bp, Lp, _ = x_prompt.shape
    Bs, Ls, _ = x_sample.shape
    past = cache_mla_latent.shape[2]
    pos_p = jnp.arange(Lp)
    pos_s = past + jnp.arange(Ls)
    sm = jax.nn.softmax(lb_param.astype(F32), axis=0)
    lbs = jnp.clip(jnp.cumsum(sm, axis=0) - sm[0:1], 0.0, 1.0)

    xp, xs = x_prompt, x_sample
    lat_p, kpe_p, pool_p, hg_p, cv_p = [], [], [], [], []
    lat_s, kpe_s, pool_s, hg_s, cv_s = [], [], [], [], []
    for layer in range(DEPTH):
        i = layer // 2
        up, us = rmsnorm(xp, g_mix[layer]), rmsnorm(xs, g_mix[layer])
        if layer % 2 == 0:
            wa = (w_in_a[i], g_qa[i], w_qb[i], g_kva[i], w_uk[i], w_uv[i], w_pool[i], pool_scale[i], w_out_a[i])
            yp, a, b, c = even_mixer(up, pos_p, None, None, None, *wa)
            lat_p.append(a); kpe_p.append(b); pool_p.append(c)
            ys, a, b, c = even_mixer(us, pos_s, cache_mla_latent[i], cache_mla_krope[i], state_pool[i], *wa)
            lat_s.append(a); kpe_s.append(b); pool_s.append(c)
        else:
            wc = (lbs[i], w_in_c[i], g_onorm[i], w_out_c[i])
            S0 = jnp.zeros((Bp, C_HEADS, HEAD_F, HEAD_I), F32)
            yp, Sp = odd_mixer(up, S0, *wc)
            ys, Ss = odd_mixer(us, state_hgrn[i], *wc)
            hg_p.append(Sp); hg_s.append(Ss)
        xp = xp + yp
        xs = xs + ys
        wf = (w_up[layer], conv_w[layer], conv_b[layer], w_down[layer])
        fp, cp = conv_ffn(rmsnorm(xp, g_ffn[layer]), jnp.zeros((Bp, CONV_W - 1, 2 * D_FF), xp.dtype), *wf)
        fs, cs = conv_ffn(rmsnorm(xs, g_ffn[layer]), state_ffn_conv[layer], *wf)
        cv_p.append(cp); cv_s.append(cs)
        xp = xp + fp
        xs = xs + fs

    y_prompt = rmsnorm(xp, g_final)
    y_sample = rmsnorm(xs, g_final)
    return (y_prompt, y_sample,
            jnp.stack(lat_p), jnp.stack(kpe_p), jnp.stack(pool_p), jnp.stack(hg_p), jnp.stack(cv_p),
            jnp.stack(lat_s), jnp.stack(kpe_s), jnp.stack(pool_s), jnp.stack(hg_s), jnp.stack(cv_s))
```

```python
import functools

import jax
import jax.numpy as jnp
from jax import lax
from jax.experimental import pallas as pl
from jax.experimental.pallas import tpu as pltpu

D_MODEL = 2048
BATCH = 8
SEQ = 2048
DEPTH = 4
DEC_BATCH = 32
DEC_SEQ = 64
PAST_LEN = 4096
CHUNK = 64
MLA_HEADS = 8
Q_RANK = 512
KV_RANK = 512
NOPE_DIM = 128
ROPE_DIM = 64
V_DIM = 128
ROPE_THETA = 10000.0
MLA_SCALE = (NOPE_DIM + ROPE_DIM) ** -0.5
MASK_NEG = -1e30
POOL_WINDOWS = (2, 4, 8, 16)
POOL_WIDTH = D_MODEL - MLA_HEADS * V_DIM
POOL_GROUP_DIM = POOL_WIDTH // len(POOL_WINDOWS)
POOL_KEEP = max(POOL_WINDOWS) - 1
C_HEADS = 16
HEAD_F = 128
HEAD_I = D_MODEL // C_HEADS
F_MIN = 1e-30
D_FF = 5632
CONV_W = 3
EPS = 1e-6

F32 = jnp.float32
BF16 = jnp.bfloat16

LANES = 128
SUBLANES = 8
VMEM_LIMIT = 56 * 1024 * 1024

IN_EVEN_PAD = Q_RANK + KV_RANK + POOL_WIDTH + LANES
Q_NOPE_ALL = MLA_HEADS * NOPE_DIM
GLA_CHUNK = 64
GLA_SUB = SUBLANES
FFN_TN = 512


def _params(*sem):
    return pltpu.CompilerParams(dimension_semantics=sem, vmem_limit_bytes=VMEM_LIMIT)


def _dot(a, b):
    return jnp.dot(a, b, preferred_element_type=F32)


def _dot_nt(a, b):
    return lax.dot_general(a, b, (((1,), (1,)), ((), ())), preferred_element_type=F32)


def _dot_tn(a, b):
    return lax.dot_general(a, b, (((0,), (0,)), ((), ())), preferred_element_type=F32)


def _rms(x, g):
    return x * lax.rsqrt(jnp.mean(x * x, axis=-1, keepdims=True) + EPS) * g


def _norm_kernel(x_ref, g_ref, xn_ref):
    xn_ref[...] = _rms(x_ref[...], g_ref[...]).astype(xn_ref.dtype)


def _norm(x, g, tm=512):
    T = x.shape[0]
    return pl.pallas_call(
        _norm_kernel,
        out_shape=jax.ShapeDtypeStruct((T, D_MODEL), BF16),
        grid=(T // tm,),
        in_specs=[pl.BlockSpec((tm, D_MODEL), lambda i: (i, 0)),
                  pl.BlockSpec((1, D_MODEL), lambda i: (0, 0))],
        out_specs=pl.BlockSpec((tm, D_MODEL), lambda i: (i, 0)),
        compiler_params=_params("parallel"),
        name="norm0",
    )(x, g)


def _rope(s, cos, sin, first_half):
    swapped = jnp.where(first_half, pltpu.roll(s, 96, 1), pltpu.roll(s, 32, 1))
    return s * cos + swapped * sin


def _even_in_kernel(xn_ref, w_ref, gqa_ref, gkva_ref, wqb_ref, wuk_ref, cos_ref, sin_ref,
                    lat_ref, latb_ref, kpe_ref, kpeb_ref, z_ref, qlat_ref, qpe_ref):
    tm = xn_ref.shape[0]
    acc = _dot(xn_ref[...], w_ref[...])
    o1, o2, o3 = Q_RANK, Q_RANK + KV_RANK, Q_RANK + KV_RANK + POOL_WIDTH
    z_ref[...] = acc[:, o2:o3]
    lat = _rms(acc[:, o1:o2], gkva_ref[...])
    lat_ref[...] = lat
    latb_ref[...] = lat.astype(BF16)
    cos = cos_ref[...]
    sin = sin_ref[...]
    lane = lax.broadcasted_iota(jnp.int32, (tm, LANES), 1)
    first_half = (lane % ROPE_DIM) < (ROPE_DIM // 2)
    kr = _rope(acc[:, o3:o3 + LANES], cos, sin, first_half)
    kpe_ref[...] = kr[:, :ROPE_DIM]
    kpeb_ref[...] = kr[:, :ROPE_DIM].astype(BF16)
    cqn = _rms(acc[:, :o1], gqa_ref[...]).astype(BF16)
    q = _dot(cqn, wqb_ref[...])
    for s in range(MLA_HEADS // 2):
        lo = Q_NOPE_ALL + LANES * s
        r = (_rope(q[:, lo:lo + LANES], cos, sin, first_half) * MLA_SCALE).astype(BF16)
        qpe_ref[2 * s] = r[:, :ROPE_DIM]
        qpe_ref[2 * s + 1] = r[:, ROPE_DIM:]
    for h in range(MLA_HEADS):
        ql = _dot(q[:, NOPE_DIM * h:NOPE_DIM * (h + 1)].astype(BF16), wuk_ref[h])
        qlat_ref[h] = (ql * MLA_SCALE).astype(BF16)


def _even_in(xn, w_in, g_qa, g_kva, w_qb, w_ukT, cos, sin, li, tm=256):
    T = xn.shape[0]
    row = lambda i: (i, 0)
    fix2 = lambda i: (0, 0)
    out_shape = (
        jax.ShapeDtypeStruct((T, KV_RANK), F32), jax.ShapeDtypeStruct((T, KV_RANK), BF16),
        jax.ShapeDtypeStruct((T, ROPE_DIM), F32), jax.ShapeDtypeStruct((T, ROPE_DIM), BF16),
        jax.ShapeDtypeStruct((T, POOL_WIDTH), F32),
        jax.ShapeDtypeStruct((MLA_HEADS, T, KV_RANK), BF16),
        jax.ShapeDtypeStruct((MLA_HEADS, T, ROPE_DIM), BF16),
    )
    return pl.pallas_call(
        _even_in_kernel,
        out_shape=out_shape,
        grid=(T // tm,),
        in_specs=[
            pl.BlockSpec((tm, D_MODEL), row),
            pl.BlockSpec((None, D_MODEL, IN_EVEN_PAD), lambda i: (li, 0, 0)),
            pl.BlockSpec((None, 1, Q_RANK), lambda i: (li, 0, 0)),
            pl.BlockSpec((None, 1, KV_RANK), lambda i: (li, 0, 0)),
            pl.BlockSpec((None, Q_RANK, MLA_HEADS * (NOPE_DIM + ROPE_DIM)), lambda i: (li, 0, 0)),
            pl.BlockSpec((None, MLA_HEADS, NOPE_DIM, KV_RANK), lambda i: (li, 0, 0, 0)),
            pl.BlockSpec((tm, LANES), row),
            pl.BlockSpec((tm, LANES), row),
        ],
        out_specs=(
            pl.BlockSpec((tm, KV_RANK), row), pl.BlockSpec((tm, KV_RANK), row),
            pl.BlockSpec((tm, ROPE_DIM), row), pl.BlockSpec((tm, ROPE_DIM), row),
            pl.BlockSpec((tm, POOL_WIDTH), row),
            pl.BlockSpec((MLA_HEADS, tm, KV_RANK), lambda i: (0, i, 0)),
            pl.BlockSpec((MLA_HEADS, tm, ROPE_DIM), lambda i: (0, i, 0)),
        ),
        compiler_params=_params("parallel"),
        name="even_in",
    )(xn, w_in, g_qa, g_kva, w_qb, w_ukT, cos, sin)


def _softmax_step(s, v_bf, m_ref, l_ref, acc_ref):
    m_prev = m_ref[...]
    m_new = jnp.maximum(m_prev, jnp.max(s, axis=1, keepdims=True))
    alpha = jnp.exp(m_prev - m_new)
    p = jnp.exp(s - m_new)
    l_ref[...] = alpha * l_ref[...] + jnp.sum(p, axis=1, keepdims=True)
    acc_ref[...] = alpha * acc_ref[...] + _dot(p.astype(BF16), v_bf)
    m_ref[...] = m_new


def _attn_init(m_ref, l_ref, acc_ref):
    m_ref[...] = jnp.full(m_ref.shape, MASK_NEG, F32)
    l_ref[...] = jnp.zeros(l_ref.shape, F32)
    acc_ref[...] = jnp.zeros(acc_ref.shape, F32)


def _attn_finish(wuv_ref, y_ref, l_ref, acc_ref, tq):
    o = (acc_ref[...] / l_ref[...]).astype(BF16)
    for h in range(MLA_HEADS):
        y_ref[:, V_DIM * h:V_DIM * (h + 1)] = _dot(o[h * tq:(h + 1) * tq], wuv_ref[h]).astype(BF16)


def _attn_prompt_kernel(qlat_ref, qpe_ref, k_ref, kp_ref, wuv_ref, y_ref, m_ref, l_ref, acc_ref, *, tq):
    i = pl.program_id(1)
    j = pl.program_id(2)
    rows = MLA_HEADS * tq

    @pl.when(j == 0)
    def _():
        _attn_init(m_ref, l_ref, acc_ref)

    @pl.when(j <= i)
    def _():
        k = k_ref[...]
        s = (_dot_nt(qlat_ref[...].reshape(rows, KV_RANK), k) +
             _dot_nt(qpe_ref[...].reshape(rows, ROPE_DIM), kp_ref[...]))
        row = lax.broadcasted_iota(jnp.int32, (rows, tq), 0)
        col = lax.broadcasted_iota(jnp.int32, (rows, tq), 1)
        q_chunk = (i * tq + row % tq) // CHUNK
        k_chunk = (j * tq + col) // CHUNK
        s = jnp.where(k_chunk <= q_chunk, s, MASK_NEG)
        _softmax_step(s, k, m_ref, l_ref, acc_ref)

    @pl.when(j == pl.num_programs(2) - 1)
    def _():
        _attn_finish(wuv_ref, y_ref, l_ref, acc_ref, tq)


def _attn_prompt(qlat, qpe, latb, kpeb, w_uvT, li, tq=256):
    T = latb.shape[0]
    nq = SEQ // tq
    qmap = lambda b, i, j: (0, b * nq + i, 0)
    kmap = lambda b, i, j: (b * nq + jnp.minimum(i, j), 0)
    rows = MLA_HEADS * tq
    return pl.pallas_call(
        functools.partial(_attn_prompt_kernel, tq=tq),
        out_shape=jax.ShapeDtypeStruct((T, MLA_HEADS * V_DIM), BF16),
        grid=(BATCH, nq, nq),
        in_specs=[
            pl.BlockSpec((MLA_HEADS, tq, KV_RANK), qmap),
            pl.BlockSpec((MLA_HEADS, tq, ROPE_DIM), qmap),
            pl.BlockSpec((tq, KV_RANK), kmap),
            pl.BlockSpec((tq, ROPE_DIM), kmap),
            pl.BlockSpec((None, MLA_HEADS, KV_RANK, V_DIM), lambda b, i, j: (li, 0, 0, 0)),
        ],
        out_specs=pl.BlockSpec((tq, MLA_HEADS * V_DIM), lambda b, i, j: (b * nq + i, 0)),
        scratch_shapes=[pltpu.VMEM((rows, 1), F32), pltpu.VMEM((rows, 1), F32),
                        pltpu.VMEM((rows, KV_RANK), F32)],
        compiler_params=_params("parallel", "parallel", "arbitrary"),
        name="attn_prompt",
    )(qlat, qpe, latb, kpeb, w_uvT)


def _attn_sample_kernel(qlat_ref, qpe_ref, ck_ref, ckp_ref, nk_ref, nkp_ref, wuv_ref, y_ref,
                        m_ref, l_ref, acc_ref):
    j = pl.program_id(1)
    last = pl.num_programs(1) - 1
    rows = MLA_HEADS * DEC_SEQ

    @pl.when(j == 0)
    def _():
        _attn_init(m_ref, l_ref, acc_ref)

    def step(k, kp):
        s = (_dot_nt(qlat_ref[...].reshape(rows, KV_RANK), k) +
             _dot_nt(qpe_ref[...].reshape(rows, ROPE_DIM), kp))
        _softmax_step(s, k, m_ref, l_ref, acc_ref)

    @pl.when(j < last)
    def _():
        step(ck_ref[...].astype(BF16), ckp_ref[...].astype(BF16))

    @pl.when(j == last)
    def _():
        step(nk_ref[...], nkp_ref[...])
        _attn_finish(wuv_ref, y_ref, l_ref, acc_ref, DEC_SEQ)


def _attn_sample(qlat, qpe, cache_lat, cache_kpe, latb, kpeb, w_uvT, li, tk=1024):
    assert PAST_LEN % CHUNK == 0 and DEC_SEQ <= CHUNK and PAST_LEN % tk == 0
    T = latb.shape[0]
    nc = PAST_LEN // tk
    qmap = lambda b, j: (0, b, 0)
    cmap = lambda b, j: (li, b, jnp.minimum(j, nc - 1), 0)
    rows = MLA_HEADS * DEC_SEQ
    return pl.pallas_call(
        _attn_sample_kernel,
        out_shape=jax.ShapeDtypeStruct((T, MLA_HEADS * V_DIM), BF16),
        grid=(DEC_BATCH, nc + 1),
        in_specs=[
            pl.BlockSpec((MLA_HEADS, DEC_SEQ, KV_RANK), qmap),
            pl.BlockSpec((MLA_HEADS, DEC_SEQ, ROPE_DIM), qmap),
            pl.BlockSpec((None, None, tk, KV_RANK), cmap),
            pl.BlockSpec((None, None, tk, ROPE_DIM), cmap),
            pl.BlockSpec((DEC_SEQ, KV_RANK), lambda b, j: (b, 0)),
            pl.BlockSpec((DEC_SEQ, ROPE_DIM), lambda b, j: (b, 0)),
            pl.BlockSpec((None, MLA_HEADS, KV_RANK, V_DIM), lambda b, j: (li, 0, 0, 0)),
        ],
        out_specs=pl.BlockSpec((DEC_SEQ, MLA_HEADS * V_DIM), lambda b, j: (b, 0)),
        scratch_shapes=[pltpu.VMEM((rows, 1), F32), pltpu.VMEM((rows, 1), F32),
                        pltpu.VMEM((rows, KV_RANK), F32)],
        compiler_params=_params("parallel", "arbitrary"),
        name="attn_sample",
    )(qlat, qpe, cache_lat, cache_kpe, latb, kpeb, w_uvT)


POOL_HALO = 2 * SUBLANES


def _pool_kernel(*refs, tp, from_state):
    if from_state:
        z_ref, past_ref, w_ref, scale_ref, y_ref, ext_ref = refs
        ext_ref[0:1, :] = jnp.zeros((1, POOL_WIDTH), F32)
        ext_ref[1:POOL_HALO, :] = past_ref[...]
    else:
        z_ref, w_ref, scale_ref, y_ref, ext_ref = refs
        t = pl.program_id(1)

        @pl.when(t == 0)
        def _():
            ext_ref[0:POOL_HALO, :] = jnp.zeros((POOL_HALO, POOL_WIDTH), F32)

    ext_ref[POOL_HALO:POOL_HALO + tp, :] = z_ref[...]
    for gi, w in enumerate(POOL_WINDOWS):
        lanes = slice(POOL_GROUP_DIM * gi, POOL_GROUP_DIM * (gi + 1))
        tok = ext_ref[POOL_HALO:POOL_HALO + tp, lanes]
        acc = tok
        for d in range(1, w):
            acc = acc + ext_ref[POOL_HALO - d:POOL_HALO - d + tp, lanes]
        if from_state:
            mean = acc / float(w)
        else:
            pos = t * tp + lax.broadcasted_iota(jnp.int32, (tp, 1), 0)
            mean = acc / jnp.minimum(pos + 1, w).astype(F32)
        p = (mean - tok).astype(BF16)
        y_ref[:, lanes] = (_dot(p, w_ref[gi]) * scale_ref[:, lanes]).astype(BF16)
    if not from_state:
        ext_ref[0:POOL_HALO, :] = ext_ref[tp:tp + POOL_HALO, :]


def _pool_prompt(z, w_pool, pool_scale, li, tp=512):
    T = z.shape[0]
    nt = SEQ // tp
    return pl.pallas_call(
        functools.partial(_pool_kernel, tp=tp, from_state=False),
        out_shape=jax.ShapeDtypeStruct((T, POOL_WIDTH), BF16),
        grid=(BATCH, nt),
        in_specs=[
            pl.BlockSpec((tp, POOL_WIDTH), lambda b, t: (b * nt + t, 0)),
            pl.BlockSpec((None, len(POOL_WINDOWS), POOL_GROUP_DIM, POOL_GROUP_DIM), lambda b, t: (li, 0, 0, 0)),
            pl.BlockSpec((None, 1, POOL_WIDTH), lambda b, t: (li, 0, 0)),
        ],
        out_specs=pl.BlockSpec((tp, POOL_WIDTH), lambda b, t: (b * nt + t, 0)),
        scratch_shapes=[pltpu.VMEM((POOL_HALO + tp, POOL_WIDTH), F32)],
        compiler_params=_params("arbitrary", "arbitrary"),
        name="pool_prompt",
    )(z, w_pool, pool_scale)


def _pool_sample(z, state_pool, w_pool, pool_scale, li):
    T = z.shape[0]
    return pl.pallas_call(
        functools.partial(_pool_kernel, tp=DEC_SEQ, from_state=True),
        out_shape=jax.ShapeDtypeStruct((T, POOL_WIDTH), BF16),
        grid=(DEC_BATCH,),
        in_specs=[
            pl.BlockSpec((DEC_SEQ, POOL_WIDTH), lambda b: (b, 0)),
            pl.BlockSpec((None, None, POOL_KEEP, POOL_WIDTH), lambda b: (li, b, 0, 0)),
            pl.BlockSpec((None, len(POOL_WINDOWS), POOL_GROUP_DIM, POOL_GROUP_DIM), lambda b: (li, 0, 0, 0)),
            pl.BlockSpec((None, 1, POOL_WIDTH), lambda b: (li, 0, 0)),
        ],
        out_specs=pl.BlockSpec((DEC_SEQ, POOL_WIDTH), lambda b: (b, 0)),
        scratch_shapes=[pltpu.VMEM((POOL_HALO + DEC_SEQ, POOL_WIDTH), F32)],
        compiler_params=_params("parallel"),
        name="pool_sample",
    )(z, state_pool, w_pool, pool_scale)


def _proj_res_norm_kernel(*refs, n_a, head_major, emit_x):
    a_refs = refs[:n_a]
    w_refs = refs[n_a:2 * n_a]
    x_ref, g_ref = refs[2 * n_a:2 * n_a + 2]
    outs = refs[2 * n_a + 2:]
    if head_major:
        (a_ref,), (w_ref,) = a_refs, w_refs
        xo_ref, xn_ref, cat_ref = outs
        for h in range(C_HEADS):
            cat_ref[:, HEAD_I * h:HEAD_I * (h + 1)] = a_ref[h]
        acc = _dot(cat_ref[...], w_ref[...])
    else:
        xo_ref, xn_ref = outs if emit_x else (None, outs[0])
        acc = _dot(a_refs[0][...], w_refs[0][...])
        for a_ref, w_ref in zip(a_refs[1:], w_refs[1:]):
            acc = acc + _dot(a_ref[...], w_ref[...])
    xnew = x_ref[...] + acc
    if xo_ref is not None:
        xo_ref[...] = xnew
    xn_ref[...] = _rms(xnew, g_ref[...]).astype(xn_ref.dtype)


def _proj_res_norm(a_list, w, x, g, li, head_major=False, tm=512):
    T = x.shape[0]
    row = lambda i: (i, 0)
    n_a = len(a_list)
    in_specs = []
    for a in a_list:
        if head_major:
            in_specs.append(pl.BlockSpec((C_HEADS, tm, HEAD_I), lambda i: (0, i, 0)))
        else:
            in_specs.append(pl.BlockSpec((tm, a.shape[1]), row))
    for kb, a in enumerate(a_list):
        kdim = C_HEADS * HEAD_I if head_major else a.shape[1]
        in_specs.append(pl.BlockSpec((None, kdim, D_MODEL), lambda i, kb=kb: (li, kb, 0)))
    in_specs += [pl.BlockSpec((tm, D_MODEL), row), pl.BlockSpec((1, D_MODEL), lambda i: (0, 0))]
    scratch = [pltpu.VMEM((tm, D_MODEL), BF16)] if head_major else []
    return pl.pallas_call(
        functools.partial(_proj_res_norm_kernel, n_a=n_a, head_major=head_major, emit_x=True),
        out_shape=(jax.ShapeDtypeStruct((T, D_MODEL), F32), jax.ShapeDtypeStruct((T, D_MODEL), BF16)),
        grid=(T // tm,),
        in_specs=in_specs,
        out_specs=(pl.BlockSpec((tm, D_MODEL), row), pl.BlockSpec((tm, D_MODEL), row)),
        scratch_shapes=scratch,
        compiler_params=_params("parallel"),
        name="proj_res_norm",
    )(*a_list, *([w] * n_a), x, g)


def _odd_in_kernel(xn_ref, w_ref, lbp_ref, o_ref, *, li):
    s = pl.program_id(0)
    h = _dot(xn_ref[...], w_ref[...])
    sg = jax.nn.sigmoid(h)
    p = lbp_ref[...]
    e = jnp.exp(p - jnp.max(p, axis=0, keepdims=True))
    sm = e / jnp.sum(e, axis=0, keepdims=True)
    lb = jnp.clip(jnp.sum(sm[:li + 1], axis=0, keepdims=True) - sm[0:1], 0.0, 1.0)
    f = lb + (1.0 - lb) * sg
    out = jnp.where(s == 1, f, jnp.where(s == 2, h, h * sg))
    for hd in range(C_HEADS):
        o_ref[hd] = out[:, HEAD_I * hd:HEAD_I * (hd + 1)]


def _odd_in(xn, w_in_c, lb_param, li, tm=512):
    T = xn.shape[0]
    return pl.pallas_call(
        functools.partial(_odd_in_kernel, li=li),
        out_shape=jax.ShapeDtypeStruct((4, C_HEADS, T, HEAD_I), F32),
        grid=(4, T // tm),
        in_specs=[
            pl.BlockSpec((tm, D_MODEL), lambda s, i: (i, 0)),
            pl.BlockSpec((None, D_MODEL, D_MODEL), lambda s, i: (li, 0, s)),
            pl.BlockSpec(lb_param.shape, lambda s, i: (0, 0)),
        ],
        out_specs=pl.BlockSpec((None, C_HEADS, tm, HEAD_I), lambda s, i: (s, 0, i, 0)),
        compiler_params=_params("parallel", "parallel"),
        name="odd_in",
    )(xn, w_in_c, lb_param)


def _gla_kernel(*refs, has_s0):
    if has_s0:
        q_ref, f_ref, v_ref, gs_ref, gon_ref, s0_ref, og_ref, sout_ref, st_ref = refs
    else:
        q_ref, f_ref, v_ref, gs_ref, gon_ref, og_ref, sout_ref, st_ref = refs
    C = GLA_CHUNK
    n = pl.program_id(1)
    last = pl.num_programs(1) - 1

    @pl.when(n == 0)
    def _():
        if has_s0:
            def load(h, c):
                st_ref[h] = s0_ref[h].T
                return c
            lax.fori_loop(0, C_HEADS, load, 0)
        else:
            st_ref[...] = jnp.zeros(st_ref.shape, F32)

    r2 = lax.broadcasted_iota(jnp.int32, (C, C), 0)
    c2 = lax.broadcasted_iota(jnp.int32, (C, C), 1)
    tri = (c2 <= r2).astype(BF16)
    ones = jnp.ones((HEAD_F, LANES), BF16)
    row_in_sub = lax.broadcasted_iota(jnp.int32, (C, HEAD_F), 0) % GLA_SUB
    gon = gon_ref[...]
    levels = []
    m = GLA_SUB
    while m < C:
        keep = (r2 // (2 * m) == c2 // (2 * m)) & (r2 % (2 * m) >= m) & (c2 % (2 * m) < m)
        levels.append((m, keep))
        m *= 2

    def bcast_row(a, period, r):
        a3 = a.reshape(C // period, period, a.shape[-1])
        return jnp.broadcast_to(a3[:, r:r + 1, :], a3.shape).reshape(C, a.shape[-1])

    def head(h, carry):
        q = q_ref[h]
        f = f_ref[h]
        v = v_ref[h]
        g = jnp.log(jnp.maximum(f, F_MIN))
        kk = 1.0 - f
        g1 = g.astype(BF16)
        r1 = g - g1.astype(F32)
        g2 = r1.astype(BF16)
        g3 = (r1 - g2.astype(F32)).astype(BF16)
        b = _dot(tri, g1) + _dot(tri, g2) + _dot(tri, g3)
        st = st_ref[h]
        v_bf = v.astype(BF16)
        o = _dot_nt((q * jnp.exp(b)).astype(BF16), st.astype(BF16))
        b_last = b[C - 1:C, :]
        k_dec = (kk * jnp.exp(b_last - b)).astype(BF16)
        st_new = st * jnp.exp(b_last) + _dot_tn(v_bf, k_dec)
        st_ref[h] = st_new

        @pl.when(n == last)
        def _():
            sout_ref[h] = st_new.T

        a = jnp.zeros((C, C), F32)
        for m, keep in levels:
            d = b - bcast_row(b, 2 * m, m - 1)
            qt = (q * jnp.exp(jnp.minimum(d, 0.0))).astype(BF16)
            kt = (kk * jnp.exp(jnp.minimum(-d, 0.0))).astype(BF16)
            a = a + jnp.where(keep, _dot_nt(qt, kt), 0.0)
        o = o + _dot(a.astype(BF16), v_bf)
        for s in range(GLA_SUB):
            e = jnp.exp(jnp.minimum(b - bcast_row(b, GLA_SUB, s), 0.0))
            p = jnp.where(row_in_sub >= s, q * bcast_row(kk, GLA_SUB, s) * e, 0.0)
            o = o + _dot(p.astype(BF16), ones) * bcast_row(v, GLA_SUB, s)
        y = _rms(o, gon) * gs_ref[h]
        og_ref[h] = y.astype(BF16)
        return carry

    lax.fori_loop(0, C_HEADS, head, 0)


def _gla(qfvg, g_onorm, s0, li, n_streams, seq):
    T = qfvg.shape[2]
    C = GLA_CHUNK
    nc = seq // C
    has_s0 = s0 is not None
    sec = lambda k: pl.BlockSpec((None, C_HEADS, C, HEAD_F), lambda b, n, k=k: (k, 0, b * nc + n, 0))
    in_specs = [sec(0), sec(1), sec(2), sec(3),
                pl.BlockSpec((None, 1, HEAD_I), lambda b, n: (li, 0, 0))]
    args = [qfvg, qfvg, qfvg, qfvg, g_onorm]
    if has_s0:
        in_specs.append(pl.BlockSpec((None, None, C_HEADS, HEAD_F, HEAD_I), lambda b, n: (li, b, 0, 0, 0)))
        args.append(s0)
    return pl.pallas_call(
        functools.partial(_gla_kernel, has_s0=has_s0),
        out_shape=(jax.ShapeDtypeStruct((C_HEADS, T, HEAD_I), BF16),
                   jax.ShapeDtypeStruct((n_streams, C_HEADS, HEAD_F, HEAD_I), F32)),
        grid=(n_streams, nc),
        in_specs=in_specs,
        out_specs=(pl.BlockSpec((C_HEADS, C, HEAD_I), lambda b, n: (0, b * nc + n, 0)),
                   pl.BlockSpec((None, C_HEADS, HEAD_F, HEAD_I), lambda b, n: (b, 0, 0, 0))),
        scratch_shapes=[pltpu.VMEM((C_HEADS, HEAD_I, HEAD_F), F32)],
        compiler_params=_params("parallel", "arbitrary"),
        name="gla",
    )(*args)


FFN_PAD = SUBLANES


def _ffn_up_kernel(*refs, tm, seg, from_state, tiles_per_stream):
    if from_state:
        (xn_ref, wa_ref, wb_ref, cwa_ref, cwb_ref, cba_ref, cbb_ref, pa_ref, pb_ref,
         act_ref, hla_ref, hlb_ref, ext_ref) = refs
        carry_ref = None
    else:
        (xn_ref, wa_ref, wb_ref, cwa_ref, cwb_ref, cba_ref, cbb_ref,
         act_ref, hla_ref, hlb_ref, ext_ref, carry_ref) = refs
        pa_ref = pb_ref = None
    i = pl.program_id(0)
    j = pl.program_id(1)
    nseg = tm // seg
    xn = xn_ref[...]

    def conv_half(hidx, w_ref, cw_ref, cb_ref, past_ref, hl_ref):
        h = _dot(xn, w_ref[...])
        for s in range(nseg):
            if from_state:
                ext_ref[hidx, s, FFN_PAD - 2:FFN_PAD, :] = past_ref[s]
            else:
                @pl.when(i % tiles_per_stream == 0)
                def _():
                    ext_ref[hidx, s, 0:FFN_PAD, :] = jnp.zeros((FFN_PAD, FFN_TN), F32)

                @pl.when(i % tiles_per_stream != 0)
                def _():
                    ext_ref[hidx, s, 0:FFN_PAD, :] = carry_ref[j, hidx]
            ext_ref[hidx, s, FFN_PAD:FFN_PAD + seg, :] = h[s * seg:(s + 1) * seg]
        if not from_state:
            carry_ref[j, hidx] = h[tm - FFN_PAD:tm]
        cw = cw_ref[...]
        cb = cb_ref[...]
        out = []
        for s in range(nseg):
            c = cb + (ext_ref[hidx, s, FFN_PAD - 2:FFN_PAD - 2 + seg, :] * cw[0:1] +
                      ext_ref[hidx, s, FFN_PAD - 1:FFN_PAD - 1 + seg, :] * cw[1:2] +
                      ext_ref[hidx, s, FFN_PAD:FFN_PAD + seg, :] * cw[2:3])
            out.append(c)
            hl_ref[s] = ext_ref[hidx, s, FFN_PAD + seg - 2:FFN_PAD + seg, :]
        return out

    ca = conv_half(0, wa_ref, cwa_ref, cba_ref, pa_ref, hla_ref)
    cb_ = conv_half(1, wb_ref, cwb_ref, cbb_ref, pb_ref, hlb_ref)
    for s in range(nseg):
        a = ca[s]
        act_ref[s * seg:(s + 1) * seg, :] = (a * jax.nn.sigmoid(a) * cb_[s]).astype(BF16)


def _ffn_up(xn, w_up, conv_w, conv_b, conv_past, layer, tm, seg):
    T = xn.shape[0]
    tn = FFN_TN
    nj = D_FF // tn
    nseg = tm // seg
    from_state = conv_past is not None
    wmap_a = lambda i, j: (layer, 0, j)
    wmap_b = lambda i, j: (layer, 0, nj + j)
    in_specs = [
        pl.BlockSpec((tm, D_MODEL), lambda i, j: (i, 0)),
        pl.BlockSpec((None, D_MODEL, tn), wmap_a), pl.BlockSpec((None, D_MODEL, tn), wmap_b),
        pl.BlockSpec((None, CONV_W, tn), wmap_a), pl.BlockSpec((None, CONV_W, tn), wmap_b),
        pl.BlockSpec((None, 1, tn), wmap_a), pl.BlockSpec((None, 1, tn), wmap_b),
    ]
    args = [xn, w_up, w_up, conv_w, conv_w, conv_b, conv_b]
    scratch = [pltpu.VMEM((2, nseg, FFN_PAD + seg, tn), F32)]
    if from_state:
        in_specs += [pl.BlockSpec((None, nseg, CONV_W - 1, tn), lambda i, j: (layer, i, 0, j)),
                     pl.BlockSpec((None, nseg, CONV_W - 1, tn), lambda i, j: (layer, i, 0, nj + j))]
        args += [conv_past, conv_past]
        tiles_per_stream = 1
    else:
        scratch.append(pltpu.VMEM((nj, 2, FFN_PAD, tn), F32))
        tiles_per_stream = SEQ // tm
    hl_shape = jax.ShapeDtypeStruct((T // seg, CONV_W - 1, D_FF), F32)
    hl_spec = pl.BlockSpec((nseg, CONV_W - 1, tn), lambda i, j: (i, 0, j))
    return pl.pallas_call(
        functools.partial(_ffn_up_kernel, tm=tm, seg=seg, from_state=from_state,
                          tiles_per_stream=tiles_per_stream),
        out_shape=(jax.ShapeDtypeStruct((T, D_FF), BF16), hl_shape, hl_shape),
        grid=(T // tm, nj),
        in_specs=in_specs,
        out_specs=(pl.BlockSpec((tm, tn), lambda i, j: (i, j)), hl_spec, hl_spec),
        scratch_shapes=scratch,
        compiler_params=_params("arbitrary", "arbitrary"),
        name="ffn_up",
    )(*args)


def _ffn_down_kernel(a_ref, w_ref, x_ref, g_ref, *outs, emit_x):
    if emit_x:
        xo_ref, xn_ref, acc_ref = outs
    else:
        xn_ref, acc_ref = outs
        xo_ref = None
    k = pl.program_id(1)
    part = _dot(a_ref[...], w_ref[...])

    @pl.when(k == 0)
    def _():
        acc_ref[...] = part

    @pl.when(k > 0)
    def _():
        acc_ref[...] += part

    @pl.when(k == pl.num_programs(1) - 1)
    def _():
        xnew = x_ref[...] + acc_ref[...]
        if xo_ref is not None:
            xo_ref[...] = xnew
        xn_ref[...] = _rms(xnew, g_ref[...]).astype(xn_ref.dtype)


def _ffn_down(act, w_down, x, g, layer, emit_x, xn_dtype, tm=512, nk=4):
    T = x.shape[0]
    tk = D_FF // nk
    row = lambda i, k: (i, 0)
    out_shape = [jax.ShapeDtypeStruct((T, D_MODEL), xn_dtype)]
    out_specs = [pl.BlockSpec((tm, D_MODEL), row)]
    if emit_x:
        out_shape.insert(0, jax.ShapeDtypeStruct((T, D_MODEL), F32))
        out_specs.insert(0, pl.BlockSpec((tm, D_MODEL), row))
    res = pl.pallas_call(
        functools.partial(_ffn_down_kernel, emit_x=emit_x),
        out_shape=tuple(out_shape),
        grid=(T // tm, nk),
        in_specs=[
            pl.BlockSpec((tm, tk), lambda i, k: (i, k)),
            pl.BlockSpec((None, tk, D_MODEL), lambda i, k: (layer, k, 0)),
            pl.BlockSpec((tm, D_MODEL), row),
            pl.BlockSpec((1, D_MODEL), lambda i, k: (0, 0)),
        ],
        out_specs=tuple(out_specs),
        scratch_shapes=[pltpu.VMEM((tm, D_MODEL), F32)],
        compiler_params=_params("parallel", "arbitrary"),
        name="ffn_down",
    )(act, w_down, x, g)
    return res if emit_x else (None, res[0])


def _rope_tables(pos):
    inv = ROPE_THETA ** (-jnp.arange(0, ROPE_DIM, 2, dtype=F32) / ROPE_DIM)
    ang = pos.astype(F32)[:, None] * inv[None, :]
    cos, sin = jnp.cos(ang), jnp.sin(ang)
    reps = LANES // ROPE_DIM
    return (jnp.tile(jnp.concatenate([cos, cos], axis=-1), (1, reps)),
            jnp.tile(jnp.concatenate([-sin, sin], axis=-1), (1, reps)))


def kernel(x_prompt, x_sample, cache_mla_latent, cache_mla_krope, state_pool, state_hgrn, state_ffn_conv,
           g_mix, g_ffn, g_final, w_in_a, g_qa, w_qb, g_kva, w_uk, w_uv, w_pool, pool_scale, w_out_a,
           w_in_c, lb_param, g_onorm, w_out_c, w_up, conv_w, conv_b, w_down):
    n_even = w_in_a.shape[0]
    n_odd = w_in_c.shape[0]
    Tp, Ts = BATCH * SEQ, DEC_BATCH * DEC_SEQ

    o1, o2, o3 = Q_RANK, Q_RANK + KV_RANK, Q_RANK + KV_RANK + ROPE_DIM
    w_in_e = jnp.concatenate(
        [w_in_a[..., :o2], w_in_a[..., o3:], w_in_a[..., o2:o3],
         jnp.zeros((n_even, D_MODEL, LANES - ROPE_DIM), w_in_a.dtype)], axis=-1).astype(BF16)
    w_qb4 = w_qb.reshape(n_even, Q_RANK, MLA_HEADS, NOPE_DIM + ROPE_DIM)
    w_qb_e = jnp.concatenate(
        [w_qb4[..., :NOPE_DIM].reshape(n_even, Q_RANK, Q_NOPE_ALL),
         w_qb4[..., NOPE_DIM:].reshape(n_even, Q_RANK, MLA_HEADS * ROPE_DIM)], axis=-1).astype(BF16)
    w_ukT = jnp.transpose(w_uk, (0, 2, 3, 1)).astype(BF16)
    w_uvT = jnp.transpose(w_uv, (0, 2, 1, 3)).astype(BF16)
    w_pool_b = w_pool.astype(BF16)
    w_out_a_b = w_out_a.astype(BF16)
    w_in_c_b = w_in_c.astype(BF16)
    w_out_c_b = w_out_c.astype(BF16)
    w_up_b = w_up.astype(BF16)
    w_down_b = w_down.astype(BF16)
    g_qa3 = g_qa[:, None, :]
    g_kva3 = g_kva[:, None, :]
    pool_scale3 = pool_scale[:, None, :]
    g_onorm3 = g_onorm[:, None, :]
    conv_b3 = conv_b[:, None, :]

    cos_p, sin_p = _rope_tables(jnp.tile(jnp.arange(SEQ), BATCH))
    cos_s, sin_s = _rope_tables(jnp.tile(PAST_LEN + jnp.arange(DEC_SEQ), DEC_BATCH))

    xp = x_prompt.reshape(Tp, D_MODEL)
    xs = x_sample.reshape(Ts, D_MODEL)
    xnp = _norm(xp, g_mix[0:1])
    xns = _norm(xs, g_mix[0:1])

    lat_p, kpe_p, pool_p, hg_p, cv_p = [], [], [], [], []
    lat_s, kpe_s, pool_s, hg_s, cv_s = [], [], [], [], []
    for layer in range(DEPTH):
        li = layer // 2
        g_next = g_ffn[layer:layer + 1]
        if layer % 2 == 0:
            wa = (w_in_e, g_qa3, g_kva3, w_qb_e, w_ukT)
            lat, latb, kpe, kpeb, z, qlat, qpe = _even_in(xnp, *wa, cos_p, sin_p, li)
            y_mla = _attn_prompt(qlat, qpe, latb, kpeb, w_uvT, li)
            y_pool = _pool_prompt(z, w_pool_b, pool_scale3, li)
            xp, xnp = _proj_res_norm([y_mla, y_pool], w_out_a_b, xp, g_next, li)
            lat_p.append(lat.reshape(BATCH, SEQ, KV_RANK))
            kpe_p.append(kpe.reshape(BATCH, SEQ, ROPE_DIM))
            pool_p.append(z.reshape(BATCH, SEQ, POOL_WIDTH)[:, SEQ - POOL_KEEP:])

            lat, latb, kpe, kpeb, z, qlat, qpe = _even_in(xns, *wa, cos_s, sin_s, li)
            y_mla = _attn_sample(qlat, qpe, cache_mla_latent, cache_mla_krope, latb, kpeb, w_uvT, li)
            y_pool = _pool_sample(z, state_pool, w_pool_b, pool_scale3, li)
            xs, xns = _proj_res_norm([y_mla, y_pool], w_out_a_b, xs, g_next, li)
            lat_s.append(lat.reshape(DEC_BATCH, DEC_SEQ, KV_RANK))
            kpe_s.append(kpe.reshape(DEC_BATCH, DEC_SEQ, ROPE_DIM))
            pool_s.append(z.reshape(DEC_BATCH, DEC_SEQ, POOL_WIDTH)[:, DEC_SEQ - POOL_KEEP:])
        else:
            qfvg = _odd_in(xnp, w_in_c_b, lb_param, li)
            og, s_fin = _gla(qfvg, g_onorm3, None, li, BATCH, SEQ)
            xp, xnp = _proj_res_norm([og], w_out_c_b, xp, g_next, li, head_major=True)
            hg_p.append(s_fin)

            qfvg = _odd_in(xns, w_in_c_b, lb_param, li)
            og, s_fin = _gla(qfvg, g_onorm3, state_hgrn, li, DEC_BATCH, DEC_SEQ)
            xs, xns = _proj_res_norm([og], w_out_c_b, xs, g_next, li, head_major=True)
            hg_s.append(s_fin)

        final = layer == DEPTH - 1
        g_after = g_final[None, :] if final else g_mix[layer + 1:layer + 2]
        xn_dtype = F32 if final else BF16

        tm_p = 1024
        act, hla, hlb = _ffn_up(xnp, w_up_b, conv_w, conv_b3, None, layer, tm=tm_p, seg=tm_p)
        xp, xnp = _ffn_down(act, w_down_b, xp, g_after, layer, not final, xn_dtype)
        hl = jnp.concatenate([hla, hlb], axis=-1).reshape(BATCH, SEQ // tm_p, CONV_W - 1, 2 * D_FF)
        cv_p.append(hl[:, -1])

        act, hla, hlb = _ffn_up(xns, w_up_b, conv_w, conv_b3, state_ffn_conv, layer, tm=512, seg=DEC_SEQ)
        xs, xns = _ffn_down(act, w_down_b, xs, g_after, layer, not final, xn_dtype)
        cv_s.append(jnp.concatenate([hla, hlb], axis=-1))

    return (xnp.reshape(BATCH, SEQ, D_MODEL), xns.reshape(DEC_BATCH, DEC_SEQ, D_MODEL),
            jnp.stack(lat_p), jnp.stack(kpe_p), jnp.stack(pool_p), jnp.stack(hg_p), jnp.stack(cv_p),
            jnp.stack(lat_s), jnp.stack(kpe_s), jnp.stack(pool_s), jnp.stack(hg_s), jnp.stack(cv_s))
```

```python
import functools

import jax
import jax.numpy as jnp
from jax import lax
from jax.experimental import pallas as pl
from jax.experimental.pallas import tpu as pltpu

D_MODEL = 2048
BATCH = 8
SEQ = 2048
DEPTH = 4
DEC_BATCH = 32
DEC_SEQ = 64
PAST_LEN = 4096
CHUNK = 64
MLA_HEADS = 8
Q_RANK = 512
KV_RANK = 512
NOPE_DIM = 128
ROPE_DIM = 64
V_DIM = 128
ROPE_THETA = 10000.0
MLA_SCALE = (NOPE_DIM + ROPE_DIM) ** -0.5
MASK_NEG = -1e30
POOL_WINDOWS = (2, 4, 8, 16)
POOL_WIDTH = D_MODEL - MLA_HEADS * V_DIM
POOL_GROUP_DIM = POOL_WIDTH // len(POOL_WINDOWS)
POOL_KEEP = max(POOL_WINDOWS) - 1
C_HEADS = 16
HEAD_F = 128
HEAD_I = D_MODEL // C_HEADS
F_MIN = 1e-30
D_FF = 5632
CONV_W = 3
EPS = 1e-6

F32 = jnp.float32
BF16 = jnp.bfloat16

LANES = 128
SUBLANES = 8
VMEM_LIMIT = 56 * 1024 * 1024

IN_EVEN_PAD = Q_RANK + KV_RANK + POOL_WIDTH + LANES
Q_NOPE_ALL = MLA_HEADS * NOPE_DIM
GLA_CHUNK = 64
GLA_SUB = SUBLANES
GLA_HEADS_PER_ITER = 16
FFN_TN = 512


def _params(*sem):
    return pltpu.CompilerParams(dimension_semantics=sem, vmem_limit_bytes=VMEM_LIMIT)


def _dot(a, b):
    return jnp.dot(a, b, preferred_element_type=F32)


def _dot_nt(a, b):
    return lax.dot_general(a, b, (((1,), (1,)), ((), ())), preferred_element_type=F32)


def _dot_tn(a, b):
    return lax.dot_general(a, b, (((0,), (0,)), ((), ())), preferred_element_type=F32)


def _rms(x, g):
    return x * lax.rsqrt(jnp.mean(x * x, axis=-1, keepdims=True) + EPS) * g


def _norm_kernel(x_ref, g_ref, xn_ref):
    xn_ref[...] = _rms(x_ref[...], g_ref[...]).astype(xn_ref.dtype)


def _norm(x, g, tm=512):
    T = x.shape[0]
    return pl.pallas_call(
        _norm_kernel,
        out_shape=jax.ShapeDtypeStruct((T, D_MODEL), BF16),
        grid=(T // tm,),
        in_specs=[pl.BlockSpec((tm, D_MODEL), lambda i: (i, 0)),
                  pl.BlockSpec((1, D_MODEL), lambda i: (0, 0))],
        out_specs=pl.BlockSpec((tm, D_MODEL), lambda i: (i, 0)),
        compiler_params=_params("parallel"),
        name="norm0",
    )(x, g)


def _rope(s, cos, sin, first_half):
    swapped = jnp.where(first_half, pltpu.roll(s, 96, 1), pltpu.roll(s, 32, 1))
    return s * cos + swapped * sin


def _even_in_kernel(xn_ref, w_ref, gqa_ref, gkva_ref, wqb_ref, wuk_ref, cos_ref, sin_ref,
                    lat_ref, latb_ref, kpe_ref, kpeb_ref, z_ref, qlat_ref, qpe_ref):
    tm = xn_ref.shape[0]
    acc = _dot(xn_ref[...], w_ref[...])
    o1, o2, o3 = Q_RANK, Q_RANK + KV_RANK, Q_RANK + KV_RANK + POOL_WIDTH
    z_ref[...] = acc[:, o2:o3]
    lat = _rms(acc[:, o1:o2], gkva_ref[...])
    lat_ref[...] = lat
    latb_ref[...] = lat.astype(BF16)
    cos = cos_ref[...]
    sin = sin_ref[...]
    lane = lax.broadcasted_iota(jnp.int32, (tm, LANES), 1)
    first_half = (lane % ROPE_DIM) < (ROPE_DIM // 2)
    kr = _rope(acc[:, o3:o3 + LANES], cos, sin, first_half)
    kpe_ref[...] = kr[:, :ROPE_DIM]
    kpeb_ref[...] = kr[:, :ROPE_DIM].astype(BF16)
    cqn = _rms(acc[:, :o1], gqa_ref[...]).astype(BF16)
    q = _dot(cqn, wqb_ref[...])
    for s in range(MLA_HEADS // 2):
        lo = Q_NOPE_ALL + LANES * s
        r = (_rope(q[:, lo:lo + LANES], cos, sin, first_half) * MLA_SCALE).astype(BF16)
        qpe_ref[2 * s] = r[:, :ROPE_DIM]
        qpe_ref[2 * s + 1] = r[:, ROPE_DIM:]
    for h in range(MLA_HEADS):
        ql = _dot(q[:, NOPE_DIM * h:NOPE_DIM * (h + 1)].astype(BF16), wuk_ref[h])
        qlat_ref[h] = (ql * MLA_SCALE).astype(BF16)


def _even_in(xn, w_in, g_qa, g_kva, w_qb, w_ukT, cos, sin, li, tm=256):
    T = xn.shape[0]
    row = lambda i: (i, 0)
    fix2 = lambda i: (0, 0)
    out_shape = (
        jax.ShapeDtypeStruct((T, KV_RANK), F32), jax.ShapeDtypeStruct((T, KV_RANK), BF16),
        jax.ShapeDtypeStruct((T, ROPE_DIM), F32), jax.ShapeDtypeStruct((T, ROPE_DIM), BF16),
        jax.ShapeDtypeStruct((T, POOL_WIDTH), F32),
        jax.ShapeDtypeStruct((MLA_HEADS, T, KV_RANK), BF16),
        jax.ShapeDtypeStruct((MLA_HEADS, T, ROPE_DIM), BF16),
    )
    return pl.pallas_call(
        _even_in_kernel,
        out_shape=out_shape,
        grid=(T // tm,),
        in_specs=[
            pl.BlockSpec((tm, D_MODEL), row),
            pl.BlockSpec((None, D_MODEL, IN_EVEN_PAD), lambda i: (li, 0, 0)),
            pl.BlockSpec((None, 1, Q_RANK), lambda i: (li, 0, 0)),
            pl.BlockSpec((None, 1, KV_RANK), lambda i: (li, 0, 0)),
            pl.BlockSpec((None, Q_RANK, MLA_HEADS * (NOPE_DIM + ROPE_DIM)), lambda i: (li, 0, 0)),
            pl.BlockSpec((None, MLA_HEADS, NOPE_DIM, KV_RANK), lambda i: (li, 0, 0, 0)),
            pl.BlockSpec((tm, LANES), row),
            pl.BlockSpec((tm, LANES), row),
        ],
        out_specs=(
            pl.BlockSpec((tm, KV_RANK), row), pl.BlockSpec((tm, KV_RANK), row),
            pl.BlockSpec((tm, ROPE_DIM), row), pl.BlockSpec((tm, ROPE_DIM), row),
            pl.BlockSpec((tm, POOL_WIDTH), row),
            pl.BlockSpec((MLA_HEADS, tm, KV_RANK), lambda i: (0, i, 0)),
            pl.BlockSpec((MLA_HEADS, tm, ROPE_DIM), lambda i: (0, i, 0)),
        ),
        compiler_params=_params("parallel"),
        name="even_in",
    )(xn, w_in, g_qa, g_kva, w_qb, w_ukT, cos, sin)


def _lane_tile(x, width):
    if width <= LANES:
        return x[:, :width]
    return jnp.concatenate([x] * (width // LANES), axis=1)


def _attn_keys_step(qlat_ref, qpe_ref, k, kp, m_ref, l_ref, acc_ref, tq, visible=None):
    tk = k.shape[0]

    def scores(h):
        s = _dot_nt(qlat_ref[h], k) + _dot_nt(qpe_ref[h], kp)
        return s if visible is None else jnp.where(visible, s, MASK_NEG)

    s_next = scores(0)
    for h in range(MLA_HEADS):
        s = s_next
        if h + 1 < MLA_HEADS:
            s_next = scores(h + 1)
        rows = pl.ds(h * tq, tq)
        m_prev = m_ref[rows, :]
        m_new = jnp.maximum(m_prev, jnp.max(s, axis=1, keepdims=True))
        alpha = jnp.exp(m_prev - m_new)
        p = jnp.exp(s - _lane_tile(m_new, tk))
        l_ref[rows, :] = alpha * l_ref[rows, :] + jnp.sum(p, axis=1, keepdims=True)
        acc_ref[rows, :] = _lane_tile(alpha, KV_RANK) * acc_ref[rows, :] + _dot(p.astype(BF16), k)
        m_ref[rows, :] = m_new


def _attn_init(m_ref, l_ref, acc_ref):
    m_ref[...] = jnp.full(m_ref.shape, MASK_NEG, F32)
    l_ref[...] = jnp.zeros(l_ref.shape, F32)
    acc_ref[...] = jnp.zeros(acc_ref.shape, F32)


def _attn_finish(wuv_ref, y_ref, l_ref, acc_ref, tq):
    for h in range(MLA_HEADS):
        rows = pl.ds(h * tq, tq)
        o = (acc_ref[rows, :] / _lane_tile(l_ref[rows, :], KV_RANK)).astype(BF16)
        y_ref[:, V_DIM * h:V_DIM * (h + 1)] = _dot(o, wuv_ref[h]).astype(BF16)


def _attn_prompt_kernel(qi_ref, kj_ref, qlat_ref, qpe_ref, k_ref, kp_ref, wuv_ref, y_ref,
                        m_ref, l_ref, acc_ref, *, tq):
    p = pl.program_id(1)
    i = qi_ref[p]
    j = kj_ref[p]

    @pl.when(j == 0)
    def _():
        _attn_init(m_ref, l_ref, acc_ref)

    @pl.when(j < i)
    def _():
        _attn_keys_step(qlat_ref, qpe_ref, k_ref[...], kp_ref[...], m_ref, l_ref, acc_ref, tq)

    @pl.when(j == i)
    def _():
        row = lax.broadcasted_iota(jnp.int32, (tq, tq), 0)
        col = lax.broadcasted_iota(jnp.int32, (tq, tq), 1)
        visible = col // CHUNK <= row // CHUNK
        _attn_keys_step(qlat_ref, qpe_ref, k_ref[...], kp_ref[...], m_ref, l_ref, acc_ref, tq, visible)
        _attn_finish(wuv_ref, y_ref, l_ref, acc_ref, tq)


def _attn_prompt(qlat, qpe, latb, kpeb, w_uvT, li, tq=256):
    assert tq % CHUNK == 0
    T = latb.shape[0]
    nq = SEQ // tq
    pairs = [(i, j) for i in range(nq) for j in range(i + 1)]
    q_of = jnp.asarray([i for i, _ in pairs], jnp.int32)
    k_of = jnp.asarray([j for _, j in pairs], jnp.int32)
    qmap = lambda b, p, qi, kj: (0, b * nq + qi[p], 0)
    kmap = lambda b, p, qi, kj: (b * nq + kj[p], 0)
    rows = MLA_HEADS * tq
    return pl.pallas_call(
        functools.partial(_attn_prompt_kernel, tq=tq),
        out_shape=jax.ShapeDtypeStruct((T, MLA_HEADS * V_DIM), BF16),
        grid_spec=pltpu.PrefetchScalarGridSpec(
            num_scalar_prefetch=2,
            grid=(BATCH, len(pairs)),
            in_specs=[
                pl.BlockSpec((MLA_HEADS, tq, KV_RANK), qmap),
                pl.BlockSpec((MLA_HEADS, tq, ROPE_DIM), qmap),
                pl.BlockSpec((tq, KV_RANK), kmap),
                pl.BlockSpec((tq, ROPE_DIM), kmap),
                pl.BlockSpec((None, MLA_HEADS, KV_RANK, V_DIM), lambda b, p, qi, kj: (li, 0, 0, 0)),
            ],
            out_specs=pl.BlockSpec((tq, MLA_HEADS * V_DIM), lambda b, p, qi, kj: (b * nq + qi[p], 0)),
            scratch_shapes=[pltpu.VMEM((rows, LANES), F32), pltpu.VMEM((rows, LANES), F32),
                            pltpu.VMEM((rows, KV_RANK), F32)],
        ),
        compiler_params=_params("parallel", "arbitrary"),
        name="attn_prompt",
    )(q_of, k_of, qlat, qpe, latb, kpeb, w_uvT)


def _attn_sample_kernel(qlat_ref, qpe_ref, ck_ref, ckp_ref, nk_ref, nkp_ref, wuv_ref, y_ref,
                        m_ref, l_ref, acc_ref):
    j = pl.program_id(1)
    last = pl.num_programs(1) - 1

    @pl.when(j == 0)
    def _():
        _attn_init(m_ref, l_ref, acc_ref)

    def step(k, kp):
        _attn_keys_step(qlat_ref, qpe_ref, k, kp, m_ref, l_ref, acc_ref, DEC_SEQ)

    @pl.when(j < last)
    def _():
        step(ck_ref[...].astype(BF16), ckp_ref[...].astype(BF16))

    @pl.when(j == last)
    def _():
        step(nk_ref[...], nkp_ref[...])
        _attn_finish(wuv_ref, y_ref, l_ref, acc_ref, DEC_SEQ)


def _attn_sample(qlat, qpe, cache_lat, cache_kpe, latb, kpeb, w_uvT, li, tk=1024):
    assert PAST_LEN % CHUNK == 0 and DEC_SEQ <= CHUNK and PAST_LEN % tk == 0
    T = latb.shape[0]
    nc = PAST_LEN // tk
    qmap = lambda b, j: (0, b, 0)
    cmap = lambda b, j: (li, b, jnp.minimum(j, nc - 1), 0)
    rows = MLA_HEADS * DEC_SEQ
    return pl.pallas_call(
        _attn_sample_kernel,
        out_shape=jax.ShapeDtypeStruct((T, MLA_HEADS * V_DIM), BF16),
        grid=(DEC_BATCH, nc + 1),
        in_specs=[
            pl.BlockSpec((MLA_HEADS, DEC_SEQ, KV_RANK), qmap),
            pl.BlockSpec((MLA_HEADS, DEC_SEQ, ROPE_DIM), qmap),
            pl.BlockSpec((None, None, tk, KV_RANK), cmap),
            pl.BlockSpec((None, None, tk, ROPE_DIM), cmap),
            pl.BlockSpec((DEC_SEQ, KV_RANK), lambda b, j: (b, 0)),
            pl.BlockSpec((DEC_SEQ, ROPE_DIM), lambda b, j: (b, 0)),
            pl.BlockSpec((None, MLA_HEADS, KV_RANK, V_DIM), lambda b, j: (li, 0, 0, 0)),
        ],
        out_specs=pl.BlockSpec((DEC_SEQ, MLA_HEADS * V_DIM), lambda b, j: (b, 0)),
        scratch_shapes=[pltpu.VMEM((rows, LANES), F32), pltpu.VMEM((rows, LANES), F32),
                        pltpu.VMEM((rows, KV_RANK), F32)],
        compiler_params=_params("parallel", "arbitrary"),
        name="attn_sample",
    )(qlat, qpe, cache_lat, cache_kpe, latb, kpeb, w_uvT)


POOL_HALO = 2 * SUBLANES


def _pool_kernel(*refs, tp, from_state):
    if from_state:
        z_ref, past_ref, w_ref, scale_ref, y_ref, ext_ref = refs
        ext_ref[0:1, :] = jnp.zeros((1, POOL_WIDTH), F32)
        ext_ref[1:POOL_HALO, :] = past_ref[...]
    else:
        z_ref, w_ref, scale_ref, y_ref, ext_ref = refs
        t = pl.program_id(1)

        @pl.when(t == 0)
        def _():
            ext_ref[0:POOL_HALO, :] = jnp.zeros((POOL_HALO, POOL_WIDTH), F32)

    ext_ref[POOL_HALO:POOL_HALO + tp, :] = z_ref[...]
    for gi, w in enumerate(POOL_WINDOWS):
        lanes = slice(POOL_GROUP_DIM * gi, POOL_GROUP_DIM * (gi + 1))
        tok = ext_ref[POOL_HALO:POOL_HALO + tp, lanes]
        acc = tok
        for d in range(1, w):
            acc = acc + ext_ref[POOL_HALO - d:POOL_HALO - d + tp, lanes]
        if from_state:
            mean = acc / float(w)
        else:
            pos = t * tp + lax.broadcasted_iota(jnp.int32, (tp, 1), 0)
            mean = acc / jnp.minimum(pos + 1, w).astype(F32)
        p = (mean - tok).astype(BF16)
        y_ref[:, lanes] = (_dot(p, w_ref[gi]) * scale_ref[:, lanes]).astype(BF16)
    if not from_state:
        ext_ref[0:POOL_HALO, :] = ext_ref[tp:tp + POOL_HALO, :]


def _pool_prompt(z, w_pool, pool_scale, li, tp=512):
    T = z.shape[0]
    nt = SEQ // tp
    return pl.pallas_call(
        functools.partial(_pool_kernel, tp=tp, from_state=False),
        out_shape=jax.ShapeDtypeStruct((T, POOL_WIDTH), BF16),
        grid=(BATCH, nt),
        in_specs=[
            pl.BlockSpec((tp, POOL_WIDTH), lambda b, t: (b * nt + t, 0)),
            pl.BlockSpec((None, len(POOL_WINDOWS), POOL_GROUP_DIM, POOL_GROUP_DIM), lambda b, t: (li, 0, 0, 0)),
            pl.BlockSpec((None, 1, POOL_WIDTH), lambda b, t: (li, 0, 0)),
        ],
        out_specs=pl.BlockSpec((tp, POOL_WIDTH), lambda b, t: (b * nt + t, 0)),
        scratch_shapes=[pltpu.VMEM((POOL_HALO + tp, POOL_WIDTH), F32)],
        compiler_params=_params("arbitrary", "arbitrary"),
        name="pool_prompt",
    )(z, w_pool, pool_scale)


def _pool_sample(z, state_pool, w_pool, pool_scale, li):
    T = z.shape[0]
    return pl.pallas_call(
        functools.partial(_pool_kernel, tp=DEC_SEQ, from_state=True),
        out_shape=jax.ShapeDtypeStruct((T, POOL_WIDTH), BF16),
        grid=(DEC_BATCH,),
        in_specs=[
            pl.BlockSpec((DEC_SEQ, POOL_WIDTH), lambda b: (b, 0)),
            pl.BlockSpec((None, None, POOL_KEEP, POOL_WIDTH), lambda b: (li, b, 0, 0)),
            pl.BlockSpec((None, len(POOL_WINDOWS), POOL_GROUP_DIM, POOL_GROUP_DIM), lambda b: (li, 0, 0, 0)),
            pl.BlockSpec((None, 1, POOL_WIDTH), lambda b: (li, 0, 0)),
        ],
        out_specs=pl.BlockSpec((DEC_SEQ, POOL_WIDTH), lambda b: (b, 0)),
        scratch_shapes=[pltpu.VMEM((POOL_HALO + DEC_SEQ, POOL_WIDTH), F32)],
        compiler_params=_params("parallel"),
        name="pool_sample",
    )(z, state_pool, w_pool, pool_scale)


def _proj_res_norm_kernel(*refs, n_a, head_major, emit_x):
    a_refs = refs[:n_a]
    w_refs = refs[n_a:2 * n_a]
    x_ref, g_ref = refs[2 * n_a:2 * n_a + 2]
    outs = refs[2 * n_a + 2:]
    if head_major:
        (a_ref,), (w_ref,) = a_refs, w_refs
        xo_ref, xn_ref, cat_ref = outs
        for h in range(C_HEADS):
            cat_ref[:, HEAD_I * h:HEAD_I * (h + 1)] = a_ref[h]
        acc = _dot(cat_ref[...], w_ref[...])
    else:
        xo_ref, xn_ref = outs if emit_x else (None, outs[0])
        acc = _dot(a_refs[0][...], w_refs[0][...])
        for a_ref, w_ref in zip(a_refs[1:], w_refs[1:]):
            acc = acc + _dot(a_ref[...], w_ref[...])
    xnew = x_ref[...] + acc
    if xo_ref is not None:
        xo_ref[...] = xnew
    xn_ref[...] = _rms(xnew, g_ref[...]).astype(xn_ref.dtype)


def _proj_res_norm(a_list, w, x, g, li, head_major=False, tm=512):
    T = x.shape[0]
    row = lambda i: (i, 0)
    n_a = len(a_list)
    in_specs = []
    for a in a_list:
        if head_major:
            in_specs.append(pl.BlockSpec((C_HEADS, tm, HEAD_I), lambda i: (0, i, 0)))
        else:
            in_specs.append(pl.BlockSpec((tm, a.shape[1]), row))
    for kb, a in enumerate(a_list):
        kdim = C_HEADS * HEAD_I if head_major else a.shape[1]
        in_specs.append(pl.BlockSpec((None, kdim, D_MODEL), lambda i, kb=kb: (li, kb, 0)))
    in_specs += [pl.BlockSpec((tm, D_MODEL), row), pl.BlockSpec((1, D_MODEL), lambda i: (0, 0))]
    scratch = [pltpu.VMEM((tm, D_MODEL), BF16)] if head_major else []
    return pl.pallas_call(
        functools.partial(_proj_res_norm_kernel, n_a=n_a, head_major=head_major, emit_x=True),
        out_shape=(jax.ShapeDtypeStruct((T, D_MODEL), F32), jax.ShapeDtypeStruct((T, D_MODEL), BF16)),
        grid=(T // tm,),
        in_specs=in_specs,
        out_specs=(pl.BlockSpec((tm, D_MODEL), row), pl.BlockSpec((tm, D_MODEL), row)),
        scratch_shapes=scratch,
        compiler_params=_params("parallel"),
        name="proj_res_norm",
    )(*a_list, *([w] * n_a), x, g)


def _odd_in_kernel(xn_ref, w_ref, lbp_ref, o_ref, *, li):
    s = pl.program_id(0)
    h = _dot(xn_ref[...], w_ref[...])
    sg = jax.nn.sigmoid(h)
    p = lbp_ref[...]
    e = jnp.exp(p - jnp.max(p, axis=0, keepdims=True))
    sm = e / jnp.sum(e, axis=0, keepdims=True)
    lb = jnp.clip(jnp.sum(sm[:li + 1], axis=0, keepdims=True) - sm[0:1], 0.0, 1.0)
    f = lb + (1.0 - lb) * sg
    out = jnp.where(s == 1, f, jnp.where(s == 2, h, h * sg))
    for hd in range(C_HEADS):
        o_ref[hd] = out[:, HEAD_I * hd:HEAD_I * (hd + 1)]


def _odd_in(xn, w_in_c, lb_param, li, tm=512):
    T = xn.shape[0]
    return pl.pallas_call(
        functools.partial(_odd_in_kernel, li=li),
        out_shape=jax.ShapeDtypeStruct((4, C_HEADS, T, HEAD_I), F32),
        grid=(4, T // tm),
        in_specs=[
            pl.BlockSpec((tm, D_MODEL), lambda s, i: (i, 0)),
            pl.BlockSpec((None, D_MODEL, D_MODEL), lambda s, i: (li, 0, s)),
            pl.BlockSpec(lb_param.shape, lambda s, i: (0, 0)),
        ],
        out_specs=pl.BlockSpec((None, C_HEADS, tm, HEAD_I), lambda s, i: (s, 0, i, 0)),
        compiler_params=_params("parallel", "parallel"),
        name="odd_in",
    )(xn, w_in_c, lb_param)


def _gla_kernel(*refs, has_s0):
    if has_s0:
        q_ref, f_ref, v_ref, gs_ref, gon_ref, s0_ref, og_ref, sout_ref, st_ref = refs
    else:
        q_ref, f_ref, v_ref, gs_ref, gon_ref, og_ref, sout_ref, st_ref = refs
    C = GLA_CHUNK
    n = pl.program_id(1)
    last = pl.num_programs(1) - 1

    @pl.when(n == 0)
    def _():
        if has_s0:
            def load(h, c):
                st_ref[h] = s0_ref[h].T
                return c
            lax.fori_loop(0, C_HEADS, load, 0)
        else:
            st_ref[...] = jnp.zeros(st_ref.shape, F32)

    r2 = lax.broadcasted_iota(jnp.int32, (C, C), 0)
    c2 = lax.broadcasted_iota(jnp.int32, (C, C), 1)
    tri = (c2 <= r2).astype(BF16)
    ones = jnp.ones((HEAD_F, LANES), BF16)
    gon = gon_ref[...]
    levels = []
    m = GLA_SUB
    while m < C:
        keep = (r2 // (2 * m) == c2 // (2 * m)) & (r2 % (2 * m) >= m) & (c2 % (2 * m) < m)
        levels.append((m, keep))
        m *= 2
    near = [(c2 == r2 - delta) & (r2 % GLA_SUB >= delta) for delta in range(GLA_SUB)]

    def bcast_row(a, period, r):
        a3 = a.reshape(C // period, period, a.shape[-1])
        return jnp.broadcast_to(a3[:, r:r + 1, :], a3.shape).reshape(C, a.shape[-1])

    def shift_in_group(a, delta):
        if delta == 0:
            return a
        a3 = a.reshape(C // GLA_SUB, GLA_SUB, a.shape[-1])
        return pltpu.roll(a3, delta, 1).reshape(C, a.shape[-1])

    def split3(g):
        g1 = g.astype(BF16)
        r1 = g - g1.astype(F32)
        g2 = r1.astype(BF16)
        return g1, g2, (r1 - g2.astype(F32)).astype(BF16)

    def head_group(hg, carry):
        hs = [hg * GLA_HEADS_PER_ITER + u for u in range(GLA_HEADS_PER_ITER)]
        U = range(len(hs))
        q = [q_ref[h] for h in hs]
        f = [f_ref[h] for h in hs]
        v_bf = [v_ref[h].astype(BF16) for h in hs]
        fc = [jnp.maximum(x, F_MIN) for x in f]
        kk = [1.0 - x for x in f]
        gp = [split3(jnp.log2(x)) for x in fc]
        b = [_dot(tri, gp[u][0]) + _dot(tri, gp[u][1]) + _dot(tri, gp[u][2]) for u in U]
        st = [st_ref[h] for h in hs]
        o = [_dot_nt((q[u] * jnp.exp2(b[u])).astype(BF16), st[u].astype(BF16)) for u in U]
        b_last = [x[C - 1:C, :] for x in b]
        k_dec = [(kk[u] * jnp.exp2(b_last[u] - b[u])).astype(BF16) for u in U]
        for u in U:
            st_ref[hs[u]] = st[u] * jnp.exp2(b_last[u]) + _dot_tn(v_bf[u], k_dec[u])

        a = [jnp.zeros((C, C), F32) for _ in U]
        for m, keep in levels:
            for u in U:
                d = b[u] - bcast_row(b[u], 2 * m, m - 1)
                qt = (q[u] * jnp.exp2(jnp.minimum(d, 0.0))).astype(BF16)
                kt = (kk[u] * jnp.exp2(jnp.minimum(-d, 0.0))).astype(BF16)
                a[u] = a[u] + jnp.where(keep, _dot_nt(qt, kt), 0.0)
        dec = [None for _ in U]
        for delta in range(GLA_SUB):
            for u in U:
                p = q[u] * shift_in_group(kk[u], delta)
                if delta > 0:
                    f_sh = shift_in_group(fc[u], delta - 1)
                    dec[u] = f_sh if dec[u] is None else dec[u] * f_sh
                    p = p * dec[u]
                a[u] = a[u] + jnp.where(near[delta], _dot(p.astype(BF16), ones)[:, :C], 0.0)
        for u in U:
            y = _rms(o[u] + _dot(a[u].astype(BF16), v_bf[u]), gon) * gs_ref[hs[u]]
            og_ref[hs[u]] = y.astype(BF16)
        return carry

    lax.fori_loop(0, C_HEADS // GLA_HEADS_PER_ITER, head_group, 0)

    @pl.when(n == last)
    def _():
        def store(h, c):
            sout_ref[h] = st_ref[h].T
            return c
        lax.fori_loop(0, C_HEADS, store, 0)


def _gla(qfvg, g_onorm, s0, li, n_streams, seq):
    T = qfvg.shape[2]
    C = GLA_CHUNK
    nc = seq // C
    has_s0 = s0 is not None
    sec = lambda k: pl.BlockSpec((None, C_HEADS, C, HEAD_F), lambda b, n, k=k: (k, 0, b * nc + n, 0))
    in_specs = [sec(0), sec(1), sec(2), sec(3),
                pl.BlockSpec((None, 1, HEAD_I), lambda b, n: (li, 0, 0))]
    args = [qfvg, qfvg, qfvg, qfvg, g_onorm]
    if has_s0:
        in_specs.append(pl.BlockSpec((None, None, C_HEADS, HEAD_F, HEAD_I), lambda b, n: (li, b, 0, 0, 0)))
        args.append(s0)
    return pl.pallas_call(
        functools.partial(_gla_kernel, has_s0=has_s0),
        out_shape=(jax.ShapeDtypeStruct((C_HEADS, T, HEAD_I), BF16),
                   jax.ShapeDtypeStruct((n_streams, C_HEADS, HEAD_F, HEAD_I), F32)),
        grid=(n_streams, nc),
        in_specs=in_specs,
        out_specs=(pl.BlockSpec((C_HEADS, C, HEAD_I), lambda b, n: (0, b * nc + n, 0)),
                   pl.BlockSpec((None, C_HEADS, HEAD_F, HEAD_I), lambda b, n: (b, 0, 0, 0))),
        scratch_shapes=[pltpu.VMEM((C_HEADS, HEAD_I, HEAD_F), F32)],
        compiler_params=_params("parallel", "arbitrary"),
        name="gla",
    )(*args)


FFN_PAD = SUBLANES
FFN_SUB_ROWS = 256


def _ffn_up_kernel(*refs, tm, seg, from_state, tiles_per_stream):
    if from_state:
        (xn_ref, wa_ref, wb_ref, cwa_ref, cwb_ref, cba_ref, cbb_ref, pa_ref, pb_ref,
         act_ref, hla_ref, hlb_ref, ext_ref) = refs
        carry_ref = None
    else:
        (xn_ref, wa_ref, wb_ref, cwa_ref, cwb_ref, cba_ref, cbb_ref,
         act_ref, hla_ref, hlb_ref, ext_ref, carry_ref) = refs
        pa_ref = pb_ref = None
    i = pl.program_id(0)
    j = pl.program_id(1)
    nseg = tm // seg
    sub = min(FFN_SUB_ROWS, tm)
    assert seg % sub == 0 or sub % seg == 0
    halves = ((0, wa_ref, cwa_ref, cba_ref, pa_ref, hla_ref), (1, wb_ref, cwb_ref, cbb_ref, pb_ref, hlb_ref))

    for hidx, _, _, _, past_ref, _ in halves:
        for s in range(nseg):
            if from_state:
                ext_ref[hidx, s, FFN_PAD - 2:FFN_PAD, :] = past_ref[s]
            else:
                @pl.when(i % tiles_per_stream == 0)
                def _():
                    ext_ref[hidx, s, 0:FFN_PAD, :] = jnp.zeros((FFN_PAD, FFN_TN), F32)

                @pl.when(i % tiles_per_stream != 0)
                def _():
                    ext_ref[hidx, s, 0:FFN_PAD, :] = carry_ref[j, hidx]

    def conv(hidx, cw_ref, cb_ref, s, o, n):
        base = FFN_PAD + o
        cw = cw_ref[...]
        return cb_ref[...] + (ext_ref[hidx, s, base - 2:base - 2 + n, :] * cw[0:1] +
                              ext_ref[hidx, s, base - 1:base - 1 + n, :] * cw[1:2] +
                              ext_ref[hidx, s, base:base + n, :] * cw[2:3])

    for r0 in range(0, tm, sub):
        for hidx, w_ref, _, _, _, _ in halves:
            h = _dot(xn_ref[r0:r0 + sub, :], w_ref[...])
            if seg >= sub:
                o = FFN_PAD + r0 % seg
                ext_ref[hidx, r0 // seg, o:o + sub, :] = h
            else:
                for k in range(sub // seg):
                    ext_ref[hidx, r0 // seg + k, FFN_PAD:FFN_PAD + seg, :] = h[k * seg:(k + 1) * seg]
        if seg >= sub:
            pieces = [(r0 // seg, r0 % seg, sub)]
        else:
            pieces = [(r0 // seg + k, 0, seg) for k in range(sub // seg)]
        for s, o, n in pieces:
            a = conv(0, cwa_ref, cba_ref, s, o, n)
            b = conv(1, cwb_ref, cbb_ref, s, o, n)
            act_ref[s * seg + o:s * seg + o + n, :] = (a * jax.nn.sigmoid(a) * b).astype(BF16)

    for hidx, _, _, _, _, hl_ref in halves:
        for s in range(nseg):
            hl_ref[s] = ext_ref[hidx, s, FFN_PAD + seg - 2:FFN_PAD + seg, :]
        if not from_state:
            carry_ref[j, hidx] = ext_ref[hidx, 0, seg:seg + FFN_PAD, :]


def _ffn_up(xn, w_up, conv_w, conv_b, conv_past, layer, tm, seg):
    T = xn.shape[0]
    tn = FFN_TN
    nj = D_FF // tn
    nseg = tm // seg
    from_state = conv_past is not None
    wmap_a = lambda i, j: (layer, 0, j)
    wmap_b = lambda i, j: (layer, 0, nj + j)
    in_specs = [
        pl.BlockSpec((tm, D_MODEL), lambda i, j: (i, 0)),
        pl.BlockSpec((None, D_MODEL, tn), wmap_a), pl.BlockSpec((None, D_MODEL, tn), wmap_b),
        pl.BlockSpec((None, CONV_W, tn), wmap_a), pl.BlockSpec((None, CONV_W, tn), wmap_b),
        pl.BlockSpec((None, 1, tn), wmap_a), pl.BlockSpec((None, 1, tn), wmap_b),
    ]
    args = [xn, w_up, w_up, conv_w, conv_w, conv_b, conv_b]
    scratch = [pltpu.VMEM((2, nseg, FFN_PAD + seg, tn), F32)]
    if from_state:
        in_specs += [pl.BlockSpec((None, nseg, CONV_W - 1, tn), lambda i, j: (layer, i, 0, j)),
                     pl.BlockSpec((None, nseg, CONV_W - 1, tn), lambda i, j: (layer, i, 0, nj + j))]
        args += [conv_past, conv_past]
        tiles_per_stream = 1
    else:
        scratch.append(pltpu.VMEM((nj, 2, FFN_PAD, tn), F32))
        tiles_per_stream = SEQ // tm
    hl_shape = jax.ShapeDtypeStruct((T // seg, CONV_W - 1, D_FF), F32)
    hl_spec = pl.BlockSpec((nseg, CONV_W - 1, tn), lambda i, j: (i, 0, j))
    return pl.pallas_call(
        functools.partial(_ffn_up_kernel, tm=tm, seg=seg, from_state=from_state,
                          tiles_per_stream=tiles_per_stream),
        out_shape=(jax.ShapeDtypeStruct((T, D_FF), BF16), hl_shape, hl_shape),
        grid=(T // tm, nj),
        in_specs=in_specs,
        out_specs=(pl.BlockSpec((tm, tn), lambda i, j: (i, j)), hl_spec, hl_spec),
        scratch_shapes=scratch,
        compiler_params=_params("arbitrary", "arbitrary"),
        name="ffn_up",
    )(*args)


def _ffn_down_kernel(a_ref, w_ref, x_ref, g_ref, *outs, emit_x):
    if emit_x:
        acc_ref, xn_ref = outs
    else:
        xn_ref, acc_ref = outs
    k = pl.program_id(1)

    @pl.when(k == 0)
    def _():
        acc_ref[...] = x_ref[...]

    acc_ref[...] += _dot(a_ref[...], w_ref[...])

    @pl.when(k == pl.num_programs(1) - 1)
    def _():
        xn_ref[...] = _rms(acc_ref[...], g_ref[...]).astype(xn_ref.dtype)


def _ffn_down(act, w_down, x, g, layer, emit_x, xn_dtype, tm=1024, nk=11):
    T = x.shape[0]
    tk = D_FF // nk
    row = lambda i, k: (i, 0)
    out_shape = [jax.ShapeDtypeStruct((T, D_MODEL), xn_dtype)]
    out_specs = [pl.BlockSpec((tm, D_MODEL), row)]
    scratch = [pltpu.VMEM((tm, D_MODEL), F32)]
    if emit_x:
        out_shape.insert(0, jax.ShapeDtypeStruct((T, D_MODEL), F32))
        out_specs.insert(0, pl.BlockSpec((tm, D_MODEL), row))
        scratch = []
    res = pl.pallas_call(
        functools.partial(_ffn_down_kernel, emit_x=emit_x),
        out_shape=tuple(out_shape),
        grid=(T // tm, nk),
        in_specs=[
            pl.BlockSpec((tm, tk), lambda i, k: (i, k)),
            pl.BlockSpec((None, tk, D_MODEL), lambda i, k: (layer, k, 0)),
            pl.BlockSpec((tm, D_MODEL), row),
            pl.BlockSpec((1, D_MODEL), lambda i, k: (0, 0)),
        ],
        out_specs=tuple(out_specs),
        scratch_shapes=scratch,
        compiler_params=_params("parallel", "arbitrary"),
        name="ffn_down",
    )(act, w_down, x, g)
    return res if emit_x else (None, res[0])


def _rope_tables(pos):
    inv = ROPE_THETA ** (-jnp.arange(0, ROPE_DIM, 2, dtype=F32) / ROPE_DIM)
    ang = pos.astype(F32)[:, None] * inv[None, :]
    cos, sin = jnp.cos(ang), jnp.sin(ang)
    reps = LANES // ROPE_DIM
    return (jnp.tile(jnp.concatenate([cos, cos], axis=-1), (1, reps)),
            jnp.tile(jnp.concatenate([-sin, sin], axis=-1), (1, reps)))


def kernel(x_prompt, x_sample, cache_mla_latent, cache_mla_krope, state_pool, state_hgrn, state_ffn_conv,
           g_mix, g_ffn, g_final, w_in_a, g_qa, w_qb, g_kva, w_uk, w_uv, w_pool, pool_scale, w_out_a,
           w_in_c, lb_param, g_onorm, w_out_c, w_up, conv_w, conv_b, w_down):
    n_even = w_in_a.shape[0]
    n_odd = w_in_c.shape[0]
    Tp, Ts = BATCH * SEQ, DEC_BATCH * DEC_SEQ

    o1, o2, o3 = Q_RANK, Q_RANK + KV_RANK, Q_RANK + KV_RANK + ROPE_DIM
    w_in_e = jnp.concatenate(
        [w_in_a[..., :o2], w_in_a[..., o3:], w_in_a[..., o2:o3],
         jnp.zeros((n_even, D_MODEL, LANES - ROPE_DIM), w_in_a.dtype)], axis=-1).astype(BF16)
    w_qb4 = w_qb.reshape(n_even, Q_RANK, MLA_HEADS, NOPE_DIM + ROPE_DIM)
    w_qb_e = jnp.concatenate(
        [w_qb4[..., :NOPE_DIM].reshape(n_even, Q_RANK, Q_NOPE_ALL),
         w_qb4[..., NOPE_DIM:].reshape(n_even, Q_RANK, MLA_HEADS * ROPE_DIM)], axis=-1).astype(BF16)
    w_ukT = jnp.transpose(w_uk, (0, 2, 3, 1)).astype(BF16)
    w_uvT = jnp.transpose(w_uv, (0, 2, 1, 3)).astype(BF16)
    w_pool_b = w_pool.astype(BF16)
    w_out_a_b = w_out_a.astype(BF16)
    w_in_c_b = w_in_c.astype(BF16)
    w_out_c_b = w_out_c.astype(BF16)
    w_up_b = w_up.astype(BF16)
    w_down_b = w_down.astype(BF16)
    g_qa3 = g_qa[:, None, :]
    g_kva3 = g_kva[:, None, :]
    pool_scale3 = pool_scale[:, None, :]
    g_onorm3 = g_onorm[:, None, :]
    conv_b3 = conv_b[:, None, :]

    cos_p, sin_p = _rope_tables(jnp.tile(jnp.arange(SEQ), BATCH))
    cos_s, sin_s = _rope_tables(jnp.tile(PAST_LEN + jnp.arange(DEC_SEQ), DEC_BATCH))

    xp = x_prompt.reshape(Tp, D_MODEL)
    xs = x_sample.reshape(Ts, D_MODEL)
    xnp = _norm(xp, g_mix[0:1])
    xns = _norm(xs, g_mix[0:1])

    lat_p, kpe_p, pool_p, hg_p, cv_p = [], [], [], [], []
    lat_s, kpe_s, pool_s, hg_s, cv_s = [], [], [], [], []
    for layer in range(DEPTH):
        li = layer // 2
        g_next = g_ffn[layer:layer + 1]
        if layer % 2 == 0:
            wa = (w_in_e, g_qa3, g_kva3, w_qb_e, w_ukT)
            lat, latb, kpe, kpeb, z, qlat, qpe = _even_in(xnp, *wa, cos_p, sin_p, li)
            y_mla = _attn_prompt(qlat, qpe, latb, kpeb, w_uvT, li)
            y_pool = _pool_prompt(z, w_pool_b, pool_scale3, li)
            xp, xnp = _proj_res_norm([y_mla, y_pool], w_out_a_b, xp, g_next, li)
            lat_p.append(lat.reshape(BATCH, SEQ, KV_RANK))
            kpe_p.append(kpe.reshape(BATCH, SEQ, ROPE_DIM))
            pool_p.append(z.reshape(BATCH, SEQ, POOL_WIDTH)[:, SEQ - POOL_KEEP:])

            lat, latb, kpe, kpeb, z, qlat, qpe = _even_in(xns, *wa, cos_s, sin_s, li)
            y_mla = _attn_sample(qlat, qpe, cache_mla_latent, cache_mla_krope, latb, kpeb, w_uvT, li)
            y_pool = _pool_sample(z, state_pool, w_pool_b, pool_scale3, li)
            xs, xns = _proj_res_norm([y_mla, y_pool], w_out_a_b, xs, g_next, li)
            lat_s.append(lat.reshape(DEC_BATCH, DEC_SEQ, KV_RANK))
            kpe_s.append(kpe.reshape(DEC_BATCH, DEC_SEQ, ROPE_DIM))
            pool_s.append(z.reshape(DEC_BATCH, DEC_SEQ, POOL_WIDTH)[:, DEC_SEQ - POOL_KEEP:])
        else:
            qfvg = _odd_in(xnp, w_in_c_b, lb_param, li)
            og, s_fin = _gla(qfvg, g_onorm3, None, li, BATCH, SEQ)
            xp, xnp = _proj_res_norm([og], w_out_c_b, xp, g_next, li, head_major=True)
            hg_p.append(s_fin)

            qfvg = _odd_in(xns, w_in_c_b, lb_param, li)
            og, s_fin = _gla(qfvg, g_onorm3, state_hgrn, li, DEC_BATCH, DEC_SEQ)
            xs, xns = _proj_res_norm([og], w_out_c_b, xs, g_next, li, head_major=True)
            hg_s.append(s_fin)

        final = layer == DEPTH - 1
        g_after = g_final[None, :] if final else g_mix[layer + 1:layer + 2]
        xn_dtype = F32 if final else BF16

        tm_p = 1024
        act, hla, hlb = _ffn_up(xnp, w_up_b, conv_w, conv_b3, None, layer, tm=tm_p, seg=tm_p)
        xp, xnp = _ffn_down(act, w_down_b, xp, g_after, layer, not final, xn_dtype)
        hl = jnp.concatenate([hla, hlb], axis=-1).reshape(BATCH, SEQ // tm_p, CONV_W - 1, 2 * D_FF)
        cv_p.append(hl[:, -1])

        act, hla, hlb = _ffn_up(xns, w_up_b, conv_w, conv_b3, state_ffn_conv, layer, tm=512, seg=DEC_SEQ)
        xs, xns = _ffn_down(act, w_down_b, xs, g_after, layer, not final, xn_dtype)
        cv_s.append(jnp.concatenate([hla, hlb], axis=-1))

    return (xnp.reshape(BATCH, SEQ, D_MODEL), xns.reshape(DEC_BATCH, DEC_SEQ, D_MODEL),
            jnp.stack(lat_p), jnp.stack(kpe_p), jnp.stack(pool_p), jnp.stack(hg_p), jnp.stack(cv_p),
            jnp.stack(lat_s), jnp.stack(kpe_s), jnp.stack(pool_s), jnp.stack(hg_s), jnp.stack(cv_s))
```

```python
import functools

import jax
import jax.numpy as jnp
from jax import lax
from jax.experimental import pallas as pl
from jax.experimental.pallas import tpu as pltpu

D_MODEL = 2048
BATCH = 8
SEQ = 2048
DEPTH = 4
DEC_BATCH = 32
DEC_SEQ = 64
PAST_LEN = 4096
CHUNK = 64
MLA_HEADS = 8
Q_RANK = 512
KV_RANK = 512
NOPE_DIM = 128
ROPE_DIM = 64
V_DIM = 128
ROPE_THETA = 10000.0
MLA_SCALE = (NOPE_DIM + ROPE_DIM) ** -0.5
MASK_NEG = -1e30
POOL_WINDOWS = (2, 4, 8, 16)
POOL_WIDTH = D_MODEL - MLA_HEADS * V_DIM
POOL_GROUP_DIM = POOL_WIDTH // len(POOL_WINDOWS)
POOL_KEEP = max(POOL_WINDOWS) - 1
C_HEADS = 16
HEAD_F = 128
HEAD_I = D_MODEL // C_HEADS
F_MIN = 1e-30
D_FF = 5632
CONV_W = 3
EPS = 1e-6

F32 = jnp.float32
BF16 = jnp.bfloat16

LANES = 128
SUBLANES = 8
VMEM_LIMIT = 56 * 1024 * 1024

IN_EVEN_PAD = Q_RANK + KV_RANK + POOL_WIDTH + LANES
Q_NOPE_ALL = MLA_HEADS * NOPE_DIM
GLA_CHUNK = 64
GLA_SUB = SUBLANES
GLA_HEADS_PER_ITER = 16
FFN_TN = 512


def _params(*sem):
    return pltpu.CompilerParams(dimension_semantics=sem, vmem_limit_bytes=VMEM_LIMIT)


def _dot(a, b):
    return jnp.dot(a, b, preferred_element_type=F32)


def _dot_nt(a, b):
    return lax.dot_general(a, b, (((1,), (1,)), ((), ())), preferred_element_type=F32)


def _dot_tn(a, b):
    return lax.dot_general(a, b, (((0,), (0,)), ((), ())), preferred_element_type=F32)


def _rms(x, g):
    return x * lax.rsqrt(jnp.mean(x * x, axis=-1, keepdims=True) + EPS) * g


def _norm_kernel(x_ref, g_ref, xn_ref):
    xn_ref[...] = _rms(x_ref[...], g_ref[...]).astype(xn_ref.dtype)


def _norm(x, g, tm=512):
    T = x.shape[0]
    return pl.pallas_call(
        _norm_kernel,
        out_shape=jax.ShapeDtypeStruct((T, D_MODEL), BF16),
        grid=(T // tm,),
        in_specs=[pl.BlockSpec((tm, D_MODEL), lambda i: (i, 0)),
                  pl.BlockSpec((1, D_MODEL), lambda i: (0, 0))],
        out_specs=pl.BlockSpec((tm, D_MODEL), lambda i: (i, 0)),
        compiler_params=_params("parallel"),
        name="norm0",
    )(x, g)


def _rope(s, cos, sin, first_half):
    swapped = jnp.where(first_half, pltpu.roll(s, 96, 1), pltpu.roll(s, 32, 1))
    return s * cos + swapped * sin


def _even_in_kernel(xn_ref, w_ref, gqa_ref, gkva_ref, wqb_ref, wuk_ref, cos_ref, sin_ref,
                    lat_ref, latb_ref, kpe_ref, kpeb_ref, z_ref, qlat_ref, qpe_ref):
    tm = xn_ref.shape[0]
    acc = _dot(xn_ref[...], w_ref[...])
    o1, o2, o3 = Q_RANK, Q_RANK + KV_RANK, Q_RANK + KV_RANK + POOL_WIDTH
    z_ref[...] = acc[:, o2:o3]
    lat = _rms(acc[:, o1:o2], gkva_ref[...])
    lat_ref[...] = lat
    latb_ref[...] = lat.astype(BF16)
    cos = cos_ref[...]
    sin = sin_ref[...]
    lane = lax.broadcasted_iota(jnp.int32, (tm, LANES), 1)
    first_half = (lane % ROPE_DIM) < (ROPE_DIM // 2)
    kr = _rope(acc[:, o3:o3 + LANES], cos, sin, first_half)
    kpe_ref[...] = kr[:, :ROPE_DIM]
    kpeb_ref[...] = kr[:, :ROPE_DIM].astype(BF16)
    cqn = _rms(acc[:, :o1], gqa_ref[...]).astype(BF16)
    q = _dot(cqn, wqb_ref[...])
    for s in range(MLA_HEADS // 2):
        lo = Q_NOPE_ALL + LANES * s
        r = (_rope(q[:, lo:lo + LANES], cos, sin, first_half) * MLA_SCALE).astype(BF16)
        qpe_ref[2 * s] = r[:, :ROPE_DIM]
        qpe_ref[2 * s + 1] = r[:, ROPE_DIM:]
    for h in range(MLA_HEADS):
        ql = _dot(q[:, NOPE_DIM * h:NOPE_DIM * (h + 1)].astype(BF16), wuk_ref[h])
        qlat_ref[h] = (ql * MLA_SCALE).astype(BF16)


def _even_in(xn, w_in, g_qa, g_kva, w_qb, w_ukT, cos, sin, li, tm=256):
    T = xn.shape[0]
    row = lambda i: (i, 0)
    fix2 = lambda i: (0, 0)
    out_shape = (
        jax.ShapeDtypeStruct((T, KV_RANK), F32), jax.ShapeDtypeStruct((T, KV_RANK), BF16),
        jax.ShapeDtypeStruct((T, ROPE_DIM), F32), jax.ShapeDtypeStruct((T, ROPE_DIM), BF16),
        jax.ShapeDtypeStruct((T, POOL_WIDTH), F32),
        jax.ShapeDtypeStruct((MLA_HEADS, T, KV_RANK), BF16),
        jax.ShapeDtypeStruct((MLA_HEADS, T, ROPE_DIM), BF16),
    )
    return pl.pallas_call(
        _even_in_kernel,
        out_shape=out_shape,
        grid=(T // tm,),
        in_specs=[
            pl.BlockSpec((tm, D_MODEL), row),
            pl.BlockSpec((None, D_MODEL, IN_EVEN_PAD), lambda i: (li, 0, 0)),
            pl.BlockSpec((None, 1, Q_RANK), lambda i: (li, 0, 0)),
            pl.BlockSpec((None, 1, KV_RANK), lambda i: (li, 0, 0)),
            pl.BlockSpec((None, Q_RANK, MLA_HEADS * (NOPE_DIM + ROPE_DIM)), lambda i: (li, 0, 0)),
            pl.BlockSpec((None, MLA_HEADS, NOPE_DIM, KV_RANK), lambda i: (li, 0, 0, 0)),
            pl.BlockSpec((tm, LANES), row),
            pl.BlockSpec((tm, LANES), row),
        ],
        out_specs=(
            pl.BlockSpec((tm, KV_RANK), row), pl.BlockSpec((tm, KV_RANK), row),
            pl.BlockSpec((tm, ROPE_DIM), row), pl.BlockSpec((tm, ROPE_DIM), row),
            pl.BlockSpec((tm, POOL_WIDTH), row),
            pl.BlockSpec((MLA_HEADS, tm, KV_RANK), lambda i: (0, i, 0)),
            pl.BlockSpec((MLA_HEADS, tm, ROPE_DIM), lambda i: (0, i, 0)),
        ),
        compiler_params=_params("parallel"),
        name="even_in",
    )(xn, w_in, g_qa, g_kva, w_qb, w_ukT, cos, sin)


ATTN_HEAD_GROUP = 4
ATTN_SAMPLE_SUB_KEYS = 512


def _lane_tile(x, width):
    if width <= LANES:
        return x[:, :width]
    return jnp.concatenate([x] * (width // LANES), axis=1)


def _attn_keys_step(qlat_ref, qpe_ref, k, kp, m_ref, l_ref, acc_ref, tq, group, visible=None):
    tk = k.shape[0]
    n = group * tq

    def scores(g):
        ql = qlat_ref[g * group:(g + 1) * group].reshape(n, KV_RANK)
        qp = qpe_ref[g * group:(g + 1) * group].reshape(n, ROPE_DIM)
        s = _dot_nt(ql, k) + _dot_nt(qp, kp)
        return s if visible is None else jnp.where(visible, s, MASK_NEG)

    s_next = scores(0)
    for g in range(MLA_HEADS // group):
        s = s_next
        if g + 1 < MLA_HEADS // group:
            s_next = scores(g + 1)
        rows = pl.ds(g * n, n)
        m_prev = m_ref[rows, :]
        m_new = jnp.maximum(m_prev, jnp.max(s, axis=1, keepdims=True))
        alpha = jnp.exp(m_prev - m_new)
        p = jnp.exp(s - _lane_tile(m_new, tk))
        l_ref[rows, :] = alpha * l_ref[rows, :] + jnp.sum(p, axis=1, keepdims=True)
        acc_ref[rows, :] = _lane_tile(alpha, KV_RANK) * acc_ref[rows, :] + _dot(p.astype(BF16), k)
        m_ref[rows, :] = m_new


def _attn_init(m_ref, l_ref, acc_ref):
    m_ref[...] = jnp.full(m_ref.shape, MASK_NEG, F32)
    l_ref[...] = jnp.zeros(l_ref.shape, F32)
    acc_ref[...] = jnp.zeros(acc_ref.shape, F32)


def _attn_finish(wuv_ref, y_ref, l_ref, acc_ref, tq):
    for h in range(MLA_HEADS):
        rows = pl.ds(h * tq, tq)
        o = (acc_ref[rows, :] / _lane_tile(l_ref[rows, :], KV_RANK)).astype(BF16)
        y_ref[:, V_DIM * h:V_DIM * (h + 1)] = _dot(o, wuv_ref[h]).astype(BF16)


def _attn_prompt_kernel(qi_ref, kj_ref, qlat_ref, qpe_ref, k_ref, kp_ref, wuv_ref, y_ref,
                        m_ref, l_ref, acc_ref, *, tq):
    p = pl.program_id(1)
    i = qi_ref[p]
    j = kj_ref[p]

    @pl.when(j == 0)
    def _():
        _attn_init(m_ref, l_ref, acc_ref)

    @pl.when(j < i)
    def _():
        _attn_keys_step(qlat_ref, qpe_ref, k_ref[...], kp_ref[...], m_ref, l_ref, acc_ref, tq,
                        ATTN_HEAD_GROUP)

    @pl.when(j == i)
    def _():
        n = ATTN_HEAD_GROUP * tq
        row = lax.broadcasted_iota(jnp.int32, (n, tq), 0) % tq
        col = lax.broadcasted_iota(jnp.int32, (n, tq), 1)
        visible = col // CHUNK <= row // CHUNK
        _attn_keys_step(qlat_ref, qpe_ref, k_ref[...], kp_ref[...], m_ref, l_ref, acc_ref, tq,
                        ATTN_HEAD_GROUP, visible)
        _attn_finish(wuv_ref, y_ref, l_ref, acc_ref, tq)


def _attn_prompt(qlat, qpe, latb, kpeb, w_uvT, li, tq=256):
    assert tq % CHUNK == 0
    T = latb.shape[0]
    nq = SEQ // tq
    pairs = [(i, j) for i in range(nq) for j in range(i + 1)]
    q_of = jnp.asarray([i for i, _ in pairs], jnp.int32)
    k_of = jnp.asarray([j for _, j in pairs], jnp.int32)
    qmap = lambda b, p, qi, kj: (0, b * nq + qi[p], 0)
    kmap = lambda b, p, qi, kj: (b * nq + kj[p], 0)
    rows = MLA_HEADS * tq
    return pl.pallas_call(
        functools.partial(_attn_prompt_kernel, tq=tq),
        out_shape=jax.ShapeDtypeStruct((T, MLA_HEADS * V_DIM), BF16),
        grid_spec=pltpu.PrefetchScalarGridSpec(
            num_scalar_prefetch=2,
            grid=(BATCH, len(pairs)),
            in_specs=[
                pl.BlockSpec((MLA_HEADS, tq, KV_RANK), qmap),
                pl.BlockSpec((MLA_HEADS, tq, ROPE_DIM), qmap),
                pl.BlockSpec((tq, KV_RANK), kmap),
                pl.BlockSpec((tq, ROPE_DIM), kmap),
                pl.BlockSpec((None, MLA_HEADS, KV_RANK, V_DIM), lambda b, p, qi, kj: (li, 0, 0, 0)),
            ],
            out_specs=pl.BlockSpec((tq, MLA_HEADS * V_DIM), lambda b, p, qi, kj: (b * nq + qi[p], 0)),
            scratch_shapes=[pltpu.VMEM((rows, LANES), F32), pltpu.VMEM((rows, LANES), F32),
                            pltpu.VMEM((rows, KV_RANK), F32)],
        ),
        compiler_params=_params("parallel", "arbitrary"),
        name="attn_prompt",
    )(q_of, k_of, qlat, qpe, latb, kpeb, w_uvT)


def _attn_sample_kernel(qlat_ref, qpe_ref, ck_ref, ckp_ref, nk_ref, nkp_ref, wuv_ref, y_ref,
                        m_ref, l_ref, acc_ref):
    j = pl.program_id(1)
    last = pl.num_programs(1) - 1

    @pl.when(j == 0)
    def _():
        _attn_init(m_ref, l_ref, acc_ref)

    def step(k, kp):
        _attn_keys_step(qlat_ref, qpe_ref, k, kp, m_ref, l_ref, acc_ref, DEC_SEQ, MLA_HEADS)

    @pl.when(j < last)
    def _():
        for c in range(0, ck_ref.shape[0], ATTN_SAMPLE_SUB_KEYS):
            keys = pl.ds(c, ATTN_SAMPLE_SUB_KEYS)
            step(ck_ref[keys, :].astype(BF16), ckp_ref[keys, :].astype(BF16))

    @pl.when(j == last)
    def _():
        step(nk_ref[...], nkp_ref[...])
        _attn_finish(wuv_ref, y_ref, l_ref, acc_ref, DEC_SEQ)


def _attn_sample(qlat, qpe, cache_lat, cache_kpe, latb, kpeb, w_uvT, li, tk=1024):
    assert PAST_LEN % CHUNK == 0 and DEC_SEQ <= CHUNK and PAST_LEN % tk == 0
    T = latb.shape[0]
    nc = PAST_LEN // tk
    qmap = lambda b, j: (0, b, 0)
    cmap = lambda b, j: (li, b, jnp.minimum(j, nc - 1), 0)
    rows = MLA_HEADS * DEC_SEQ
    return pl.pallas_call(
        _attn_sample_kernel,
        out_shape=jax.ShapeDtypeStruct((T, MLA_HEADS * V_DIM), BF16),
        grid=(DEC_BATCH, nc + 1),
        in_specs=[
            pl.BlockSpec((MLA_HEADS, DEC_SEQ, KV_RANK), qmap),
            pl.BlockSpec((MLA_HEADS, DEC_SEQ, ROPE_DIM), qmap),
            pl.BlockSpec((None, None, tk, KV_RANK), cmap),
            pl.BlockSpec((None, None, tk, ROPE_DIM), cmap),
            pl.BlockSpec((DEC_SEQ, KV_RANK), lambda b, j: (b, 0)),
            pl.BlockSpec((DEC_SEQ, ROPE_DIM), lambda b, j: (b, 0)),
            pl.BlockSpec((None, MLA_HEADS, KV_RANK, V_DIM), lambda b, j: (li, 0, 0, 0)),
        ],
        out_specs=pl.BlockSpec((DEC_SEQ, MLA_HEADS * V_DIM), lambda b, j: (b, 0)),
        scratch_shapes=[pltpu.VMEM((rows, LANES), F32), pltpu.VMEM((rows, LANES), F32),
                        pltpu.VMEM((rows, KV_RANK), F32)],
        compiler_params=_params("parallel", "arbitrary"),
        name="attn_sample",
    )(qlat, qpe, cache_lat, cache_kpe, latb, kpeb, w_uvT)


POOL_HALO = 2 * SUBLANES


def _pool_kernel(*refs, tp, from_state):
    if from_state:
        z_ref, past_ref, w_ref, scale_ref, y_ref, ext_ref = refs
        ext_ref[0:1, :] = jnp.zeros((1, POOL_WIDTH), F32)
        ext_ref[1:POOL_HALO, :] = past_ref[...]
    else:
        z_ref, w_ref, scale_ref, y_ref, ext_ref = refs
        t = pl.program_id(1)

        @pl.when(t == 0)
        def _():
            ext_ref[0:POOL_HALO, :] = jnp.zeros((POOL_HALO, POOL_WIDTH), F32)

    ext_ref[POOL_HALO:POOL_HALO + tp, :] = z_ref[...]
    for gi, w in enumerate(POOL_WINDOWS):
        lanes = slice(POOL_GROUP_DIM * gi, POOL_GROUP_DIM * (gi + 1))
        tok = ext_ref[POOL_HALO:POOL_HALO + tp, lanes]
        acc = tok
        for d in range(1, w):
            acc = acc + ext_ref[POOL_HALO - d:POOL_HALO - d + tp, lanes]
        if from_state:
            mean = acc / float(w)
        else:
            pos = t * tp + lax.broadcasted_iota(jnp.int32, (tp, 1), 0)
            mean = acc / jnp.minimum(pos + 1, w).astype(F32)
        p = (mean - tok).astype(BF16)
        y_ref[:, lanes] = (_dot(p, w_ref[gi]) * scale_ref[:, lanes]).astype(BF16)
    if not from_state:
        ext_ref[0:POOL_HALO, :] = ext_ref[tp:tp + POOL_HALO, :]


def _pool_prompt(z, w_pool, pool_scale, li, tp=512):
    T = z.shape[0]
    nt = SEQ // tp
    return pl.pallas_call(
        functools.partial(_pool_kernel, tp=tp, from_state=False),
        out_shape=jax.ShapeDtypeStruct((T, POOL_WIDTH), BF16),
        grid=(BATCH, nt),
        in_specs=[
            pl.BlockSpec((tp, POOL_WIDTH), lambda b, t: (b * nt + t, 0)),
            pl.BlockSpec((None, len(POOL_WINDOWS), POOL_GROUP_DIM, POOL_GROUP_DIM), lambda b, t: (li, 0, 0, 0)),
            pl.BlockSpec((None, 1, POOL_WIDTH), lambda b, t: (li, 0, 0)),
        ],
        out_specs=pl.BlockSpec((tp, POOL_WIDTH), lambda b, t: (b * nt + t, 0)),
        scratch_shapes=[pltpu.VMEM((POOL_HALO + tp, POOL_WIDTH), F32)],
        compiler_params=_params("arbitrary", "arbitrary"),
        name="pool_prompt",
    )(z, w_pool, pool_scale)


def _pool_sample(z, state_pool, w_pool, pool_scale, li):
    T = z.shape[0]
    return pl.pallas_call(
        functools.partial(_pool_kernel, tp=DEC_SEQ, from_state=True),
        out_shape=jax.ShapeDtypeStruct((T, POOL_WIDTH), BF16),
        grid=(DEC_BATCH,),
        in_specs=[
            pl.BlockSpec((DEC_SEQ, POOL_WIDTH), lambda b: (b, 0)),
            pl.BlockSpec((None, None, POOL_KEEP, POOL_WIDTH), lambda b: (li, b, 0, 0)),
            pl.BlockSpec((None, len(POOL_WINDOWS), POOL_GROUP_DIM, POOL_GROUP_DIM), lambda b: (li, 0, 0, 0)),
            pl.BlockSpec((None, 1, POOL_WIDTH), lambda b: (li, 0, 0)),
        ],
        out_specs=pl.BlockSpec((DEC_SEQ, POOL_WIDTH), lambda b: (b, 0)),
        scratch_shapes=[pltpu.VMEM((POOL_HALO + DEC_SEQ, POOL_WIDTH), F32)],
        compiler_params=_params("parallel"),
        name="pool_sample",
    )(z, state_pool, w_pool, pool_scale)


def _proj_res_norm_kernel(*refs, n_a, head_major, emit_x):
    a_refs = refs[:n_a]
    w_refs = refs[n_a:2 * n_a]
    x_ref, g_ref = refs[2 * n_a:2 * n_a + 2]
    outs = refs[2 * n_a + 2:]
    if head_major:
        (a_ref,), (w_ref,) = a_refs, w_refs
        xo_ref, xn_ref, cat_ref = outs
        for h in range(C_HEADS):
            cat_ref[:, HEAD_I * h:HEAD_I * (h + 1)] = a_ref[h]
        acc = _dot(cat_ref[...], w_ref[...])
    else:
        xo_ref, xn_ref = outs if emit_x else (None, outs[0])
        acc = _dot(a_refs[0][...], w_refs[0][...])
        for a_ref, w_ref in zip(a_refs[1:], w_refs[1:]):
            acc = acc + _dot(a_ref[...], w_ref[...])
    xnew = x_ref[...] + acc
    if xo_ref is not None:
        xo_ref[...] = xnew
    xn_ref[...] = _rms(xnew, g_ref[...]).astype(xn_ref.dtype)


def _proj_res_norm(a_list, w, x, g, li, head_major=False, tm=512):
    T = x.shape[0]
    row = lambda i: (i, 0)
    n_a = len(a_list)
    in_specs = []
    for a in a_list:
        if head_major:
            in_specs.append(pl.BlockSpec((C_HEADS, tm, HEAD_I), lambda i: (0, i, 0)))
        else:
            in_specs.append(pl.BlockSpec((tm, a.shape[1]), row))
    for kb, a in enumerate(a_list):
        kdim = C_HEADS * HEAD_I if head_major else a.shape[1]
        in_specs.append(pl.BlockSpec((None, kdim, D_MODEL), lambda i, kb=kb: (li, kb, 0)))
    in_specs += [pl.BlockSpec((tm, D_MODEL), row), pl.BlockSpec((1, D_MODEL), lambda i: (0, 0))]
    scratch = [pltpu.VMEM((tm, D_MODEL), BF16)] if head_major else []
    return pl.pallas_call(
        functools.partial(_proj_res_norm_kernel, n_a=n_a, head_major=head_major, emit_x=True),
        out_shape=(jax.ShapeDtypeStruct((T, D_MODEL), F32), jax.ShapeDtypeStruct((T, D_MODEL), BF16)),
        grid=(T // tm,),
        in_specs=in_specs,
        out_specs=(pl.BlockSpec((tm, D_MODEL), row), pl.BlockSpec((tm, D_MODEL), row)),
        scratch_shapes=scratch,
        compiler_params=_params("parallel"),
        name="proj_res_norm",
    )(*a_list, *([w] * n_a), x, g)


def _odd_in_kernel(xn_ref, w_ref, lbp_ref, o_ref, *, li):
    s = pl.program_id(0)
    h = _dot(xn_ref[...], w_ref[...])
    sg = jax.nn.sigmoid(h)
    p = lbp_ref[...]
    e = jnp.exp(p - jnp.max(p, axis=0, keepdims=True))
    sm = e / jnp.sum(e, axis=0, keepdims=True)
    lb = jnp.clip(jnp.sum(sm[:li + 1], axis=0, keepdims=True) - sm[0:1], 0.0, 1.0)
    f = lb + (1.0 - lb) * sg
    out = jnp.where(s == 1, f, jnp.where(s == 2, h, h * sg))
    for hd in range(C_HEADS):
        o_ref[hd] = out[:, HEAD_I * hd:HEAD_I * (hd + 1)]


def _odd_in(xn, w_in_c, lb_param, li, tm=1024):
    T = xn.shape[0]
    return pl.pallas_call(
        functools.partial(_odd_in_kernel, li=li),
        out_shape=jax.ShapeDtypeStruct((4, C_HEADS, T, HEAD_I), F32),
        grid=(4, T // tm),
        in_specs=[
            pl.BlockSpec((tm, D_MODEL), lambda s, i: (i, 0)),
            pl.BlockSpec((None, D_MODEL, D_MODEL), lambda s, i: (li, 0, s)),
            pl.BlockSpec(lb_param.shape, lambda s, i: (0, 0)),
        ],
        out_specs=pl.BlockSpec((None, C_HEADS, tm, HEAD_I), lambda s, i: (s, 0, i, 0)),
        compiler_params=_params("parallel", "parallel"),
        name="odd_in",
    )(xn, w_in_c, lb_param)


def _gla_kernel(*refs, has_s0):
    if has_s0:
        q_ref, f_ref, v_ref, gs_ref, gon_ref, s0_ref, og_ref, sout_ref, st_ref = refs
    else:
        q_ref, f_ref, v_ref, gs_ref, gon_ref, og_ref, sout_ref, st_ref = refs
    C = GLA_CHUNK
    n = pl.program_id(1)
    last = pl.num_programs(1) - 1

    @pl.when(n == 0)
    def _():
        if has_s0:
            def load(h, c):
                st_ref[h] = s0_ref[h].T
                return c
            lax.fori_loop(0, C_HEADS, load, 0)
        else:
            st_ref[...] = jnp.zeros(st_ref.shape, F32)

    r2 = lax.broadcasted_iota(jnp.int32, (C, C), 0)
    c2 = lax.broadcasted_iota(jnp.int32, (C, C), 1)
    tri = (c2 <= r2).astype(BF16)
    ones = jnp.ones((HEAD_F, LANES), BF16)
    gon = gon_ref[...]
    levels = []
    m = GLA_SUB
    while m < C:
        keep = (r2 // (2 * m) == c2 // (2 * m)) & (r2 % (2 * m) >= m) & (c2 % (2 * m) < m)
        levels.append((m, keep))
        m *= 2
    near = [(c2 == r2 - delta) & (r2 % GLA_SUB >= delta) for delta in range(GLA_SUB)]

    def bcast_row(a, period, r):
        a3 = a.reshape(C // period, period, a.shape[-1])
        return jnp.broadcast_to(a3[:, r:r + 1, :], a3.shape).reshape(C, a.shape[-1])

    def shift_in_group(a, delta):
        if delta == 0:
            return a
        a3 = a.reshape(C // GLA_SUB, GLA_SUB, a.shape[-1])
        return pltpu.roll(a3, delta, 1).reshape(C, a.shape[-1])

    def split3(g):
        g1 = g.astype(BF16)
        r1 = g - g1.astype(F32)
        g2 = r1.astype(BF16)
        return g1, g2, (r1 - g2.astype(F32)).astype(BF16)

    def head_group(hg, carry):
        hs = [hg * GLA_HEADS_PER_ITER + u for u in range(GLA_HEADS_PER_ITER)]
        U = range(len(hs))
        q = [q_ref[h] for h in hs]
        f = [f_ref[h] for h in hs]
        v_bf = [v_ref[h].astype(BF16) for h in hs]
        fc = [jnp.maximum(x, F_MIN) for x in f]
        kk = [1.0 - x for x in f]
        gp = [split3(jnp.log2(x)) for x in fc]
        b = [_dot(tri, gp[u][0]) + _dot(tri, gp[u][1]) + _dot(tri, gp[u][2]) for u in U]
        st = [st_ref[h] for h in hs]
        o = [_dot_nt((q[u] * jnp.exp2(b[u])).astype(BF16), st[u].astype(BF16)) for u in U]
        b_last = [x[C - 1:C, :] for x in b]
        k_dec = [(kk[u] * jnp.exp2(b_last[u] - b[u])).astype(BF16) for u in U]
        for u in U:
            st_ref[hs[u]] = st[u] * jnp.exp2(b_last[u]) + _dot_tn(v_bf[u], k_dec[u])

        a = [jnp.zeros((C, C), F32) for _ in U]
        for m, keep in levels:
            for u in U:
                d = b[u] - bcast_row(b[u], 2 * m, m - 1)
                qt = (q[u] * jnp.exp2(jnp.minimum(d, 0.0))).astype(BF16)
                kt = (kk[u] * jnp.exp2(jnp.minimum(-d, 0.0))).astype(BF16)
                a[u] = a[u] + jnp.where(keep, _dot_nt(qt, kt), 0.0)
        dec = [None for _ in U]
        for delta in range(GLA_SUB):
            for u in U:
                p = q[u] * shift_in_group(kk[u], delta)
                if delta > 0:
                    f_sh = shift_in_group(fc[u], delta - 1)
                    dec[u] = f_sh if dec[u] is None else dec[u] * f_sh
                    p = p * dec[u]
                a[u] = a[u] + jnp.where(near[delta], _dot(p.astype(BF16), ones)[:, :C], 0.0)
        for u in U:
            y = _rms(o[u] + _dot(a[u].astype(BF16), v_bf[u]), gon) * gs_ref[hs[u]]
            og_ref[hs[u]] = y.astype(BF16)
        return carry

    lax.fori_loop(0, C_HEADS // GLA_HEADS_PER_ITER, head_group, 0)

    @pl.when(n == last)
    def _():
        def store(h, c):
            sout_ref[h] = st_ref[h].T
            return c
        lax.fori_loop(0, C_HEADS, store, 0)


def _gla(qfvg, g_onorm, s0, li, n_streams, seq):
    T = qfvg.shape[2]
    C = GLA_CHUNK
    nc = seq // C
    has_s0 = s0 is not None
    sec = lambda k: pl.BlockSpec((None, C_HEADS, C, HEAD_F), lambda b, n, k=k: (k, 0, b * nc + n, 0))
    in_specs = [sec(0), sec(1), sec(2), sec(3),
                pl.BlockSpec((None, 1, HEAD_I), lambda b, n: (li, 0, 0))]
    args = [qfvg, qfvg, qfvg, qfvg, g_onorm]
    if has_s0:
        in_specs.append(pl.BlockSpec((None, None, C_HEADS, HEAD_F, HEAD_I), lambda b, n: (li, b, 0, 0, 0)))
        args.append(s0)
    return pl.pallas_call(
        functools.partial(_gla_kernel, has_s0=has_s0),
        out_shape=(jax.ShapeDtypeStruct((C_HEADS, T, HEAD_I), BF16),
                   jax.ShapeDtypeStruct((n_streams, C_HEADS, HEAD_F, HEAD_I), F32)),
        grid=(n_streams, nc),
        in_specs=in_specs,
        out_specs=(pl.BlockSpec((C_HEADS, C, HEAD_I), lambda b, n: (0, b * nc + n, 0)),
                   pl.BlockSpec((None, C_HEADS, HEAD_F, HEAD_I), lambda b, n: (b, 0, 0, 0))),
        scratch_shapes=[pltpu.VMEM((C_HEADS, HEAD_I, HEAD_F), F32)],
        compiler_params=_params("parallel", "arbitrary"),
        name="gla",
    )(*args)


FFN_PAD = SUBLANES
FFN_SUB_ROWS = 256


def _ffn_up_kernel(*refs, tm, seg, from_state, tiles_per_stream):
    if from_state:
        (xn_ref, wa_ref, wb_ref, cwa_ref, cwb_ref, cba_ref, cbb_ref, pa_ref, pb_ref,
         act_ref, hla_ref, hlb_ref) = refs
        carry_ref = None
    else:
        (xn_ref, wa_ref, wb_ref, cwa_ref, cwb_ref, cba_ref, cbb_ref,
         act_ref, hla_ref, hlb_ref, carry_ref) = refs
        pa_ref = pb_ref = None
    i = pl.program_id(0)
    j = pl.program_id(1)
    sub = min(FFN_SUB_ROWS, tm)
    assert seg % sub == 0 or sub % seg == 0
    halves = ((0, wa_ref, cwa_ref, cba_ref, pa_ref, hla_ref), (1, wb_ref, cwb_ref, cbb_ref, pb_ref, hlb_ref))

    def conv(front, h, cw, cb):
        n = h.shape[0]
        ext = jnp.concatenate([front, h], axis=0)
        return cb + (ext[FFN_PAD - 2:FFN_PAD - 2 + n] * cw[0:1] + ext[FFN_PAD - 1:FFN_PAD - 1 + n] * cw[1:2] +
                     h * cw[2:3])

    fronts = [None, None]
    if not from_state:
        @pl.when(i % tiles_per_stream == 0)
        def _():
            carry_ref[j] = jnp.zeros((2, FFN_PAD, FFN_TN), F32)
        fronts = [carry_ref[j, 0], carry_ref[j, 1]]

    for r0 in range(0, tm, sub):
        convs = []
        for hidx, w_ref, cw_ref, cb_ref, past_ref, hl_ref in halves:
            h = _dot(xn_ref[r0:r0 + sub, :], w_ref[...])
            cw = cw_ref[...]
            cb = cb_ref[...]
            if seg >= sub:
                convs.append([conv(fronts[hidx], h, cw, cb)])
                fronts[hidx] = h[sub - FFN_PAD:sub]
                if (r0 + sub) % seg == 0:
                    hl_ref[(r0 + sub) // seg - 1] = h[sub - 2:sub]
            else:
                pieces = []
                for k in range(sub // seg):
                    s = r0 // seg + k
                    hk = h[k * seg:(k + 1) * seg]
                    front = jnp.concatenate([jnp.zeros((FFN_PAD - 2, FFN_TN), F32), past_ref[s]], axis=0)
                    pieces.append(conv(front, hk, cw, cb))
                    hl_ref[s] = hk[seg - 2:seg]
                convs.append(pieces)
        for n, (a, b) in enumerate(zip(*convs)):
            rows = a.shape[0]
            o = r0 + n * rows
            act_ref[o:o + rows, :] = (a * jax.nn.sigmoid(a) * b).astype(BF16)

    if not from_state:
        carry_ref[j, 0] = fronts[0]
        carry_ref[j, 1] = fronts[1]


def _ffn_up(xn, w_up, conv_w, conv_b, conv_past, layer, tm, seg):
    T = xn.shape[0]
    tn = FFN_TN
    nj = D_FF // tn
    nseg = tm // seg
    from_state = conv_past is not None
    wmap_a = lambda i, j: (layer, 0, j)
    wmap_b = lambda i, j: (layer, 0, nj + j)
    in_specs = [
        pl.BlockSpec((tm, D_MODEL), lambda i, j: (i, 0)),
        pl.BlockSpec((None, D_MODEL, tn), wmap_a), pl.BlockSpec((None, D_MODEL, tn), wmap_b),
        pl.BlockSpec((None, CONV_W, tn), wmap_a), pl.BlockSpec((None, CONV_W, tn), wmap_b),
        pl.BlockSpec((None, 1, tn), wmap_a), pl.BlockSpec((None, 1, tn), wmap_b),
    ]
    args = [xn, w_up, w_up, conv_w, conv_w, conv_b, conv_b]
    scratch = []
    if from_state:
        in_specs += [pl.BlockSpec((None, nseg, CONV_W - 1, tn), lambda i, j: (layer, i, 0, j)),
                     pl.BlockSpec((None, nseg, CONV_W - 1, tn), lambda i, j: (layer, i, 0, nj + j))]
        args += [conv_past, conv_past]
        tiles_per_stream = 1
    else:
        scratch.append(pltpu.VMEM((nj, 2, FFN_PAD, tn), F32))
        tiles_per_stream = SEQ // tm
    hl_shape = jax.ShapeDtypeStruct((T // seg, CONV_W - 1, D_FF), F32)
    hl_spec = pl.BlockSpec((nseg, CONV_W - 1, tn), lambda i, j: (i, 0, j))
    return pl.pallas_call(
        functools.partial(_ffn_up_kernel, tm=tm, seg=seg, from_state=from_state,
                          tiles_per_stream=tiles_per_stream),
        out_shape=(jax.ShapeDtypeStruct((T, D_FF), BF16), hl_shape, hl_shape),
        grid=(T // tm, nj),
        in_specs=in_specs,
        out_specs=(pl.BlockSpec((tm, tn), lambda i, j: (i, j)), hl_spec, hl_spec),
        scratch_shapes=scratch,
        compiler_params=_params("arbitrary", "arbitrary"),
        name="ffn_up",
    )(*args)


def _ffn_down_kernel(a_ref, w_ref, x_ref, g_ref, *outs, emit_x):
    if emit_x:
        acc_ref, xn_ref = outs
    else:
        xn_ref, acc_ref = outs
    k = pl.program_id(1)

    @pl.when(k == 0)
    def _():
        acc_ref[...] = x_ref[...]

    acc_ref[...] += _dot(a_ref[...], w_ref[...])

    @pl.when(k == pl.num_programs(1) - 1)
    def _():
        xn_ref[...] = _rms(acc_ref[...], g_ref[...]).astype(xn_ref.dtype)


def _ffn_down(act, w_down, x, g, layer, emit_x, xn_dtype, tm=1024, nk=11):
    T = x.shape[0]
    tk = D_FF // nk
    row = lambda i, k: (i, 0)
    out_shape = [jax.ShapeDtypeStruct((T, D_MODEL), xn_dtype)]
    out_specs = [pl.BlockSpec((tm, D_MODEL), row)]
    scratch = [pltpu.VMEM((tm, D_MODEL), F32)]
    if emit_x:
        out_shape.insert(0, jax.ShapeDtypeStruct((T, D_MODEL), F32))
        out_specs.insert(0, pl.BlockSpec((tm, D_MODEL), row))
        scratch = []
    res = pl.pallas_call(
        functools.partial(_ffn_down_kernel, emit_x=emit_x),
        out_shape=tuple(out_shape),
        grid=(T // tm, nk),
        in_specs=[
            pl.BlockSpec((tm, tk), lambda i, k: (i, k)),
            pl.BlockSpec((None, tk, D_MODEL), lambda i, k: (layer, k, 0)),
            pl.BlockSpec((tm, D_MODEL), row),
            pl.BlockSpec((1, D_MODEL), lambda i, k: (0, 0)),
        ],
        out_specs=tuple(out_specs),
        scratch_shapes=scratch,
        compiler_params=_params("parallel", "arbitrary"),
        name="ffn_down",
    )(act, w_down, x, g)
    return res if emit_x else (None, res[0])


def _rope_tables(pos):
    inv = ROPE_THETA ** (-jnp.arange(0, ROPE_DIM, 2, dtype=F32) / ROPE_DIM)
    ang = pos.astype(F32)[:, None] * inv[None, :]
    cos, sin = jnp.cos(ang), jnp.sin(ang)
    reps = LANES // ROPE_DIM
    return (jnp.tile(jnp.concatenate([cos, cos], axis=-1), (1, reps)),
            jnp.tile(jnp.concatenate([-sin, sin], axis=-1), (1, reps)))


def kernel(x_prompt, x_sample, cache_mla_latent, cache_mla_krope, state_pool, state_hgrn, state_ffn_conv,
           g_mix, g_ffn, g_final, w_in_a, g_qa, w_qb, g_kva, w_uk, w_uv, w_pool, pool_scale, w_out_a,
           w_in_c, lb_param, g_onorm, w_out_c, w_up, conv_w, conv_b, w_down):
    n_even = w_in_a.shape[0]
    n_odd = w_in_c.shape[0]
    Tp, Ts = BATCH * SEQ, DEC_BATCH * DEC_SEQ

    o1, o2, o3 = Q_RANK, Q_RANK + KV_RANK, Q_RANK + KV_RANK + ROPE_DIM
    w_in_e = jnp.concatenate(
        [w_in_a[..., :o2], w_in_a[..., o3:], w_in_a[..., o2:o3],
         jnp.zeros((n_even, D_MODEL, LANES - ROPE_DIM), w_in_a.dtype)], axis=-1).astype(BF16)
    w_qb4 = w_qb.reshape(n_even, Q_RANK, MLA_HEADS, NOPE_DIM + ROPE_DIM)
    w_qb_e = jnp.concatenate(
        [w_qb4[..., :NOPE_DIM].reshape(n_even, Q_RANK, Q_NOPE_ALL),
         w_qb4[..., NOPE_DIM:].reshape(n_even, Q_RANK, MLA_HEADS * ROPE_DIM)], axis=-1).astype(BF16)
    w_ukT = jnp.transpose(w_uk, (0, 2, 3, 1)).astype(BF16)
    w_uvT = jnp.transpose(w_uv, (0, 2, 1, 3)).astype(BF16)
    w_pool_b = w_pool.astype(BF16)
    w_out_a_b = w_out_a.astype(BF16)
    w_in_c_b = w_in_c.astype(BF16)
    w_out_c_b = w_out_c.astype(BF16)
    w_up_b = w_up.astype(BF16)
    w_down_b = w_down.astype(BF16)
    g_qa3 = g_qa[:, None, :]
    g_kva3 = g_kva[:, None, :]
    pool_scale3 = pool_scale[:, None, :]
    g_onorm3 = g_onorm[:, None, :]
    conv_b3 = conv_b[:, None, :]

    cos_p, sin_p = _rope_tables(jnp.tile(jnp.arange(SEQ), BATCH))
    cos_s, sin_s = _rope_tables(jnp.tile(PAST_LEN + jnp.arange(DEC_SEQ), DEC_BATCH))

    xp = x_prompt.reshape(Tp, D_MODEL)
    xs = x_sample.reshape(Ts, D_MODEL)
    xnp = _norm(xp, g_mix[0:1])
    xns = _norm(xs, g_mix[0:1])

    lat_p, kpe_p, pool_p, hg_p, cv_p = [], [], [], [], []
    lat_s, kpe_s, pool_s, hg_s, cv_s = [], [], [], [], []
    for layer in range(DEPTH):
        li = layer // 2
        g_next = g_ffn[layer:layer + 1]
        if layer % 2 == 0:
            wa = (w_in_e, g_qa3, g_kva3, w_qb_e, w_ukT)
            lat, latb, kpe, kpeb, z, qlat, qpe = _even_in(xnp, *wa, cos_p, sin_p, li)
            y_mla = _attn_prompt(qlat, qpe, latb, kpeb, w_uvT, li)
            y_pool = _pool_prompt(z, w_pool_b, pool_scale3, li)
            xp, xnp = _proj_res_norm([y_mla, y_pool], w_out_a_b, xp, g_next, li)
            lat_p.append(lat.reshape(BATCH, SEQ, KV_RANK))
            kpe_p.append(kpe.reshape(BATCH, SEQ, ROPE_DIM))
            pool_p.append(z.reshape(BATCH, SEQ, POOL_WIDTH)[:, SEQ - POOL_KEEP:])

            lat, latb, kpe, kpeb, z, qlat, qpe = _even_in(xns, *wa, cos_s, sin_s, li)
            y_mla = _attn_sample(qlat, qpe, cache_mla_latent, cache_mla_krope, latb, kpeb, w_uvT, li)
            y_pool = _pool_sample(z, state_pool, w_pool_b, pool_scale3, li)
            xs, xns = _proj_res_norm([y_mla, y_pool], w_out_a_b, xs, g_next, li)
            lat_s.append(lat.reshape(DEC_BATCH, DEC_SEQ, KV_RANK))
            kpe_s.append(kpe.reshape(DEC_BATCH, DEC_SEQ, ROPE_DIM))
            pool_s.append(z.reshape(DEC_BATCH, DEC_SEQ, POOL_WIDTH)[:, DEC_SEQ - POOL_KEEP:])
        else:
            qfvg = _odd_in(xnp, w_in_c_b, lb_param, li)
            og, s_fin = _gla(qfvg, g_onorm3, None, li, BATCH, SEQ)
            xp, xnp = _proj_res_norm([og], w_out_c_b, xp, g_next, li, head_major=True)
            hg_p.append(s_fin)

            qfvg = _odd_in(xns, w_in_c_b, lb_param, li)
            og, s_fin = _gla(qfvg, g_onorm3, state_hgrn, li, DEC_BATCH, DEC_SEQ)
            xs, xns = _proj_res_norm([og], w_out_c_b, xs, g_next, li, head_major=True)
            hg_s.append(s_fin)

        final = layer == DEPTH - 1
        g_after = g_final[None, :] if final else g_mix[layer + 1:layer + 2]
        xn_dtype = F32 if final else BF16

        tm_p = 1024
        act, hla, hlb = _ffn_up(xnp, w_up_b, conv_w, conv_b3, None, layer, tm=tm_p, seg=tm_p)
        xp, xnp = _ffn_down(act, w_down_b, xp, g_after, layer, not final, xn_dtype)
        hl = jnp.concatenate([hla, hlb], axis=-1).reshape(BATCH, SEQ // tm_p, CONV_W - 1, 2 * D_FF)
        cv_p.append(hl[:, -1])

        act, hla, hlb = _ffn_up(xns, w_up_b, conv_w, conv_b3, state_ffn_conv, layer, tm=512, seg=DEC_SEQ)
        xs, xns = _ffn_down(act, w_down_b, xs, g_after, layer, not final, xn_dtype)
        cv_s.append(jnp.concatenate([hla, hlb], axis=-1))

    return (xnp.reshape(BATCH, SEQ, D_MODEL), xns.reshape(DEC_BATCH, DEC_SEQ, D_MODEL),
            jnp.stack(lat_p), jnp.stack(kpe_p), jnp.stack(pool_p), jnp.stack(hg_p), jnp.stack(cv_p),
            jnp.stack(lat_s), jnp.stack(kpe_s), jnp.stack(pool_s), jnp.stack(hg_s), jnp.stack(cv_s))
```

```python
import functools

import jax
import jax.numpy as jnp
from jax import lax
from jax.experimental import pallas as pl
from jax.experimental.pallas import tpu as pltpu

D_MODEL = 2048
BATCH = 8
SEQ = 2048
DEPTH = 4
DEC_BATCH = 32
DEC_SEQ = 64
PAST_LEN = 4096
CHUNK = 64
MLA_HEADS = 8
Q_RANK = 512
KV_RANK = 512
NOPE_DIM = 128
ROPE_DIM = 64
V_DIM = 128
ROPE_THETA = 10000.0
MLA_SCALE = (NOPE_DIM + ROPE_DIM) ** -0.5
MASK_NEG = -1e30
POOL_WINDOWS = (2, 4, 8, 16)
POOL_WIDTH = D_MODEL - MLA_HEADS * V_DIM
POOL_GROUP_DIM = POOL_WIDTH // len(POOL_WINDOWS)
POOL_KEEP = max(POOL_WINDOWS) - 1
C_HEADS = 16
HEAD_F = 128
HEAD_I = D_MODEL // C_HEADS
F_MIN = 1e-30
D_FF = 5632
CONV_W = 3
EPS = 1e-6

F32 = jnp.float32
BF16 = jnp.bfloat16

LANES = 128
SUBLANES = 8
VMEM_LIMIT = 56 * 1024 * 1024

IN_EVEN_PAD = Q_RANK + KV_RANK + POOL_WIDTH + LANES
Q_NOPE_ALL = MLA_HEADS * NOPE_DIM
GLA_CHUNK = 64
GLA_SUB = SUBLANES
GLA_HEADS_PER_ITER = 16
FFN_TN = 512


def _params(*sem):
    return pltpu.CompilerParams(dimension_semantics=sem, vmem_limit_bytes=VMEM_LIMIT)


def _dot(a, b):
    return jnp.dot(a, b, preferred_element_type=F32)


def _dot_nt(a, b):
    return lax.dot_general(a, b, (((1,), (1,)), ((), ())), preferred_element_type=F32)


def _dot_tn(a, b):
    return lax.dot_general(a, b, (((0,), (0,)), ((), ())), preferred_element_type=F32)


def _rms(x, g):
    return x * lax.rsqrt(jnp.mean(x * x, axis=-1, keepdims=True) + EPS) * g


def _norm_kernel(x_ref, g_ref, xn_ref):
    xn_ref[...] = _rms(x_ref[...], g_ref[...]).astype(xn_ref.dtype)


def _norm(x, g, tm=512):
    T = x.shape[0]
    return pl.pallas_call(
        _norm_kernel,
        out_shape=jax.ShapeDtypeStruct((T, D_MODEL), BF16),
        grid=(T // tm,),
        in_specs=[pl.BlockSpec((tm, D_MODEL), lambda i: (i, 0)),
                  pl.BlockSpec((1, D_MODEL), lambda i: (0, 0))],
        out_specs=pl.BlockSpec((tm, D_MODEL), lambda i: (i, 0)),
        compiler_params=_params("parallel"),
        name="norm0",
    )(x, g)


def _rope(s, cos, sin, first_half):
    swapped = jnp.where(first_half, pltpu.roll(s, 96, 1), pltpu.roll(s, 32, 1))
    return s * cos + swapped * sin


def _even_in_kernel(xn_ref, w_ref, gqa_ref, gkva_ref, wqb_ref, wuk_ref, cos_ref, sin_ref,
                    lat_ref, latb_ref, kpe_ref, kpeb_ref, z_ref, qlat_ref, qpe_ref):
    tm = xn_ref.shape[0]
    acc = _dot(xn_ref[...], w_ref[...])
    o1, o2, o3 = Q_RANK, Q_RANK + KV_RANK, Q_RANK + KV_RANK + POOL_WIDTH
    z_ref[...] = acc[:, o2:o3]
    lat = _rms(acc[:, o1:o2], gkva_ref[...])
    lat_ref[...] = lat
    latb_ref[...] = lat.astype(BF16)
    cos = cos_ref[...]
    sin = sin_ref[...]
    lane = lax.broadcasted_iota(jnp.int32, (tm, LANES), 1)
    first_half = (lane % ROPE_DIM) < (ROPE_DIM // 2)
    kr = _rope(acc[:, o3:o3 + LANES], cos, sin, first_half)
    kpe_ref[...] = kr[:, :ROPE_DIM]
    kpeb_ref[...] = kr[:, :ROPE_DIM].astype(BF16)
    cqn = _rms(acc[:, :o1], gqa_ref[...]).astype(BF16)
    q = _dot(cqn, wqb_ref[...])
    for s in range(MLA_HEADS // 2):
        lo = Q_NOPE_ALL + LANES * s
        r = (_rope(q[:, lo:lo + LANES], cos, sin, first_half) * MLA_SCALE).astype(BF16)
        qpe_ref[2 * s] = r[:, :ROPE_DIM]
        qpe_ref[2 * s + 1] = r[:, ROPE_DIM:]
    for h in range(MLA_HEADS):
        ql = _dot(q[:, NOPE_DIM * h:NOPE_DIM * (h + 1)].astype(BF16), wuk_ref[h])
        qlat_ref[h] = (ql * MLA_SCALE).astype(BF16)


def _even_in(xn, w_in, g_qa, g_kva, w_qb, w_ukT, cos, sin, li, tm=256):
    T = xn.shape[0]
    row = lambda i: (i, 0)
    fix2 = lambda i: (0, 0)
    out_shape = (
        jax.ShapeDtypeStruct((T, KV_RANK), F32), jax.ShapeDtypeStruct((T, KV_RANK), BF16),
        jax.ShapeDtypeStruct((T, ROPE_DIM), F32), jax.ShapeDtypeStruct((T, ROPE_DIM), BF16),
        jax.ShapeDtypeStruct((T, POOL_WIDTH), F32),
        jax.ShapeDtypeStruct((MLA_HEADS, T, KV_RANK), BF16),
        jax.ShapeDtypeStruct((MLA_HEADS, T, ROPE_DIM), BF16),
    )
    return pl.pallas_call(
        _even_in_kernel,
        out_shape=out_shape,
        grid=(T // tm,),
        in_specs=[
            pl.BlockSpec((tm, D_MODEL), row),
            pl.BlockSpec((None, D_MODEL, IN_EVEN_PAD), lambda i: (li, 0, 0)),
            pl.BlockSpec((None, 1, Q_RANK), lambda i: (li, 0, 0)),
            pl.BlockSpec((None, 1, KV_RANK), lambda i: (li, 0, 0)),
            pl.BlockSpec((None, Q_RANK, MLA_HEADS * (NOPE_DIM + ROPE_DIM)), lambda i: (li, 0, 0)),
            pl.BlockSpec((None, MLA_HEADS, NOPE_DIM, KV_RANK), lambda i: (li, 0, 0, 0)),
            pl.BlockSpec((tm, LANES), row),
            pl.BlockSpec((tm, LANES), row),
        ],
        out_specs=(
            pl.BlockSpec((tm, KV_RANK), row), pl.BlockSpec((tm, KV_RANK), row),
            pl.BlockSpec((tm, ROPE_DIM), row), pl.BlockSpec((tm, ROPE_DIM), row),
            pl.BlockSpec((tm, POOL_WIDTH), row),
            pl.BlockSpec((MLA_HEADS, tm, KV_RANK), lambda i: (0, i, 0)),
            pl.BlockSpec((MLA_HEADS, tm, ROPE_DIM), lambda i: (0, i, 0)),
        ),
        compiler_params=_params("parallel"),
        name="even_in",
    )(xn, w_in, g_qa, g_kva, w_qb, w_ukT, cos, sin)


ATTN_HEAD_GROUP = 2
ATTN_SAMPLE_SUB_KEYS = 512


def _lane_tile(x, width):
    if width <= LANES:
        return x[:, :width]
    return jnp.concatenate([x] * (width // LANES), axis=1)


def _attn_keys_step(qlat_ref, qpe_ref, k, kp, m_ref, l_ref, acc_ref, tq, group, visible=None):
    tk = k.shape[0]
    n = group * tq

    def scores(g):
        ql = qlat_ref[g * group:(g + 1) * group].reshape(n, KV_RANK)
        qp = qpe_ref[g * group:(g + 1) * group].reshape(n, ROPE_DIM)
        s = _dot_nt(ql, k) + _dot_nt(qp, kp)
        return s if visible is None else jnp.where(visible, s, MASK_NEG)

    s_next = scores(0)
    for g in range(MLA_HEADS // group):
        s = s_next
        if g + 1 < MLA_HEADS // group:
            s_next = scores(g + 1)
        rows = pl.ds(g * n, n)
        m_prev = m_ref[rows, :]
        m_new = jnp.maximum(m_prev, jnp.max(s, axis=1, keepdims=True))
        alpha = jnp.exp(m_prev - m_new)
        p = jnp.exp(s - _lane_tile(m_new, tk))
        l_ref[rows, :] = alpha * l_ref[rows, :] + jnp.sum(p, axis=1, keepdims=True)
        acc_ref[rows, :] = _lane_tile(alpha, KV_RANK) * acc_ref[rows, :] + _dot(p.astype(BF16), k)
        m_ref[rows, :] = m_new


def _attn_init(m_ref, l_ref, acc_ref):
    m_ref[...] = jnp.full(m_ref.shape, MASK_NEG, F32)
    l_ref[...] = jnp.zeros(l_ref.shape, F32)
    acc_ref[...] = jnp.zeros(acc_ref.shape, F32)


def _attn_finish(wuv_ref, y_ref, l_ref, acc_ref, tq):
    for h in range(MLA_HEADS):
        rows = pl.ds(h * tq, tq)
        o = (acc_ref[rows, :] / _lane_tile(l_ref[rows, :], KV_RANK)).astype(BF16)
        y_ref[:, V_DIM * h:V_DIM * (h + 1)] = _dot(o, wuv_ref[h]).astype(BF16)


def _attn_prompt_kernel(qi_ref, kj_ref, qlat_ref, qpe_ref, k_ref, kp_ref, wuv_ref, y_ref,
                        m_ref, l_ref, acc_ref, *, tq):
    p = pl.program_id(1)
    i = qi_ref[p]
    j = kj_ref[p]

    @pl.when(j == 0)
    def _():
        _attn_init(m_ref, l_ref, acc_ref)

    @pl.when(j < i)
    def _():
        _attn_keys_step(qlat_ref, qpe_ref, k_ref[...], kp_ref[...], m_ref, l_ref, acc_ref, tq,
                        ATTN_HEAD_GROUP)

    @pl.when(j == i)
    def _():
        n = ATTN_HEAD_GROUP * tq
        row = lax.broadcasted_iota(jnp.int32, (n, tq), 0) % tq
        col = lax.broadcasted_iota(jnp.int32, (n, tq), 1)
        visible = col // CHUNK <= row // CHUNK
        _attn_keys_step(qlat_ref, qpe_ref, k_ref[...], kp_ref[...], m_ref, l_ref, acc_ref, tq,
                        ATTN_HEAD_GROUP, visible)
        _attn_finish(wuv_ref, y_ref, l_ref, acc_ref, tq)


def _attn_prompt(qlat, qpe, latb, kpeb, w_uvT, li, tq=256):
    assert tq % CHUNK == 0
    T = latb.shape[0]
    nq = SEQ // tq
    pairs = [(i, j) for i in range(nq) for j in range(i + 1)]
    q_of = jnp.asarray([i for i, _ in pairs], jnp.int32)
    k_of = jnp.asarray([j for _, j in pairs], jnp.int32)
    qmap = lambda b, p, qi, kj: (0, b * nq + qi[p], 0)
    kmap = lambda b, p, qi, kj: (b * nq + kj[p], 0)
    rows = MLA_HEADS * tq
    return pl.pallas_call(
        functools.partial(_attn_prompt_kernel, tq=tq),
        out_shape=jax.ShapeDtypeStruct((T, MLA_HEADS * V_DIM), BF16),
        grid_spec=pltpu.PrefetchScalarGridSpec(
            num_scalar_prefetch=2,
            grid=(BATCH, len(pairs)),
            in_specs=[
                pl.BlockSpec((MLA_HEADS, tq, KV_RANK), qmap),
                pl.BlockSpec((MLA_HEADS, tq, ROPE_DIM), qmap),
                pl.BlockSpec((tq, KV_RANK), kmap),
                pl.BlockSpec((tq, ROPE_DIM), kmap),
                pl.BlockSpec((None, MLA_HEADS, KV_RANK, V_DIM), lambda b, p, qi, kj: (li, 0, 0, 0)),
            ],
            out_specs=pl.BlockSpec((tq, MLA_HEADS * V_DIM), lambda b, p, qi, kj: (b * nq + qi[p], 0)),
            scratch_shapes=[pltpu.VMEM((rows, LANES), F32), pltpu.VMEM((rows, LANES), F32),
                            pltpu.VMEM((rows, KV_RANK), F32)],
        ),
        compiler_params=_params("parallel", "arbitrary"),
        name="attn_prompt",
    )(q_of, k_of, qlat, qpe, latb, kpeb, w_uvT)


def _attn_sample_kernel(qlat_ref, qpe_ref, ck_ref, ckp_ref, nk_ref, nkp_ref, wuv_ref, y_ref,
                        m_ref, l_ref, acc_ref):
    j = pl.program_id(1)
    last = pl.num_programs(1) - 1

    @pl.when(j == 0)
    def _():
        _attn_init(m_ref, l_ref, acc_ref)

    def step(k, kp):
        _attn_keys_step(qlat_ref, qpe_ref, k, kp, m_ref, l_ref, acc_ref, DEC_SEQ, MLA_HEADS)

    @pl.when(j < last)
    def _():
        for c in range(0, ck_ref.shape[0], ATTN_SAMPLE_SUB_KEYS):
            keys = pl.ds(c, ATTN_SAMPLE_SUB_KEYS)
            step(ck_ref[keys, :].astype(BF16), ckp_ref[keys, :].astype(BF16))

    @pl.when(j == last)
    def _():
        step(nk_ref[...], nkp_ref[...])
        _attn_finish(wuv_ref, y_ref, l_ref, acc_ref, DEC_SEQ)


def _attn_sample(qlat, qpe, cache_lat, cache_kpe, latb, kpeb, w_uvT, li, tk=1024):
    assert PAST_LEN % CHUNK == 0 and DEC_SEQ <= CHUNK and PAST_LEN % tk == 0
    T = latb.shape[0]
    nc = PAST_LEN // tk
    qmap = lambda b, j: (0, b, 0)
    cmap = lambda b, j: (li, b, jnp.minimum(j, nc - 1), 0)
    rows = MLA_HEADS * DEC_SEQ
    return pl.pallas_call(
        _attn_sample_kernel,
        out_shape=jax.ShapeDtypeStruct((T, MLA_HEADS * V_DIM), BF16),
        grid=(DEC_BATCH, nc + 1),
        in_specs=[
            pl.BlockSpec((MLA_HEADS, DEC_SEQ, KV_RANK), qmap),
            pl.BlockSpec((MLA_HEADS, DEC_SEQ, ROPE_DIM), qmap),
            pl.BlockSpec((None, None, tk, KV_RANK), cmap),
            pl.BlockSpec((None, None, tk, ROPE_DIM), cmap),
            pl.BlockSpec((DEC_SEQ, KV_RANK), lambda b, j: (b, 0)),
            pl.BlockSpec((DEC_SEQ, ROPE_DIM), lambda b, j: (b, 0)),
            pl.BlockSpec((None, MLA_HEADS, KV_RANK, V_DIM), lambda b, j: (li, 0, 0, 0)),
        ],
        out_specs=pl.BlockSpec((DEC_SEQ, MLA_HEADS * V_DIM), lambda b, j: (b, 0)),
        scratch_shapes=[pltpu.VMEM((rows, LANES), F32), pltpu.VMEM((rows, LANES), F32),
                        pltpu.VMEM((rows, KV_RANK), F32)],
        compiler_params=_params("parallel", "arbitrary"),
        name="attn_sample",
    )(qlat, qpe, cache_lat, cache_kpe, latb, kpeb, w_uvT)


POOL_HALO = 2 * SUBLANES


def _pool_kernel(*refs, tp, from_state):
    if from_state:
        z_ref, past_ref, w_ref, scale_ref, y_ref, ext_ref = refs
        ext_ref[0:1, :] = jnp.zeros((1, POOL_WIDTH), F32)
        ext_ref[1:POOL_HALO, :] = past_ref[...]
    else:
        z_ref, w_ref, scale_ref, y_ref, ext_ref = refs
        t = pl.program_id(1)

        @pl.when(t == 0)
        def _():
            ext_ref[0:POOL_HALO, :] = jnp.zeros((POOL_HALO, POOL_WIDTH), F32)

    ext_ref[POOL_HALO:POOL_HALO + tp, :] = z_ref[...]
    for gi, w in enumerate(POOL_WINDOWS):
        lanes = slice(POOL_GROUP_DIM * gi, POOL_GROUP_DIM * (gi + 1))
        tok = ext_ref[POOL_HALO:POOL_HALO + tp, lanes]
        acc = tok
        for d in range(1, w):
            acc = acc + ext_ref[POOL_HALO - d:POOL_HALO - d + tp, lanes]
        if from_state:
            mean = acc / float(w)
        else:
            pos = t * tp + lax.broadcasted_iota(jnp.int32, (tp, 1), 0)
            mean = acc / jnp.minimum(pos + 1, w).astype(F32)
        p = (mean - tok).astype(BF16)
        y_ref[:, lanes] = (_dot(p, w_ref[gi]) * scale_ref[:, lanes]).astype(BF16)
    if not from_state:
        ext_ref[0:POOL_HALO, :] = ext_ref[tp:tp + POOL_HALO, :]


def _pool_prompt(z, w_pool, pool_scale, li, tp=512):
    T = z.shape[0]
    nt = SEQ // tp
    return pl.pallas_call(
        functools.partial(_pool_kernel, tp=tp, from_state=False),
        out_shape=jax.ShapeDtypeStruct((T, POOL_WIDTH), BF16),
        grid=(BATCH, nt),
        in_specs=[
            pl.BlockSpec((tp, POOL_WIDTH), lambda b, t: (b * nt + t, 0)),
            pl.BlockSpec((None, len(POOL_WINDOWS), POOL_GROUP_DIM, POOL_GROUP_DIM), lambda b, t: (li, 0, 0, 0)),
            pl.BlockSpec((None, 1, POOL_WIDTH), lambda b, t: (li, 0, 0)),
        ],
        out_specs=pl.BlockSpec((tp, POOL_WIDTH), lambda b, t: (b * nt + t, 0)),
        scratch_shapes=[pltpu.VMEM((POOL_HALO + tp, POOL_WIDTH), F32)],
        compiler_params=_params("arbitrary", "arbitrary"),
        name="pool_prompt",
    )(z, w_pool, pool_scale)


def _pool_sample(z, state_pool, w_pool, pool_scale, li):
    T = z.shape[0]
    return pl.pallas_call(
        functools.partial(_pool_kernel, tp=DEC_SEQ, from_state=True),
        out_shape=jax.ShapeDtypeStruct((T, POOL_WIDTH), BF16),
        grid=(DEC_BATCH,),
        in_specs=[
            pl.BlockSpec((DEC_SEQ, POOL_WIDTH), lambda b: (b, 0)),
            pl.BlockSpec((None, None, POOL_KEEP, POOL_WIDTH), lambda b: (li, b, 0, 0)),
            pl.BlockSpec((None, len(POOL_WINDOWS), POOL_GROUP_DIM, POOL_GROUP_DIM), lambda b: (li, 0, 0, 0)),
            pl.BlockSpec((None, 1, POOL_WIDTH), lambda b: (li, 0, 0)),
        ],
        out_specs=pl.BlockSpec((DEC_SEQ, POOL_WIDTH), lambda b: (b, 0)),
        scratch_shapes=[pltpu.VMEM((POOL_HALO + DEC_SEQ, POOL_WIDTH), F32)],
        compiler_params=_params("parallel"),
        name="pool_sample",
    )(z, state_pool, w_pool, pool_scale)


def _proj_res_norm_kernel(*refs, n_a, head_major, emit_x):
    a_refs = refs[:n_a]
    w_refs = refs[n_a:2 * n_a]
    x_ref, g_ref = refs[2 * n_a:2 * n_a + 2]
    outs = refs[2 * n_a + 2:]
    if head_major:
        (a_ref,), (w_ref,) = a_refs, w_refs
        xo_ref, xn_ref, cat_ref = outs
        for h in range(C_HEADS):
            cat_ref[:, HEAD_I * h:HEAD_I * (h + 1)] = a_ref[h]
        acc = _dot(cat_ref[...], w_ref[...])
    else:
        xo_ref, xn_ref = outs if emit_x else (None, outs[0])
        acc = _dot(a_refs[0][...], w_refs[0][...])
        for a_ref, w_ref in zip(a_refs[1:], w_refs[1:]):
            acc = acc + _dot(a_ref[...], w_ref[...])
    xnew = x_ref[...] + acc
    if xo_ref is not None:
        xo_ref[...] = xnew
    xn_ref[...] = _rms(xnew, g_ref[...]).astype(xn_ref.dtype)


def _proj_res_norm(a_list, w, x, g, li, head_major=False, tm=512):
    T = x.shape[0]
    row = lambda i: (i, 0)
    n_a = len(a_list)
    in_specs = []
    for a in a_list:
        if head_major:
            in_specs.append(pl.BlockSpec((C_HEADS, tm, HEAD_I), lambda i: (0, i, 0)))
        else:
            in_specs.append(pl.BlockSpec((tm, a.shape[1]), row))
    for kb, a in enumerate(a_list):
        kdim = C_HEADS * HEAD_I if head_major else a.shape[1]
        in_specs.append(pl.BlockSpec((None, kdim, D_MODEL), lambda i, kb=kb: (li, kb, 0)))
    in_specs += [pl.BlockSpec((tm, D_MODEL), row), pl.BlockSpec((1, D_MODEL), lambda i: (0, 0))]
    scratch = [pltpu.VMEM((tm, D_MODEL), BF16)] if head_major else []
    return pl.pallas_call(
        functools.partial(_proj_res_norm_kernel, n_a=n_a, head_major=head_major, emit_x=True),
        out_shape=(jax.ShapeDtypeStruct((T, D_MODEL), F32), jax.ShapeDtypeStruct((T, D_MODEL), BF16)),
        grid=(T // tm,),
        in_specs=in_specs,
        out_specs=(pl.BlockSpec((tm, D_MODEL), row), pl.BlockSpec((tm, D_MODEL), row)),
        scratch_shapes=scratch,
        compiler_params=_params("parallel"),
        name="proj_res_norm",
    )(*a_list, *([w] * n_a), x, g)


def _odd_in_kernel(xn_ref, w_ref, lbp_ref, o_ref, *, li):
    s = pl.program_id(0)
    h = _dot(xn_ref[...], w_ref[...])
    sg = jax.nn.sigmoid(h)
    p = lbp_ref[...]
    e = jnp.exp(p - jnp.max(p, axis=0, keepdims=True))
    sm = e / jnp.sum(e, axis=0, keepdims=True)
    lb = jnp.clip(jnp.sum(sm[:li + 1], axis=0, keepdims=True) - sm[0:1], 0.0, 1.0)
    f = lb + (1.0 - lb) * sg
    out = jnp.where(s == 1, f, jnp.where(s == 2, h, h * sg))
    for hd in range(C_HEADS):
        o_ref[hd] = out[:, HEAD_I * hd:HEAD_I * (hd + 1)]


def _odd_in(xn, w_in_c, lb_param, li, tm=1024):
    T = xn.shape[0]
    return pl.pallas_call(
        functools.partial(_odd_in_kernel, li=li),
        out_shape=jax.ShapeDtypeStruct((4, C_HEADS, T, HEAD_I), F32),
        grid=(4, T // tm),
        in_specs=[
            pl.BlockSpec((tm, D_MODEL), lambda s, i: (i, 0)),
            pl.BlockSpec((None, D_MODEL, D_MODEL), lambda s, i: (li, 0, s)),
            pl.BlockSpec(lb_param.shape, lambda s, i: (0, 0)),
        ],
        out_specs=pl.BlockSpec((None, C_HEADS, tm, HEAD_I), lambda s, i: (s, 0, i, 0)),
        compiler_params=_params("parallel", "parallel"),
        name="odd_in",
    )(xn, w_in_c, lb_param)


def _gla_kernel(*refs, has_s0):
    if has_s0:
        q_ref, f_ref, v_ref, gs_ref, gon_ref, s0_ref, og_ref, sout_ref, st_ref = refs
    else:
        q_ref, f_ref, v_ref, gs_ref, gon_ref, og_ref, sout_ref, st_ref = refs
    C = GLA_CHUNK
    n = pl.program_id(1)
    last = pl.num_programs(1) - 1

    @pl.when(n == 0)
    def _():
        if has_s0:
            st_ref[...] = s0_ref[...]
        else:
            st_ref[...] = jnp.zeros(st_ref.shape, F32)

    r2 = lax.broadcasted_iota(jnp.int32, (C, C), 0)
    c2 = lax.broadcasted_iota(jnp.int32, (C, C), 1)
    tri = (c2 <= r2).astype(BF16)
    ones = jnp.ones((HEAD_F, LANES), BF16)
    gon = gon_ref[...]
    levels = []
    m = GLA_SUB
    while m < C:
        keep = (r2 // (2 * m) == c2 // (2 * m)) & (r2 % (2 * m) >= m) & (c2 % (2 * m) < m)
        levels.append((m, keep))
        m *= 2
    near = [(c2 == r2 - delta) & (r2 % GLA_SUB >= delta) for delta in range(GLA_SUB)]

    def bcast_row(a, period, r):
        a3 = a.reshape(C // period, period, a.shape[-1])
        return jnp.broadcast_to(a3[:, r:r + 1, :], a3.shape).reshape(C, a.shape[-1])

    def shift_in_group(a, delta):
        if delta == 0:
            return a
        a3 = a.reshape(C // GLA_SUB, GLA_SUB, a.shape[-1])
        return pltpu.roll(a3, delta, 1).reshape(C, a.shape[-1])

    def split3(g):
        g1 = g.astype(BF16)
        r1 = g - g1.astype(F32)
        g2 = r1.astype(BF16)
        return g1, g2, (r1 - g2.astype(F32)).astype(BF16)

    def head_group(hg, carry):
        hs = [hg * GLA_HEADS_PER_ITER + u for u in range(GLA_HEADS_PER_ITER)]
        U = range(len(hs))
        q = [q_ref[h] for h in hs]
        f = [f_ref[h] for h in hs]
        v_bf = [v_ref[h].astype(BF16) for h in hs]
        fc = [jnp.maximum(x, F_MIN) for x in f]
        kk = [1.0 - x for x in f]
        gp = [split3(jnp.log2(x)) for x in fc]
        b = [_dot(tri, gp[u][0]) + _dot(tri, gp[u][1]) + _dot(tri, gp[u][2]) for u in U]
        st = [st_ref[h] for h in hs]
        o = [_dot((q[u] * jnp.exp2(b[u])).astype(BF16), st[u].astype(BF16)) for u in U]
        b_last = [x[C - 1:C, :] for x in b]
        k_dec = [(kk[u] * jnp.exp2(b_last[u] - b[u])).astype(BF16) for u in U]
        for u in U:
            d_rows = jnp.broadcast_to(jnp.exp2(b_last[u]), (LANES, HEAD_F)).T
            st_ref[hs[u]] = st[u] * d_rows + _dot_tn(k_dec[u], v_bf[u])

        a = [jnp.zeros((C, C), F32) for _ in U]
        for m, keep in levels:
            for u in U:
                d = b[u] - bcast_row(b[u], 2 * m, m - 1)
                qt = (q[u] * jnp.exp2(jnp.minimum(d, 0.0))).astype(BF16)
                kt = (kk[u] * jnp.exp2(jnp.minimum(-d, 0.0))).astype(BF16)
                a[u] = jnp.where(keep, _dot_nt(qt, kt), a[u])
        kd = list(kk)
        for delta in range(GLA_SUB):
            for u in U:
                if delta > 0:
                    kd[u] = shift_in_group(kd[u], 1) * fc[u]
                p = (q[u] * kd[u]).astype(BF16)
                a[u] = jnp.where(near[delta], _dot(p, ones)[:, :C], a[u])
        for u in U:
            y = _rms(o[u] + _dot(a[u].astype(BF16), v_bf[u]), gon) * gs_ref[hs[u]]
            og_ref[hs[u]] = y.astype(BF16)
        return carry

    lax.fori_loop(0, C_HEADS // GLA_HEADS_PER_ITER, head_group, 0)

    @pl.when(n == last)
    def _():
        sout_ref[...] = st_ref[...]


def _gla(qfvg, g_onorm, s0, li, n_streams, seq):
    T = qfvg.shape[2]
    C = GLA_CHUNK
    nc = seq // C
    has_s0 = s0 is not None
    sec = lambda k: pl.BlockSpec((None, C_HEADS, C, HEAD_F), lambda b, n, k=k: (k, 0, b * nc + n, 0))
    in_specs = [sec(0), sec(1), sec(2), sec(3),
                pl.BlockSpec((None, 1, HEAD_I), lambda b, n: (li, 0, 0))]
    args = [qfvg, qfvg, qfvg, qfvg, g_onorm]
    if has_s0:
        in_specs.append(pl.BlockSpec((None, None, C_HEADS, HEAD_F, HEAD_I), lambda b, n: (li, b, 0, 0, 0)))
        args.append(s0)
    return pl.pallas_call(
        functools.partial(_gla_kernel, has_s0=has_s0),
        out_shape=(jax.ShapeDtypeStruct((C_HEADS, T, HEAD_I), BF16),
                   jax.ShapeDtypeStruct((n_streams, C_HEADS, HEAD_F, HEAD_I), F32)),
        grid=(n_streams, nc),
        in_specs=in_specs,
        out_specs=(pl.BlockSpec((C_HEADS, C, HEAD_I), lambda b, n: (0, b * nc + n, 0)),
                   pl.BlockSpec((None, C_HEADS, HEAD_F, HEAD_I), lambda b, n: (b, 0, 0, 0))),
        scratch_shapes=[pltpu.VMEM((C_HEADS, HEAD_I, HEAD_F), F32)],
        compiler_params=_params("parallel", "arbitrary"),
        name="gla",
    )(*args)


FFN_PAD = SUBLANES
FFN_SUB_ROWS = 128


def _ffn_up_kernel(*refs, tm, seg, from_state, tiles_per_stream):
    if from_state:
        (xn_ref, wa_ref, wb_ref, cwa_ref, cwb_ref, cba_ref, cbb_ref, pa_ref, pb_ref,
         act_ref, hla_ref, hlb_ref) = refs
        carry_ref = None
    else:
        (xn_ref, wa_ref, wb_ref, cwa_ref, cwb_ref, cba_ref, cbb_ref,
         act_ref, hla_ref, hlb_ref, carry_ref) = refs
        pa_ref = pb_ref = None
    i = pl.program_id(0)
    j = pl.program_id(1)
    sub = min(FFN_SUB_ROWS, tm)
    assert seg % sub == 0 or sub % seg == 0
    halves = ((0, wa_ref, cwa_ref, cba_ref, pa_ref, hla_ref), (1, wb_ref, cwb_ref, cbb_ref, pb_ref, hlb_ref))

    def conv(front, h, cw, cb):
        n = h.shape[0]
        ext = jnp.concatenate([front, h], axis=0)
        return cb + (ext[FFN_PAD - 2:FFN_PAD - 2 + n] * cw[0:1] + ext[FFN_PAD - 1:FFN_PAD - 1 + n] * cw[1:2] +
                     h * cw[2:3])

    fronts = [None, None]
    if not from_state:
        @pl.when(i % tiles_per_stream == 0)
        def _():
            carry_ref[j] = jnp.zeros((2, FFN_PAD, FFN_TN), F32)
        fronts = [carry_ref[j, 0], carry_ref[j, 1]]

    for r0 in range(0, tm, sub):
        convs = []
        for hidx, w_ref, cw_ref, cb_ref, past_ref, hl_ref in halves:
            h = _dot(xn_ref[r0:r0 + sub, :], w_ref[...])
            cw = cw_ref[...]
            cb = cb_ref[...]
            if seg >= sub:
                convs.append([conv(fronts[hidx], h, cw, cb)])
                fronts[hidx] = h[sub - FFN_PAD:sub]
                if (r0 + sub) % seg == 0:
                    hl_ref[(r0 + sub) // seg - 1] = h[sub - 2:sub]
            else:
                pieces = []
                for k in range(sub // seg):
                    s = r0 // seg + k
                    hk = h[k * seg:(k + 1) * seg]
                    front = jnp.concatenate([jnp.zeros((FFN_PAD - 2, FFN_TN), F32), past_ref[s]], axis=0)
                    pieces.append(conv(front, hk, cw, cb))
                    hl_ref[s] = hk[seg - 2:seg]
                convs.append(pieces)
        for n, (a, b) in enumerate(zip(*convs)):
            rows = a.shape[0]
            o = r0 + n * rows
            half = 0.5 * a
            act_ref[o:o + rows, :] = ((half + half * jnp.tanh(half)) * b).astype(BF16)

    if not from_state:
        carry_ref[j, 0] = fronts[0]
        carry_ref[j, 1] = fronts[1]


def _ffn_up(xn, w_up, conv_w, conv_b, conv_past, layer, tm, seg):
    T = xn.shape[0]
    tn = FFN_TN
    nj = D_FF // tn
    nseg = tm // seg
    from_state = conv_past is not None
    wmap_a = lambda i, j: (layer, 0, j)
    wmap_b = lambda i, j: (layer, 0, nj + j)
    in_specs = [
        pl.BlockSpec((tm, D_MODEL), lambda i, j: (i, 0)),
        pl.BlockSpec((None, D_MODEL, tn), wmap_a), pl.BlockSpec((None, D_MODEL, tn), wmap_b),
        pl.BlockSpec((None, CONV_W, tn), wmap_a), pl.BlockSpec((None, CONV_W, tn), wmap_b),
        pl.BlockSpec((None, 1, tn), wmap_a), pl.BlockSpec((None, 1, tn), wmap_b),
    ]
    args = [xn, w_up, w_up, conv_w, conv_w, conv_b, conv_b]
    scratch = []
    if from_state:
        in_specs += [pl.BlockSpec((None, nseg, CONV_W - 1, tn), lambda i, j: (layer, i, 0, j)),
                     pl.BlockSpec((None, nseg, CONV_W - 1, tn), lambda i, j: (layer, i, 0, nj + j))]
        args += [conv_past, conv_past]
        tiles_per_stream = 1
    else:
        scratch.append(pltpu.VMEM((nj, 2, FFN_PAD, tn), F32))
        tiles_per_stream = SEQ // tm
    hl_shape = jax.ShapeDtypeStruct((T // seg, CONV_W - 1, D_FF), F32)
    hl_spec = pl.BlockSpec((nseg, CONV_W - 1, tn), lambda i, j: (i, 0, j))
    return pl.pallas_call(
        functools.partial(_ffn_up_kernel, tm=tm, seg=seg, from_state=from_state,
                          tiles_per_stream=tiles_per_stream),
        out_shape=(jax.ShapeDtypeStruct((T, D_FF), BF16), hl_shape, hl_shape),
        grid=(T // tm, nj),
        in_specs=in_specs,
        out_specs=(pl.BlockSpec((tm, tn), lambda i, j: (i, j)), hl_spec, hl_spec),
        scratch_shapes=scratch,
        compiler_params=_params("arbitrary", "arbitrary"),
        name="ffn_up",
    )(*args)


def _ffn_down_kernel(a_ref, w_ref, x_ref, g_ref, *outs, emit_x):
    if emit_x:
        acc_ref, xn_ref = outs
    else:
        xn_ref, acc_ref = outs
    k = pl.program_id(1)

    @pl.when(k == 0)
    def _():
        acc_ref[...] = x_ref[...]

    acc_ref[...] += _dot(a_ref[...], w_ref[...])

    @pl.when(k == pl.num_programs(1) - 1)
    def _():
        xn_ref[...] = _rms(acc_ref[...], g_ref[...]).astype(xn_ref.dtype)


def _ffn_down(act, w_down, x, g, layer, emit_x, xn_dtype, tm=1024, nk=4):
    T = x.shape[0]
    tk = D_FF // nk
    row = lambda i, k: (i, 0)
    out_shape = [jax.ShapeDtypeStruct((T, D_MODEL), xn_dtype)]
    out_specs = [pl.BlockSpec((tm, D_MODEL), row)]
    scratch = [pltpu.VMEM((tm, D_MODEL), F32)]
    if emit_x:
        out_shape.insert(0, jax.ShapeDtypeStruct((T, D_MODEL), F32))
        out_specs.insert(0, pl.BlockSpec((tm, D_MODEL), row))
        scratch = []
    res = pl.pallas_call(
        functools.partial(_ffn_down_kernel, emit_x=emit_x),
        out_shape=tuple(out_shape),
        grid=(T // tm, nk),
        in_specs=[
            pl.BlockSpec((tm, tk), lambda i, k: (i, k)),
            pl.BlockSpec((None, tk, D_MODEL), lambda i, k: (layer, k, 0)),
            pl.BlockSpec((tm, D_MODEL), row, pipeline_mode=pl.Buffered(1)),
            pl.BlockSpec((1, D_MODEL), lambda i, k: (0, 0)),
        ],
        out_specs=tuple(out_specs),
        scratch_shapes=scratch,
        compiler_params=_params("parallel", "arbitrary"),
        name="ffn_down",
    )(act, w_down, x, g)
    return res if emit_x else (None, res[0])


def _rope_tables(pos):
    inv = ROPE_THETA ** (-jnp.arange(0, ROPE_DIM, 2, dtype=F32) / ROPE_DIM)
    ang = pos.astype(F32)[:, None] * inv[None, :]
    cos, sin = jnp.cos(ang), jnp.sin(ang)
    reps = LANES // ROPE_DIM
    return (jnp.tile(jnp.concatenate([cos, cos], axis=-1), (1, reps)),
            jnp.tile(jnp.concatenate([-sin, sin], axis=-1), (1, reps)))


def kernel(x_prompt, x_sample, cache_mla_latent, cache_mla_krope, state_pool, state_hgrn, state_ffn_conv,
           g_mix, g_ffn, g_final, w_in_a, g_qa, w_qb, g_kva, w_uk, w_uv, w_pool, pool_scale, w_out_a,
           w_in_c, lb_param, g_onorm, w_out_c, w_up, conv_w, conv_b, w_down):
    n_even = w_in_a.shape[0]
    n_odd = w_in_c.shape[0]
    Tp, Ts = BATCH * SEQ, DEC_BATCH * DEC_SEQ

    o1, o2, o3 = Q_RANK, Q_RANK + KV_RANK, Q_RANK + KV_RANK + ROPE_DIM
    w_in_e = jnp.concatenate(
        [w_in_a[..., :o2], w_in_a[..., o3:], w_in_a[..., o2:o3],
         jnp.zeros((n_even, D_MODEL, LANES - ROPE_DIM), w_in_a.dtype)], axis=-1).astype(BF16)
    w_qb4 = w_qb.reshape(n_even, Q_RANK, MLA_HEADS, NOPE_DIM + ROPE_DIM)
    w_qb_e = jnp.concatenate(
        [w_qb4[..., :NOPE_DIM].reshape(n_even, Q_RANK, Q_NOPE_ALL),
         w_qb4[..., NOPE_DIM:].reshape(n_even, Q_RANK, MLA_HEADS * ROPE_DIM)], axis=-1).astype(BF16)
    w_ukT = jnp.transpose(w_uk, (0, 2, 3, 1)).astype(BF16)
    w_uvT = jnp.transpose(w_uv, (0, 2, 1, 3)).astype(BF16)
    w_pool_b = w_pool.astype(BF16)
    w_out_a_b = w_out_a.astype(BF16)
    w_in_c_b = w_in_c.astype(BF16)
    w_out_c_b = w_out_c.astype(BF16)
    w_up_b = w_up.astype(BF16)
    w_down_b = w_down.astype(BF16)
    g_qa3 = g_qa[:, None, :]
    g_kva3 = g_kva[:, None, :]
    pool_scale3 = pool_scale[:, None, :]
    g_onorm3 = g_onorm[:, None, :]
    conv_b3 = conv_b[:, None, :]

    cos_p, sin_p = _rope_tables(jnp.tile(jnp.arange(SEQ), BATCH))
    cos_s, sin_s = _rope_tables(jnp.tile(PAST_LEN + jnp.arange(DEC_SEQ), DEC_BATCH))

    xp = x_prompt.reshape(Tp, D_MODEL)
    xs = x_sample.reshape(Ts, D_MODEL)
    xnp = _norm(xp, g_mix[0:1])
    xns = _norm(xs, g_mix[0:1])

    lat_p, kpe_p, pool_p, hg_p, cv_p = [], [], [], [], []
    lat_s, kpe_s, pool_s, hg_s, cv_s = [], [], [], [], []
    for layer in range(DEPTH):
        li = layer // 2
        g_next = g_ffn[layer:layer + 1]
        if layer % 2 == 0:
            wa = (w_in_e, g_qa3, g_kva3, w_qb_e, w_ukT)
            lat, latb, kpe, kpeb, z, qlat, qpe = _even_in(xnp, *wa, cos_p, sin_p, li)
            y_mla = _attn_prompt(qlat, qpe, latb, kpeb, w_uvT, li)
            y_pool = _pool_prompt(z, w_pool_b, pool_scale3, li)
            xp, xnp = _proj_res_norm([y_mla, y_pool], w_out_a_b, xp, g_next, li)
            lat_p.append(lat.reshape(BATCH, SEQ, KV_RANK))
            kpe_p.append(kpe.reshape(BATCH, SEQ, ROPE_DIM))
            pool_p.append(z.reshape(BATCH, SEQ, POOL_WIDTH)[:, SEQ - POOL_KEEP:])

            lat, latb, kpe, kpeb, z, qlat, qpe = _even_in(xns, *wa, cos_s, sin_s, li)
            y_mla = _attn_sample(qlat, qpe, cache_mla_latent, cache_mla_krope, latb, kpeb, w_uvT, li)
            y_pool = _pool_sample(z, state_pool, w_pool_b, pool_scale3, li)
            xs, xns = _proj_res_norm([y_mla, y_pool], w_out_a_b, xs, g_next, li)
            lat_s.append(lat.reshape(DEC_BATCH, DEC_SEQ, KV_RANK))
            kpe_s.append(kpe.reshape(DEC_BATCH, DEC_SEQ, ROPE_DIM))
            pool_s.append(z.reshape(DEC_BATCH, DEC_SEQ, POOL_WIDTH)[:, DEC_SEQ - POOL_KEEP:])
        else:
            qfvg = _odd_in(xnp, w_in_c_b, lb_param, li)
            og, s_fin = _gla(qfvg, g_onorm3, None, li, BATCH, SEQ)
            xp, xnp = _proj_res_norm([og], w_out_c_b, xp, g_next, li, head_major=True)
            hg_p.append(s_fin)

            qfvg = _odd_in(xns, w_in_c_b, lb_param, li)
            og, s_fin = _gla(qfvg, g_onorm3, state_hgrn, li, DEC_BATCH, DEC_SEQ)
            xs, xns = _proj_res_norm([og], w_out_c_b, xs, g_next, li, head_major=True)
            hg_s.append(s_fin)

        final = layer == DEPTH - 1
        g_after = g_final[None, :] if final else g_mix[layer + 1:layer + 2]
        xn_dtype = F32 if final else BF16

        tm_p = 1024
        act, hla, hlb = _ffn_up(xnp, w_up_b, conv_w, conv_b3, None, layer, tm=tm_p, seg=tm_p)
        xp, xnp = _ffn_down(act, w_down_b, xp, g_after, layer, not final, xn_dtype)
        hl = jnp.concatenate([hla, hlb], axis=-1).reshape(BATCH, SEQ // tm_p, CONV_W - 1, 2 * D_FF)
        cv_p.append(hl[:, -1])

        act, hla, hlb = _ffn_up(xns, w_up_b, conv_w, conv_b3, state_ffn_conv, layer, tm=512, seg=DEC_SEQ)
        xs, xns = _ffn_down(act, w_down_b, xs, g_after, layer, not final, xn_dtype)
        cv_s.append(jnp.concatenate([hla, hlb], axis=-1))

    return (xnp.reshape(BATCH, SEQ, D_MODEL), xns.reshape(DEC_BATCH, DEC_SEQ, D_MODEL),
            jnp.stack(lat_p), jnp.stack(kpe_p), jnp.stack(pool_p), jnp.stack(hg_p), jnp.stack(cv_p),
            jnp.stack(lat_s), jnp.stack(kpe_s), jnp.stack(pool_s), jnp.stack(hg_s), jnp.stack(cv_s))
```

```python
import functools

import jax
import jax.numpy as jnp
from jax import lax
from jax.experimental import pallas as pl
from jax.experimental.pallas import tpu as pltpu

D_MODEL = 2048
BATCH = 8
SEQ = 2048
DEPTH = 4
DEC_BATCH = 32
DEC_SEQ = 64
PAST_LEN = 4096
CHUNK = 64
MLA_HEADS = 8
Q_RANK = 512
KV_RANK = 512
NOPE_DIM = 128
ROPE_DIM = 64
V_DIM = 128
ROPE_THETA = 10000.0
MLA_SCALE = (NOPE_DIM + ROPE_DIM) ** -0.5
MASK_NEG = -1e30
POOL_WINDOWS = (2, 4, 8, 16)
POOL_WIDTH = D_MODEL - MLA_HEADS * V_DIM
POOL_GROUP_DIM = POOL_WIDTH // len(POOL_WINDOWS)
POOL_KEEP = max(POOL_WINDOWS) - 1
C_HEADS = 16
HEAD_F = 128
HEAD_I = D_MODEL // C_HEADS
F_MIN = 1e-30
D_FF = 5632
CONV_W = 3
EPS = 1e-6

F32 = jnp.float32
BF16 = jnp.bfloat16

LANES = 128
SUBLANES = 8
VMEM_LIMIT = 56 * 1024 * 1024

IN_EVEN_PAD = Q_RANK + KV_RANK + POOL_WIDTH + LANES
Q_NOPE_ALL = MLA_HEADS * NOPE_DIM
GLA_CHUNK = 64
GLA_SUB = SUBLANES
GLA_HEADS_PER_ITER = 16
GLA_STEP_ROWS = 256
FFN_TN = 512


def _params(*sem):
    return pltpu.CompilerParams(dimension_semantics=sem, vmem_limit_bytes=VMEM_LIMIT)


def _dot(a, b):
    return jnp.dot(a, b, preferred_element_type=F32)


def _dot_nt(a, b):
    return lax.dot_general(a, b, (((1,), (1,)), ((), ())), preferred_element_type=F32)


def _dot_tn(a, b):
    return lax.dot_general(a, b, (((0,), (0,)), ((), ())), preferred_element_type=F32)


def _rms(x, g):
    return x * lax.rsqrt(jnp.mean(x * x, axis=-1, keepdims=True) + EPS) * g


def _norm_kernel(x_ref, g_ref, xn_ref):
    xn_ref[...] = _rms(x_ref[...], g_ref[...]).astype(xn_ref.dtype)


def _norm(x, g, tm=512):
    T = x.shape[0]
    return pl.pallas_call(
        _norm_kernel,
        out_shape=jax.ShapeDtypeStruct((T, D_MODEL), BF16),
        grid=(T // tm,),
        in_specs=[pl.BlockSpec((tm, D_MODEL), lambda i: (i, 0)),
                  pl.BlockSpec((1, D_MODEL), lambda i: (0, 0))],
        out_specs=pl.BlockSpec((tm, D_MODEL), lambda i: (i, 0)),
        compiler_params=_params("parallel"),
        name="norm0",
    )(x, g)


def _rope(s, cos, sin, first_half):
    swapped = jnp.where(first_half, pltpu.roll(s, 96, 1), pltpu.roll(s, 32, 1))
    return s * cos + swapped * sin


def _even_in_kernel(xn_ref, w_ref, gqa_ref, gkva_ref, wqb_ref, wuk_ref, cos_ref, sin_ref,
                    lat_ref, latb_ref, kpe_ref, kpeb_ref, z_ref, qlat_ref, qpe_ref):
    tm = xn_ref.shape[0]
    acc = _dot(xn_ref[...], w_ref[...])
    o1, o2, o3 = Q_RANK, Q_RANK + KV_RANK, Q_RANK + KV_RANK + POOL_WIDTH
    z_ref[...] = acc[:, o2:o3]
    lat = _rms(acc[:, o1:o2], gkva_ref[...])
    lat_ref[...] = lat
    latb_ref[...] = lat.astype(BF16)
    cos = cos_ref[...]
    sin = sin_ref[...]
    lane = lax.broadcasted_iota(jnp.int32, (tm, LANES), 1)
    first_half = (lane % ROPE_DIM) < (ROPE_DIM // 2)
    kr = _rope(acc[:, o3:o3 + LANES], cos, sin, first_half)
    kpe_ref[...] = kr[:, :ROPE_DIM]
    kpeb_ref[...] = kr[:, :ROPE_DIM].astype(BF16)
    cqn = _rms(acc[:, :o1], gqa_ref[...]).astype(BF16)
    q = _dot(cqn, wqb_ref[...])
    for s in range(MLA_HEADS // 2):
        lo = Q_NOPE_ALL + LANES * s
        r = (_rope(q[:, lo:lo + LANES], cos, sin, first_half) * MLA_SCALE).astype(BF16)
        qpe_ref[2 * s] = r[:, :ROPE_DIM]
        qpe_ref[2 * s + 1] = r[:, ROPE_DIM:]
    for h in range(MLA_HEADS):
        ql = _dot(q[:, NOPE_DIM * h:NOPE_DIM * (h + 1)].astype(BF16), wuk_ref[h])
        qlat_ref[h] = (ql * MLA_SCALE).astype(BF16)


def _even_in(xn, w_in, g_qa, g_kva, w_qb, w_ukT, cos, sin, li, tm=256):
    T = xn.shape[0]
    row = lambda i: (i, 0)
    fix2 = lambda i: (0, 0)
    out_shape = (
        jax.ShapeDtypeStruct((T, KV_RANK), F32), jax.ShapeDtypeStruct((T, KV_RANK), BF16),
        jax.ShapeDtypeStruct((T, ROPE_DIM), F32), jax.ShapeDtypeStruct((T, ROPE_DIM), BF16),
        jax.ShapeDtypeStruct((T, POOL_WIDTH), F32),
        jax.ShapeDtypeStruct((MLA_HEADS, T, KV_RANK), BF16),
        jax.ShapeDtypeStruct((MLA_HEADS, T, ROPE_DIM), BF16),
    )
    return pl.pallas_call(
        _even_in_kernel,
        out_shape=out_shape,
        grid=(T // tm,),
        in_specs=[
            pl.BlockSpec((tm, D_MODEL), row),
            pl.BlockSpec((None, D_MODEL, IN_EVEN_PAD), lambda i: (li, 0, 0)),
            pl.BlockSpec((None, 1, Q_RANK), lambda i: (li, 0, 0)),
            pl.BlockSpec((None, 1, KV_RANK), lambda i: (li, 0, 0)),
            pl.BlockSpec((None, Q_RANK, MLA_HEADS * (NOPE_DIM + ROPE_DIM)), lambda i: (li, 0, 0)),
            pl.BlockSpec((None, MLA_HEADS, NOPE_DIM, KV_RANK), lambda i: (li, 0, 0, 0)),
            pl.BlockSpec((tm, LANES), row),
            pl.BlockSpec((tm, LANES), row),
        ],
        out_specs=(
            pl.BlockSpec((tm, KV_RANK), row), pl.BlockSpec((tm, KV_RANK), row),
            pl.BlockSpec((tm, ROPE_DIM), row), pl.BlockSpec((tm, ROPE_DIM), row),
            pl.BlockSpec((tm, POOL_WIDTH), row),
            pl.BlockSpec((MLA_HEADS, tm, KV_RANK), lambda i: (0, i, 0)),
            pl.BlockSpec((MLA_HEADS, tm, ROPE_DIM), lambda i: (0, i, 0)),
        ),
        compiler_params=_params("parallel"),
        name="even_in",
    )(xn, w_in, g_qa, g_kva, w_qb, w_ukT, cos, sin)


ATTN_HEAD_GROUP = 4
ATTN_SAMPLE_SUB_KEYS = 512


def _lane_tile(x, width):
    if width <= LANES:
        return x[:, :width]
    return jnp.concatenate([x] * (width // LANES), axis=1)


def _attn_keys_step(qlat_ref, qpe_ref, k, kp, m_ref, l_ref, acc_ref, tq, group, visible=None):
    tk = k.shape[0]
    n = group * tq

    def scores(g):
        ql = qlat_ref[g * group:(g + 1) * group].reshape(n, KV_RANK)
        qp = qpe_ref[g * group:(g + 1) * group].reshape(n, ROPE_DIM)
        s = _dot_nt(ql, k) + _dot_nt(qp, kp)
        return s if visible is None else jnp.where(visible, s, MASK_NEG)

    s_next = scores(0)
    for g in range(MLA_HEADS // group):
        s = s_next
        if g + 1 < MLA_HEADS // group:
            s_next = scores(g + 1)
        rows = pl.ds(g * n, n)
        m_prev = m_ref[rows, :]
        m_new = jnp.maximum(m_prev, jnp.max(s, axis=1, keepdims=True))
        alpha = jnp.exp(m_prev - m_new)
        p = jnp.exp(s - _lane_tile(m_new, tk))
        l_ref[rows, :] = alpha * l_ref[rows, :] + jnp.sum(p, axis=1, keepdims=True)
        acc_ref[rows, :] = _lane_tile(alpha, KV_RANK) * acc_ref[rows, :] + _dot(p.astype(BF16), k)
        m_ref[rows, :] = m_new


def _attn_init(m_ref, l_ref, acc_ref):
    m_ref[...] = jnp.full(m_ref.shape, MASK_NEG, F32)
    l_ref[...] = jnp.zeros(l_ref.shape, F32)
    acc_ref[...] = jnp.zeros(acc_ref.shape, F32)


def _attn_finish(wuv_ref, y_ref, l_ref, acc_ref, tq):
    for h in range(MLA_HEADS):
        rows = pl.ds(h * tq, tq)
        o = (acc_ref[rows, :] / _lane_tile(l_ref[rows, :], KV_RANK)).astype(BF16)
        y_ref[:, V_DIM * h:V_DIM * (h + 1)] = _dot(o, wuv_ref[h]).astype(BF16)


def _attn_prompt_kernel(qi_ref, kj_ref, qlat_ref, qpe_ref, k_ref, kp_ref, wuv_ref, y_ref,
                        m_ref, l_ref, acc_ref, *, tq):
    p = pl.program_id(1)
    i = qi_ref[p]
    j = kj_ref[p]

    @pl.when(j == 0)
    def _():
        _attn_init(m_ref, l_ref, acc_ref)

    @pl.when(j < i)
    def _():
        _attn_keys_step(qlat_ref, qpe_ref, k_ref[...], kp_ref[...], m_ref, l_ref, acc_ref, tq,
                        ATTN_HEAD_GROUP)

    @pl.when(j == i)
    def _():
        n = ATTN_HEAD_GROUP * tq
        row = lax.broadcasted_iota(jnp.int32, (n, tq), 0) % tq
        col = lax.broadcasted_iota(jnp.int32, (n, tq), 1)
        visible = col // CHUNK <= row // CHUNK
        _attn_keys_step(qlat_ref, qpe_ref, k_ref[...], kp_ref[...], m_ref, l_ref, acc_ref, tq,
                        ATTN_HEAD_GROUP, visible)
        _attn_finish(wuv_ref, y_ref, l_ref, acc_ref, tq)


def _attn_prompt(qlat, qpe, latb, kpeb, w_uvT, li, tq=256):
    assert tq % CHUNK == 0
    T = latb.shape[0]
    nq = SEQ // tq
    pairs = [(i, j) for i in range(nq) for j in range(i + 1)]
    q_of = jnp.asarray([i for i, _ in pairs], jnp.int32)
    k_of = jnp.asarray([j for _, j in pairs], jnp.int32)
    qmap = lambda b, p, qi, kj: (0, b * nq + qi[p], 0)
    kmap = lambda b, p, qi, kj: (b * nq + kj[p], 0)
    rows = MLA_HEADS * tq
    return pl.pallas_call(
        functools.partial(_attn_prompt_kernel, tq=tq),
        out_shape=jax.ShapeDtypeStruct((T, MLA_HEADS * V_DIM), BF16),
        grid_spec=pltpu.PrefetchScalarGridSpec(
            num_scalar_prefetch=2,
            grid=(BATCH, len(pairs)),
            in_specs=[
                pl.BlockSpec((MLA_HEADS, tq, KV_RANK), qmap),
                pl.BlockSpec((MLA_HEADS, tq, ROPE_DIM), qmap),
                pl.BlockSpec((tq, KV_RANK), kmap),
                pl.BlockSpec((tq, ROPE_DIM), kmap),
                pl.BlockSpec((None, MLA_HEADS, KV_RANK, V_DIM), lambda b, p, qi, kj: (li, 0, 0, 0)),
            ],
            out_specs=pl.BlockSpec((tq, MLA_HEADS * V_DIM), lambda b, p, qi, kj: (b * nq + qi[p], 0)),
            scratch_shapes=[pltpu.VMEM((rows, LANES), F32), pltpu.VMEM((rows, LANES), F32),
                            pltpu.VMEM((rows, KV_RANK), F32)],
        ),
        compiler_params=_params("parallel", "arbitrary"),
        name="attn_prompt",
    )(q_of, k_of, qlat, qpe, latb, kpeb, w_uvT)


def _attn_sample_kernel(qlat_ref, qpe_ref, ck_ref, ckp_ref, nk_ref, nkp_ref, wuv_ref, y_ref,
                        m_ref, l_ref, acc_ref):
    j = pl.program_id(1)
    last = pl.num_programs(1) - 1

    @pl.when(j == 0)
    def _():
        _attn_init(m_ref, l_ref, acc_ref)

    def step(k, kp):
        _attn_keys_step(qlat_ref, qpe_ref, k, kp, m_ref, l_ref, acc_ref, DEC_SEQ, MLA_HEADS)

    @pl.when(j < last)
    def _():
        for c in range(0, ck_ref.shape[0], ATTN_SAMPLE_SUB_KEYS):
            keys = pl.ds(c, ATTN_SAMPLE_SUB_KEYS)
            step(ck_ref[keys, :].astype(BF16), ckp_ref[keys, :].astype(BF16))

    @pl.when(j == last)
    def _():
        step(nk_ref[...], nkp_ref[...])
        _attn_finish(wuv_ref, y_ref, l_ref, acc_ref, DEC_SEQ)


def _attn_sample(qlat, qpe, cache_lat, cache_kpe, latb, kpeb, w_uvT, li, tk=1024):
    assert PAST_LEN % CHUNK == 0 and DEC_SEQ <= CHUNK and PAST_LEN % tk == 0
    T = latb.shape[0]
    nc = PAST_LEN // tk
    qmap = lambda b, j: (0, b, 0)
    cmap = lambda b, j: (li, b, jnp.minimum(j, nc - 1), 0)
    rows = MLA_HEADS * DEC_SEQ
    return pl.pallas_call(
        _attn_sample_kernel,
        out_shape=jax.ShapeDtypeStruct((T, MLA_HEADS * V_DIM), BF16),
        grid=(DEC_BATCH, nc + 1),
        in_specs=[
            pl.BlockSpec((MLA_HEADS, DEC_SEQ, KV_RANK), qmap),
            pl.BlockSpec((MLA_HEADS, DEC_SEQ, ROPE_DIM), qmap),
            pl.BlockSpec((None, None, tk, KV_RANK), cmap),
            pl.BlockSpec((None, None, tk, ROPE_DIM), cmap),
            pl.BlockSpec((DEC_SEQ, KV_RANK), lambda b, j: (b, 0)),
            pl.BlockSpec((DEC_SEQ, ROPE_DIM), lambda b, j: (b, 0)),
            pl.BlockSpec((None, MLA_HEADS, KV_RANK, V_DIM), lambda b, j: (li, 0, 0, 0)),
        ],
        out_specs=pl.BlockSpec((DEC_SEQ, MLA_HEADS * V_DIM), lambda b, j: (b, 0)),
        scratch_shapes=[pltpu.VMEM((rows, LANES), F32), pltpu.VMEM((rows, LANES), F32),
                        pltpu.VMEM((rows, KV_RANK), F32)],
        compiler_params=_params("parallel", "arbitrary"),
        name="attn_sample",
    )(qlat, qpe, cache_lat, cache_kpe, latb, kpeb, w_uvT)


POOL_HALO = 2 * SUBLANES


def _pool_kernel(*refs, tp, from_state):
    if from_state:
        z_ref, past_ref, w_ref, scale_ref, y_ref, ext_ref = refs
        ext_ref[0:1, :] = jnp.zeros((1, POOL_WIDTH), F32)
        ext_ref[1:POOL_HALO, :] = past_ref[...]
    else:
        z_ref, w_ref, scale_ref, y_ref, ext_ref = refs
        t = pl.program_id(1)

        @pl.when(t == 0)
        def _():
            ext_ref[0:POOL_HALO, :] = jnp.zeros((POOL_HALO, POOL_WIDTH), F32)

    ext_ref[POOL_HALO:POOL_HALO + tp, :] = z_ref[...]
    for gi, w in enumerate(POOL_WINDOWS):
        lanes = slice(POOL_GROUP_DIM * gi, POOL_GROUP_DIM * (gi + 1))
        tok = ext_ref[POOL_HALO:POOL_HALO + tp, lanes]
        acc = tok
        for d in range(1, w):
            acc = acc + ext_ref[POOL_HALO - d:POOL_HALO - d + tp, lanes]
        if from_state:
            mean = acc / float(w)
        else:
            pos = t * tp + lax.broadcasted_iota(jnp.int32, (tp, 1), 0)
            mean = acc / jnp.minimum(pos + 1, w).astype(F32)
        p = (mean - tok).astype(BF16)
        y_ref[:, lanes] = (_dot(p, w_ref[gi]) * scale_ref[:, lanes]).astype(BF16)
    if not from_state:
        ext_ref[0:POOL_HALO, :] = ext_ref[tp:tp + POOL_HALO, :]


def _pool_prompt(z, w_pool, pool_scale, li, tp=512):
    T = z.shape[0]
    nt = SEQ // tp
    return pl.pallas_call(
        functools.partial(_pool_kernel, tp=tp, from_state=False),
        out_shape=jax.ShapeDtypeStruct((T, POOL_WIDTH), BF16),
        grid=(BATCH, nt),
        in_specs=[
            pl.BlockSpec((tp, POOL_WIDTH), lambda b, t: (b * nt + t, 0)),
            pl.BlockSpec((None, len(POOL_WINDOWS), POOL_GROUP_DIM, POOL_GROUP_DIM), lambda b, t: (li, 0, 0, 0)),
            pl.BlockSpec((None, 1, POOL_WIDTH), lambda b, t: (li, 0, 0)),
        ],
        out_specs=pl.BlockSpec((tp, POOL_WIDTH), lambda b, t: (b * nt + t, 0)),
        scratch_shapes=[pltpu.VMEM((POOL_HALO + tp, POOL_WIDTH), F32)],
        compiler_params=_params("arbitrary", "arbitrary"),
        name="pool_prompt",
    )(z, w_pool, pool_scale)


def _pool_sample(z, state_pool, w_pool, pool_scale, li):
    T = z.shape[0]
    return pl.pallas_call(
        functools.partial(_pool_kernel, tp=DEC_SEQ, from_state=True),
        out_shape=jax.ShapeDtypeStruct((T, POOL_WIDTH), BF16),
        grid=(DEC_BATCH,),
        in_specs=[
            pl.BlockSpec((DEC_SEQ, POOL_WIDTH), lambda b: (b, 0)),
            pl.BlockSpec((None, None, POOL_KEEP, POOL_WIDTH), lambda b: (li, b, 0, 0)),
            pl.BlockSpec((None, len(POOL_WINDOWS), POOL_GROUP_DIM, POOL_GROUP_DIM), lambda b: (li, 0, 0, 0)),
            pl.BlockSpec((None, 1, POOL_WIDTH), lambda b: (li, 0, 0)),
        ],
        out_specs=pl.BlockSpec((DEC_SEQ, POOL_WIDTH), lambda b: (b, 0)),
        scratch_shapes=[pltpu.VMEM((POOL_HALO + DEC_SEQ, POOL_WIDTH), F32)],
        compiler_params=_params("parallel"),
        name="pool_sample",
    )(z, state_pool, w_pool, pool_scale)


def _proj_res_norm_kernel(*refs, n_a, head_major, emit_x):
    a_refs = refs[:n_a]
    w_refs = refs[n_a:2 * n_a]
    x_ref, g_ref = refs[2 * n_a:2 * n_a + 2]
    outs = refs[2 * n_a + 2:]
    if head_major:
        (a_ref,), (w_ref,) = a_refs, w_refs
        xo_ref, xn_ref, cat_ref = outs
        for h in range(C_HEADS):
            cat_ref[:, HEAD_I * h:HEAD_I * (h + 1)] = a_ref[h]
        acc = _dot(cat_ref[...], w_ref[...])
    else:
        xo_ref, xn_ref = outs if emit_x else (None, outs[0])
        acc = _dot(a_refs[0][...], w_refs[0][...])
        for a_ref, w_ref in zip(a_refs[1:], w_refs[1:]):
            acc = acc + _dot(a_ref[...], w_ref[...])
    xnew = x_ref[...] + acc
    if xo_ref is not None:
        xo_ref[...] = xnew
    xn_ref[...] = _rms(xnew, g_ref[...]).astype(xn_ref.dtype)


def _proj_res_norm(a_list, w, x, g, li, head_major=False, tm=512):
    T = x.shape[0]
    row = lambda i: (i, 0)
    n_a = len(a_list)
    in_specs = []
    for a in a_list:
        if head_major:
            in_specs.append(pl.BlockSpec((C_HEADS, tm, HEAD_I), lambda i: (0, i, 0)))
        else:
            in_specs.append(pl.BlockSpec((tm, a.shape[1]), row))
    for kb, a in enumerate(a_list):
        kdim = C_HEADS * HEAD_I if head_major else a.shape[1]
        in_specs.append(pl.BlockSpec((None, kdim, D_MODEL), lambda i, kb=kb: (li, kb, 0)))
    in_specs += [pl.BlockSpec((tm, D_MODEL), row), pl.BlockSpec((1, D_MODEL), lambda i: (0, 0))]
    scratch = [pltpu.VMEM((tm, D_MODEL), BF16)] if head_major else []
    return pl.pallas_call(
        functools.partial(_proj_res_norm_kernel, n_a=n_a, head_major=head_major, emit_x=True),
        out_shape=(jax.ShapeDtypeStruct((T, D_MODEL), F32), jax.ShapeDtypeStruct((T, D_MODEL), BF16)),
        grid=(T // tm,),
        in_specs=in_specs,
        out_specs=(pl.BlockSpec((tm, D_MODEL), row), pl.BlockSpec((tm, D_MODEL), row)),
        scratch_shapes=scratch,
        compiler_params=_params("parallel"),
        name="proj_res_norm",
    )(*a_list, *([w] * n_a), x, g)


def _odd_in_kernel(xn_ref, w_ref, lbp_ref, o_ref, *, li):
    s = pl.program_id(0)
    h = _dot(xn_ref[...], w_ref[...])
    sg = jax.nn.sigmoid(h)
    p = lbp_ref[...]
    e = jnp.exp(p - jnp.max(p, axis=0, keepdims=True))
    sm = e / jnp.sum(e, axis=0, keepdims=True)
    lb = jnp.clip(jnp.sum(sm[:li + 1], axis=0, keepdims=True) - sm[0:1], 0.0, 1.0)
    f = lb + (1.0 - lb) * sg
    out = jnp.where(s == 1, f, jnp.where(s == 2, h, h * sg))
    for hd in range(C_HEADS):
        o_ref[hd] = out[:, HEAD_I * hd:HEAD_I * (hd + 1)]


def _odd_in(xn, w_in_c, lb_param, li, tm=1024):
    T = xn.shape[0]
    return pl.pallas_call(
        functools.partial(_odd_in_kernel, li=li),
        out_shape=jax.ShapeDtypeStruct((4, C_HEADS, T, HEAD_I), F32),
        grid=(4, T // tm),
        in_specs=[
            pl.BlockSpec((tm, D_MODEL), lambda s, i: (i, 0)),
            pl.BlockSpec((None, D_MODEL, D_MODEL), lambda s, i: (li, 0, s)),
            pl.BlockSpec(lb_param.shape, lambda s, i: (0, 0)),
        ],
        out_specs=pl.BlockSpec((None, C_HEADS, tm, HEAD_I), lambda s, i: (s, 0, i, 0)),
        compiler_params=_params("parallel", "parallel"),
        name="odd_in",
    )(xn, w_in_c, lb_param)


def _gla_kernel(*refs, has_s0):
    if has_s0:
        q_ref, f_ref, v_ref, gs_ref, gon_ref, s0_ref, og_ref, sout_ref, st_ref = refs
    else:
        q_ref, f_ref, v_ref, gs_ref, gon_ref, og_ref, sout_ref, st_ref = refs
    C = GLA_CHUNK
    n = pl.program_id(1)
    last = pl.num_programs(1) - 1

    @pl.when(n == 0)
    def _():
        if has_s0:
            st_ref[...] = s0_ref[...]
        else:
            st_ref[...] = jnp.zeros(st_ref.shape, F32)

    r2 = lax.broadcasted_iota(jnp.int32, (C, C), 0)
    c2 = lax.broadcasted_iota(jnp.int32, (C, C), 1)
    tri = (c2 <= r2).astype(BF16)
    ones = jnp.ones((HEAD_F, LANES), BF16)
    gon = gon_ref[...]
    levels = []
    m = GLA_SUB
    while m < C:
        keep = (r2 // (2 * m) == c2 // (2 * m)) & (r2 % (2 * m) >= m) & (c2 % (2 * m) < m)
        levels.append((m, keep))
        m *= 2
    near = [(c2 == r2 - delta) & (r2 % GLA_SUB >= delta) for delta in range(GLA_SUB)]

    def bcast_row(a, period, r):
        a3 = a.reshape(C // period, period, a.shape[-1])
        return jnp.broadcast_to(a3[:, r:r + 1, :], a3.shape).reshape(C, a.shape[-1])

    def shift_in_group(a, delta):
        if delta == 0:
            return a
        a3 = a.reshape(C // GLA_SUB, GLA_SUB, a.shape[-1])
        return pltpu.roll(a3, delta, 1).reshape(C, a.shape[-1])

    def split3(g):
        g1 = g.astype(BF16)
        r1 = g - g1.astype(F32)
        g2 = r1.astype(BF16)
        return g1, g2, (r1 - g2.astype(F32)).astype(BF16)

    def head_group(hg, row0):
        hs = [hg * GLA_HEADS_PER_ITER + u for u in range(GLA_HEADS_PER_ITER)]
        U = range(len(hs))
        rows = pl.ds(row0, C)
        q = [q_ref[h, rows, :] for h in hs]
        f = [f_ref[h, rows, :] for h in hs]
        v_bf = [v_ref[h, rows, :].astype(BF16) for h in hs]
        fc = [jnp.maximum(x, F_MIN) for x in f]
        kk = [1.0 - x for x in f]
        gp = [split3(jnp.log2(x)) for x in fc]
        b = [_dot(tri, gp[u][0]) + _dot(tri, gp[u][1]) + _dot(tri, gp[u][2]) for u in U]
        st = [st_ref[h] for h in hs]
        o = [_dot((q[u] * jnp.exp2(b[u])).astype(BF16), st[u].astype(BF16)) for u in U]
        b_last = [x[C - 1:C, :] for x in b]
        k_dec = [(kk[u] * jnp.exp2(b_last[u] - b[u])).astype(BF16) for u in U]
        for u in U:
            d_rows = jnp.broadcast_to(jnp.exp2(b_last[u]), (LANES, HEAD_F)).T
            st_ref[hs[u]] = st[u] * d_rows + _dot_tn(k_dec[u], v_bf[u])

        a = [jnp.zeros((C, C), F32) for _ in U]
        for m, keep in levels:
            for u in U:
                d = b[u] - bcast_row(b[u], 2 * m, m - 1)
                qt = (q[u] * jnp.exp2(jnp.minimum(d, 0.0))).astype(BF16)
                kt = (kk[u] * jnp.exp2(jnp.minimum(-d, 0.0))).astype(BF16)
                a[u] = jnp.where(keep, _dot_nt(qt, kt), a[u])
        kd = list(kk)
        for delta in range(GLA_SUB):
            for u in U:
                if delta > 0:
                    kd[u] = shift_in_group(kd[u], 1) * fc[u]
                p = (q[u] * kd[u]).astype(BF16)
                a[u] = jnp.where(near[delta], _dot(p, ones)[:, :C], a[u])
        for u in U:
            y = _rms(o[u] + _dot(a[u].astype(BF16), v_bf[u]), gon) * gs_ref[hs[u], rows, :]
            og_ref[hs[u], rows, :] = y.astype(BF16)
        return row0

    def chunk(c, carry):
        lax.fori_loop(0, C_HEADS // GLA_HEADS_PER_ITER, head_group, pl.multiple_of(c * C, C))
        return carry

    lax.fori_loop(0, q_ref.shape[1] // C, chunk, 0)

    @pl.when(n == last)
    def _():
        sout_ref[...] = st_ref[...]


def _gla(qfvg, g_onorm, s0, li, n_streams, seq):
    T = qfvg.shape[2]
    C = min(GLA_STEP_ROWS, seq)
    nc = seq // C
    has_s0 = s0 is not None
    sec = lambda k: pl.BlockSpec((None, C_HEADS, C, HEAD_F), lambda b, n, k=k: (k, 0, b * nc + n, 0))
    in_specs = [sec(0), sec(1), sec(2), sec(3),
                pl.BlockSpec((None, 1, HEAD_I), lambda b, n: (li, 0, 0))]
    args = [qfvg, qfvg, qfvg, qfvg, g_onorm]
    if has_s0:
        in_specs.append(pl.BlockSpec((None, None, C_HEADS, HEAD_F, HEAD_I), lambda b, n: (li, b, 0, 0, 0)))
        args.append(s0)
    return pl.pallas_call(
        functools.partial(_gla_kernel, has_s0=has_s0),
        out_shape=(jax.ShapeDtypeStruct((C_HEADS, T, HEAD_I), BF16),
                   jax.ShapeDtypeStruct((n_streams, C_HEADS, HEAD_F, HEAD_I), F32)),
        grid=(n_streams, nc),
        in_specs=in_specs,
        out_specs=(pl.BlockSpec((C_HEADS, C, HEAD_I), lambda b, n: (0, b * nc + n, 0)),
                   pl.BlockSpec((None, C_HEADS, HEAD_F, HEAD_I), lambda b, n: (b, 0, 0, 0))),
        scratch_shapes=[pltpu.VMEM((C_HEADS, HEAD_I, HEAD_F), F32)],
        compiler_params=_params("parallel", "arbitrary"),
        name="gla",
    )(*args)


FFN_PAD = SUBLANES
FFN_SUB_ROWS = 256


def _ffn_up_kernel(*refs, tm, seg, from_state, tiles_per_stream):
    if from_state:
        (xn_ref, wa_ref, wb_ref, cwa_ref, cwb_ref, cba_ref, cbb_ref, pa_ref, pb_ref,
         act_ref, hla_ref, hlb_ref, w_ref) = refs
        carry_ref = None
    else:
        (xn_ref, wa_ref, wb_ref, cwa_ref, cwb_ref, cba_ref, cbb_ref,
         act_ref, hla_ref, hlb_ref, w_ref, carry_ref) = refs
        pa_ref = pb_ref = None
    j = pl.program_id(0)
    i = pl.program_id(1)
    sub = min(FFN_SUB_ROWS, tm)
    assert (seg % sub == 0 and not from_state) or (sub % seg == 0 and sub > seg)
    halves = ((0, cwa_ref, cba_ref, pa_ref, hla_ref), (1, cwb_ref, cbb_ref, pb_ref, hlb_ref))

    @pl.when(i == 0)
    def _():
        w_ref[0] = wa_ref[...].astype(BF16)
        w_ref[1] = wb_ref[...].astype(BF16)

    def conv(front, h, cw, cb):
        n = h.shape[0]
        ext = jnp.concatenate([front, h], axis=0)
        return cb + (ext[FFN_PAD - 2:FFN_PAD - 2 + n] * cw[0:1] + ext[FFN_PAD - 1:FFN_PAD - 1 + n] * cw[1:2] +
                     h * cw[2:3])

    fronts = [None, None]
    if not from_state:
        @pl.when(i % tiles_per_stream == 0)
        def _():
            carry_ref[j] = jnp.zeros((2, FFN_PAD, FFN_TN), F32)
        fronts = [carry_ref[j, 0], carry_ref[j, 1]]

    for r0 in range(0, tm, sub):
        convs = []
        for hidx, cw_ref, cb_ref, past_ref, hl_ref in halves:
            h = _dot(xn_ref[r0:r0 + sub, :], w_ref[hidx])
            cw = cw_ref[...]
            cb = cb_ref[...]
            if seg >= sub:
                convs.append([conv(fronts[hidx], h, cw, cb)])
                fronts[hidx] = h[sub - FFN_PAD:sub]
                if (r0 + sub) % seg == 0:
                    hl_ref[(r0 + sub) // seg - 1] = h[sub - 2:sub]
            else:
                pieces = []
                for k in range(sub // seg):
                    s = r0 // seg + k
                    hk = h[k * seg:(k + 1) * seg]
                    front = jnp.concatenate([jnp.zeros((FFN_PAD - 2, FFN_TN), F32), past_ref[s]], axis=0)
                    pieces.append(conv(front, hk, cw, cb))
                    hl_ref[s] = hk[seg - 2:seg]
                convs.append(pieces)
        for n, (a, b) in enumerate(zip(*convs)):
            rows = a.shape[0]
            o = r0 + n * rows
            half = 0.5 * a
            act_ref[o:o + rows, :] = ((half + half * jnp.tanh(half)) * b).astype(BF16)

    if not from_state:
        carry_ref[j, 0] = fronts[0]
        carry_ref[j, 1] = fronts[1]


def _ffn_up(xn, w_up, conv_w, conv_b, conv_past, layer, tm, seg):
    T = xn.shape[0]
    tn = FFN_TN
    nj = D_FF // tn
    nseg = tm // seg
    from_state = conv_past is not None
    wmap_a = lambda j, i: (layer, 0, j)
    wmap_b = lambda j, i: (layer, 0, nj + j)
    in_specs = [
        pl.BlockSpec((tm, D_MODEL), lambda j, i: (i, 0)),
        pl.BlockSpec((None, D_MODEL, tn), wmap_a), pl.BlockSpec((None, D_MODEL, tn), wmap_b),
        pl.BlockSpec((None, CONV_W, tn), wmap_a), pl.BlockSpec((None, CONV_W, tn), wmap_b),
        pl.BlockSpec((None, 1, tn), wmap_a), pl.BlockSpec((None, 1, tn), wmap_b),
    ]
    args = [xn, w_up, w_up, conv_w, conv_w, conv_b, conv_b]
    scratch = [pltpu.VMEM((2, D_MODEL, tn), BF16)]
    if from_state:
        in_specs += [pl.BlockSpec((None, nseg, CONV_W - 1, tn), lambda j, i: (layer, i, 0, j)),
                     pl.BlockSpec((None, nseg, CONV_W - 1, tn), lambda j, i: (layer, i, 0, nj + j))]
        args += [conv_past, conv_past]
        tiles_per_stream = 1
    else:
        scratch.append(pltpu.VMEM((nj, 2, FFN_PAD, tn), F32))
        tiles_per_stream = SEQ // tm
    hl_shape = jax.ShapeDtypeStruct((T // seg, CONV_W - 1, D_FF), F32)
    hl_spec = pl.BlockSpec((nseg, CONV_W - 1, tn), lambda j, i: (i, 0, j))
    return pl.pallas_call(
        functools.partial(_ffn_up_kernel, tm=tm, seg=seg, from_state=from_state,
                          tiles_per_stream=tiles_per_stream),
        out_shape=(jax.ShapeDtypeStruct((T, D_FF), BF16), hl_shape, hl_shape),
        grid=(nj, T // tm),
        in_specs=in_specs,
        out_specs=(pl.BlockSpec((tm, tn), lambda j, i: (i, j)), hl_spec, hl_spec),
        scratch_shapes=scratch,
        compiler_params=_params("arbitrary", "arbitrary"),
        name="ffn_up",
    )(*args)


def _ffn_down_kernel(a_ref, w_ref, x_ref, g_ref, *outs, emit_x):
    if emit_x:
        acc_ref, xn_ref = outs
    else:
        xn_ref, acc_ref = outs
    k = pl.program_id(1)

    @pl.when(k == 0)
    def _():
        acc_ref[...] = x_ref[...]

    acc_ref[...] += _dot(a_ref[...], w_ref[...])

    @pl.when(k == pl.num_programs(1) - 1)
    def _():
        xn_ref[...] = _rms(acc_ref[...], g_ref[...]).astype(xn_ref.dtype)


def _ffn_down(act, w_down, x, g, layer, emit_x, xn_dtype, tm=1024, nk=11):
    T = x.shape[0]
    tk = D_FF // nk
    row = lambda i, k: (i, 0)
    out_shape = [jax.ShapeDtypeStruct((T, D_MODEL), xn_dtype)]
    out_specs = [pl.BlockSpec((tm, D_MODEL), row)]
    scratch = [pltpu.VMEM((tm, D_MODEL), F32)]
    if emit_x:
        out_shape.insert(0, jax.ShapeDtypeStruct((T, D_MODEL), F32))
        out_specs.insert(0, pl.BlockSpec((tm, D_MODEL), row))
        scratch = []
    res = pl.pallas_call(
        functools.partial(_ffn_down_kernel, emit_x=emit_x),
        out_shape=tuple(out_shape),
        grid=(T // tm, nk),
        in_specs=[
            pl.BlockSpec((tm, tk), lambda i, k: (i, k)),
            pl.BlockSpec((None, tk, D_MODEL), lambda i, k: (layer, k, 0)),
            pl.BlockSpec((tm, D_MODEL), row),
            pl.BlockSpec((1, D_MODEL), lambda i, k: (0, 0)),
        ],
        out_specs=tuple(out_specs),
        scratch_shapes=scratch,
        compiler_params=_params("parallel", "arbitrary"),
        name="ffn_down",
    )(act, w_down, x, g)
    return res if emit_x else (None, res[0])


def _rope_tables(pos):
    inv = ROPE_THETA ** (-jnp.arange(0, ROPE_DIM, 2, dtype=F32) / ROPE_DIM)
    ang = pos.astype(F32)[:, None] * inv[None, :]
    cos, sin = jnp.cos(ang), jnp.sin(ang)
    reps = LANES // ROPE_DIM
    return (jnp.tile(jnp.concatenate([cos, cos], axis=-1), (1, reps)),
            jnp.tile(jnp.concatenate([-sin, sin], axis=-1), (1, reps)))


def kernel(x_prompt, x_sample, cache_mla_latent, cache_mla_krope, state_pool, state_hgrn, state_ffn_conv,
           g_mix, g_ffn, g_final, w_in_a, g_qa, w_qb, g_kva, w_uk, w_uv, w_pool, pool_scale, w_out_a,
           w_in_c, lb_param, g_onorm, w_out_c, w_up, conv_w, conv_b, w_down):
    n_even = w_in_a.shape[0]
    n_odd = w_in_c.shape[0]
    Tp, Ts = BATCH * SEQ, DEC_BATCH * DEC_SEQ

    o1, o2, o3 = Q_RANK, Q_RANK + KV_RANK, Q_RANK + KV_RANK + ROPE_DIM
    w_in_e = jnp.concatenate(
        [w_in_a[..., :o2], w_in_a[..., o3:], w_in_a[..., o2:o3],
         jnp.zeros((n_even, D_MODEL, LANES - ROPE_DIM), w_in_a.dtype)], axis=-1).astype(BF16)
    w_qb4 = w_qb.reshape(n_even, Q_RANK, MLA_HEADS, NOPE_DIM + ROPE_DIM)
    w_qb_e = jnp.concatenate(
        [w_qb4[..., :NOPE_DIM].reshape(n_even, Q_RANK, Q_NOPE_ALL),
         w_qb4[..., NOPE_DIM:].reshape(n_even, Q_RANK, MLA_HEADS * ROPE_DIM)], axis=-1).astype(BF16)
    w_ukT = jnp.transpose(w_uk, (0, 2, 3, 1)).astype(BF16)
    w_uvT = jnp.transpose(w_uv, (0, 2, 1, 3)).astype(BF16)
    w_pool_b = w_pool.astype(BF16)
    w_out_a_b = w_out_a.astype(BF16)
    w_in_c_b = w_in_c.astype(BF16)
    w_out_c_b = w_out_c.astype(BF16)
    w_down_b = w_down.astype(BF16)
    g_qa3 = g_qa[:, None, :]
    g_kva3 = g_kva[:, None, :]
    pool_scale3 = pool_scale[:, None, :]
    g_onorm3 = g_onorm[:, None, :]
    conv_b3 = conv_b[:, None, :]

    cos_p, sin_p = _rope_tables(jnp.tile(jnp.arange(SEQ), BATCH))
    cos_s, sin_s = _rope_tables(jnp.tile(PAST_LEN + jnp.arange(DEC_SEQ), DEC_BATCH))

    xp = x_prompt.reshape(Tp, D_MODEL)
    xs = x_sample.reshape(Ts, D_MODEL)
    xnp = _norm(xp, g_mix[0:1])
    xns = _norm(xs, g_mix[0:1])

    lat_p, kpe_p, pool_p, hg_p, cv_p = [], [], [], [], []
    lat_s, kpe_s, pool_s, hg_s, cv_s = [], [], [], [], []
    for layer in range(DEPTH):
        li = layer // 2
        g_next = g_ffn[layer:layer + 1]
        if layer % 2 == 0:
            wa = (w_in_e, g_qa3, g_kva3, w_qb_e, w_ukT)
            lat, latb, kpe, kpeb, z, qlat, qpe = _even_in(xnp, *wa, cos_p, sin_p, li)
            y_mla = _attn_prompt(qlat, qpe, latb, kpeb, w_uvT, li)
            y_pool = _pool_prompt(z, w_pool_b, pool_scale3, li)
            xp, xnp = _proj_res_norm([y_mla, y_pool], w_out_a_b, xp, g_next, li)
            lat_p.append(lat.reshape(BATCH, SEQ, KV_RANK))
            kpe_p.append(kpe.reshape(BATCH, SEQ, ROPE_DIM))
            pool_p.append(z.reshape(BATCH, SEQ, POOL_WIDTH)[:, SEQ - POOL_KEEP:])

            lat, latb, kpe, kpeb, z, qlat, qpe = _even_in(xns, *wa, cos_s, sin_s, li)
            y_mla = _attn_sample(qlat, qpe, cache_mla_latent, cache_mla_krope, latb, kpeb, w_uvT, li)
            y_pool = _pool_sample(z, state_pool, w_pool_b, pool_scale3, li)
            xs, xns = _proj_res_norm([y_mla, y_pool], w_out_a_b, xs, g_next, li)
            lat_s.append(lat.reshape(DEC_BATCH, DEC_SEQ, KV_RANK))
            kpe_s.append(kpe.reshape(DEC_BATCH, DEC_SEQ, ROPE_DIM))
            pool_s.append(z.reshape(DEC_BATCH, DEC_SEQ, POOL_WIDTH)[:, DEC_SEQ - POOL_KEEP:])
        else:
            qfvg = _odd_in(xnp, w_in_c_b, lb_param, li)
            og, s_fin = _gla(qfvg, g_onorm3, None, li, BATCH, SEQ)
            xp, xnp = _proj_res_norm([og], w_out_c_b, xp, g_next, li, head_major=True)
            hg_p.append(s_fin)

            qfvg = _odd_in(xns, w_in_c_b, lb_param, li)
            og, s_fin = _gla(qfvg, g_onorm3, state_hgrn, li, DEC_BATCH, DEC_SEQ)
            xs, xns = _proj_res_norm([og], w_out_c_b, xs, g_next, li, head_major=True)
            hg_s.append(s_fin)

        final = layer == DEPTH - 1
        g_after = g_final[None, :] if final else g_mix[layer + 1:layer + 2]
        xn_dtype = F32 if final else BF16

        tm_p = SEQ
        act, hla, hlb = _ffn_up(xnp, w_up, conv_w, conv_b3, None, layer, tm=tm_p, seg=tm_p)
        xp, xnp = _ffn_down(act, w_down_b, xp, g_after, layer, not final, xn_dtype)
        hl = jnp.concatenate([hla, hlb], axis=-1).reshape(BATCH, SEQ // tm_p, CONV_W - 1, 2 * D_FF)
        cv_p.append(hl[:, -1])

        act, hla, hlb = _ffn_up(xns, w_up, conv_w, conv_b3, state_ffn_conv, layer, tm=512, seg=DEC_SEQ)
        xs, xns = _ffn_down(act, w_down_b, xs, g_after, layer, not final, xn_dtype)
        cv_s.append(jnp.concatenate([hla, hlb], axis=-1))

    return (xnp.reshape(BATCH, SEQ, D_MODEL), xns.reshape(DEC_BATCH, DEC_SEQ, D_MODEL),
            jnp.stack(lat_p), jnp.stack(kpe_p), jnp.stack(pool_p), jnp.stack(hg_p), jnp.stack(cv_p),
            jnp.stack(lat_s), jnp.stack(kpe_s), jnp.stack(pool_s), jnp.stack(hg_s), jnp.stack(cv_s))
```

```python
import functools

import jax
import jax.numpy as jnp
from jax import lax
from jax.experimental import pallas as pl
from jax.experimental.pallas import tpu as pltpu

D_MODEL = 2048
BATCH = 8
SEQ = 2048
DEPTH = 4
DEC_BATCH = 32
DEC_SEQ = 64
PAST_LEN = 4096
CHUNK = 64
MLA_HEADS = 8
Q_RANK = 512
KV_RANK = 512
NOPE_DIM = 128
ROPE_DIM = 64
V_DIM = 128
ROPE_THETA = 10000.0
MLA_SCALE = (NOPE_DIM + ROPE_DIM) ** -0.5
MASK_NEG = -1e30
POOL_WINDOWS = (2, 4, 8, 16)
POOL_WIDTH = D_MODEL - MLA_HEADS * V_DIM
POOL_GROUP_DIM = POOL_WIDTH // len(POOL_WINDOWS)
POOL_KEEP = max(POOL_WINDOWS) - 1
C_HEADS = 16
HEAD_F = 128
HEAD_I = D_MODEL // C_HEADS
F_MIN = 1e-30
D_FF = 5632
CONV_W = 3
EPS = 1e-6

F32 = jnp.float32
BF16 = jnp.bfloat16

LANES = 128
SUBLANES = 8
VMEM_LIMIT = 56 * 1024 * 1024

IN_EVEN_PAD = Q_RANK + KV_RANK + POOL_WIDTH + LANES
Q_NOPE_ALL = MLA_HEADS * NOPE_DIM
GLA_CHUNK = 64
GLA_SUB = SUBLANES
GLA_HEADS_PER_ITER = 16
GLA_STEP_ROWS = 64
FFN_TN = 512


def _params(*sem):
    return pltpu.CompilerParams(dimension_semantics=sem, vmem_limit_bytes=VMEM_LIMIT)


def _dot(a, b):
    return jnp.dot(a, b, preferred_element_type=F32)


def _dot_nt(a, b):
    return lax.dot_general(a, b, (((1,), (1,)), ((), ())), preferred_element_type=F32)


def _dot_tn(a, b):
    return lax.dot_general(a, b, (((0,), (0,)), ((), ())), preferred_element_type=F32)


def _rms(x, g):
    return x * lax.rsqrt(jnp.mean(x * x, axis=-1, keepdims=True) + EPS) * g


def _norm_kernel(x_ref, g_ref, xn_ref):
    xn_ref[...] = _rms(x_ref[...], g_ref[...]).astype(xn_ref.dtype)


def _norm(x, g, tm=512):
    T = x.shape[0]
    return pl.pallas_call(
        _norm_kernel,
        out_shape=jax.ShapeDtypeStruct((T, D_MODEL), BF16),
        grid=(T // tm,),
        in_specs=[pl.BlockSpec((tm, D_MODEL), lambda i: (i, 0)),
                  pl.BlockSpec((1, D_MODEL), lambda i: (0, 0))],
        out_specs=pl.BlockSpec((tm, D_MODEL), lambda i: (i, 0)),
        compiler_params=_params("parallel"),
        name="norm0",
    )(x, g)


def _rope(s, cos, sin, first_half):
    swapped = jnp.where(first_half, pltpu.roll(s, 96, 1), pltpu.roll(s, 32, 1))
    return s * cos + swapped * sin


def _even_in_kernel(xn_ref, w_ref, gqa_ref, gkva_ref, wqb_ref, wuk_ref, cos_ref, sin_ref,
                    lat_ref, latb_ref, kpe_ref, kpeb_ref, z_ref, qlat_ref, qpe_ref):
    tm = xn_ref.shape[0]
    acc = _dot(xn_ref[...], w_ref[...])
    o1, o2, o3 = Q_RANK, Q_RANK + KV_RANK, Q_RANK + KV_RANK + POOL_WIDTH
    z_ref[...] = acc[:, o2:o3]
    lat = _rms(acc[:, o1:o2], gkva_ref[...])
    lat_ref[...] = lat
    latb_ref[...] = lat.astype(BF16)
    cos = cos_ref[...]
    sin = sin_ref[...]
    lane = lax.broadcasted_iota(jnp.int32, (tm, LANES), 1)
    first_half = (lane % ROPE_DIM) < (ROPE_DIM // 2)
    kr = _rope(acc[:, o3:o3 + LANES], cos, sin, first_half)
    kpe_ref[...] = kr[:, :ROPE_DIM]
    kpeb_ref[...] = kr[:, :ROPE_DIM].astype(BF16)
    cqn = _rms(acc[:, :o1], gqa_ref[...]).astype(BF16)
    q = _dot(cqn, wqb_ref[...])
    for s in range(MLA_HEADS // 2):
        lo = Q_NOPE_ALL + LANES * s
        r = (_rope(q[:, lo:lo + LANES], cos, sin, first_half) * MLA_SCALE).astype(BF16)
        qpe_ref[2 * s] = r[:, :ROPE_DIM]
        qpe_ref[2 * s + 1] = r[:, ROPE_DIM:]
    for h in range(MLA_HEADS):
        ql = _dot(q[:, NOPE_DIM * h:NOPE_DIM * (h + 1)].astype(BF16), wuk_ref[h])
        qlat_ref[h] = (ql * MLA_SCALE).astype(BF16)


def _even_in(xn, w_in, g_qa, g_kva, w_qb, w_ukT, cos, sin, li, tm=256):
    T = xn.shape[0]
    row = lambda i: (i, 0)
    fix2 = lambda i: (0, 0)
    out_shape = (
        jax.ShapeDtypeStruct((T, KV_RANK), F32), jax.ShapeDtypeStruct((T, KV_RANK), BF16),
        jax.ShapeDtypeStruct((T, ROPE_DIM), F32), jax.ShapeDtypeStruct((T, ROPE_DIM), BF16),
        jax.ShapeDtypeStruct((T, POOL_WIDTH), F32),
        jax.ShapeDtypeStruct((MLA_HEADS, T, KV_RANK), BF16),
        jax.ShapeDtypeStruct((MLA_HEADS, T, ROPE_DIM), BF16),
    )
    return pl.pallas_call(
        _even_in_kernel,
        out_shape=out_shape,
        grid=(T // tm,),
        in_specs=[
            pl.BlockSpec((tm, D_MODEL), row),
            pl.BlockSpec((None, D_MODEL, IN_EVEN_PAD), lambda i: (li, 0, 0)),
            pl.BlockSpec((None, 1, Q_RANK), lambda i: (li, 0, 0)),
            pl.BlockSpec((None, 1, KV_RANK), lambda i: (li, 0, 0)),
            pl.BlockSpec((None, Q_RANK, MLA_HEADS * (NOPE_DIM + ROPE_DIM)), lambda i: (li, 0, 0)),
            pl.BlockSpec((None, MLA_HEADS, NOPE_DIM, KV_RANK), lambda i: (li, 0, 0, 0)),
            pl.BlockSpec((tm, LANES), row),
            pl.BlockSpec((tm, LANES), row),
        ],
        out_specs=(
            pl.BlockSpec((tm, KV_RANK), row), pl.BlockSpec((tm, KV_RANK), row),
            pl.BlockSpec((tm, ROPE_DIM), row), pl.BlockSpec((tm, ROPE_DIM), row),
            pl.BlockSpec((tm, POOL_WIDTH), row),
            pl.BlockSpec((MLA_HEADS, tm, KV_RANK), lambda i: (0, i, 0)),
            pl.BlockSpec((MLA_HEADS, tm, ROPE_DIM), lambda i: (0, i, 0)),
        ),
        compiler_params=_params("parallel"),
        name="even_in",
    )(xn, w_in, g_qa, g_kva, w_qb, w_ukT, cos, sin)


ATTN_HEAD_GROUP = 4
ATTN_SAMPLE_SUB_KEYS = 512


def _lane_tile(x, width):
    if width <= LANES:
        return x[:, :width]
    return jnp.concatenate([x] * (width // LANES), axis=1)


def _attn_keys_step(qlat_ref, qpe_ref, k, kp, m_ref, l_ref, acc_ref, tq, group, visible=None,
                    kp_transposed=False):
    tk = k.shape[0]
    n = group * tq

    def scores(g):
        ql = qlat_ref[g * group:(g + 1) * group].reshape(n, KV_RANK)
        qp = qpe_ref[g * group:(g + 1) * group].reshape(n, ROPE_DIM)
        s = _dot_nt(ql, k) + (_dot(qp, kp) if kp_transposed else _dot_nt(qp, kp))
        return s if visible is None else jnp.where(visible, s, MASK_NEG)

    s_next = scores(0)
    for g in range(MLA_HEADS // group):
        s = s_next
        if g + 1 < MLA_HEADS // group:
            s_next = scores(g + 1)
        rows = pl.ds(g * n, n)
        m_prev = m_ref[rows, :]
        m_new = jnp.maximum(m_prev, jnp.max(s, axis=1, keepdims=True))
        alpha = jnp.exp(m_prev - m_new)
        p = jnp.exp(s - _lane_tile(m_new, tk))
        l_ref[rows, :] = alpha * l_ref[rows, :] + jnp.sum(p, axis=1, keepdims=True)
        acc_ref[rows, :] = _lane_tile(alpha, KV_RANK) * acc_ref[rows, :] + _dot(p.astype(BF16), k)
        m_ref[rows, :] = m_new


def _attn_init(m_ref, l_ref, acc_ref):
    m_ref[...] = jnp.full(m_ref.shape, MASK_NEG, F32)
    l_ref[...] = jnp.zeros(l_ref.shape, F32)
    acc_ref[...] = jnp.zeros(acc_ref.shape, F32)


def _attn_finish(wuv_ref, y_ref, l_ref, acc_ref, tq):
    for h in range(MLA_HEADS):
        rows = pl.ds(h * tq, tq)
        o = (acc_ref[rows, :] / _lane_tile(l_ref[rows, :], KV_RANK)).astype(BF16)
        y_ref[:, V_DIM * h:V_DIM * (h + 1)] = _dot(o, wuv_ref[h]).astype(BF16)


def _attn_prompt_kernel(qi_ref, kj_ref, qlat_ref, qpe_ref, k_ref, kp_ref, wuv_ref, y_ref,
                        m_ref, l_ref, acc_ref, *, tq):
    p = pl.program_id(1)
    i = qi_ref[p]
    j = kj_ref[p]

    @pl.when(j == 0)
    def _():
        _attn_init(m_ref, l_ref, acc_ref)

    @pl.when(j < i)
    def _():
        _attn_keys_step(qlat_ref, qpe_ref, k_ref[...], kp_ref[...], m_ref, l_ref, acc_ref, tq,
                        ATTN_HEAD_GROUP)

    @pl.when(j == i)
    def _():
        n = ATTN_HEAD_GROUP * tq
        row = lax.broadcasted_iota(jnp.int32, (n, tq), 0) % tq
        col = lax.broadcasted_iota(jnp.int32, (n, tq), 1)
        visible = col // CHUNK <= row // CHUNK
        _attn_keys_step(qlat_ref, qpe_ref, k_ref[...], kp_ref[...], m_ref, l_ref, acc_ref, tq,
                        ATTN_HEAD_GROUP, visible)
        _attn_finish(wuv_ref, y_ref, l_ref, acc_ref, tq)


def _attn_prompt(qlat, qpe, latb, kpeb, w_uvT, li, tq=256):
    assert tq % CHUNK == 0
    T = latb.shape[0]
    nq = SEQ // tq
    pairs = [(i, j) for i in range(nq) for j in range(i + 1)]
    q_of = jnp.asarray([i for i, _ in pairs], jnp.int32)
    k_of = jnp.asarray([j for _, j in pairs], jnp.int32)
    qmap = lambda b, p, qi, kj: (0, b * nq + qi[p], 0)
    kmap = lambda b, p, qi, kj: (b * nq + kj[p], 0)
    rows = MLA_HEADS * tq
    return pl.pallas_call(
        functools.partial(_attn_prompt_kernel, tq=tq),
        out_shape=jax.ShapeDtypeStruct((T, MLA_HEADS * V_DIM), BF16),
        grid_spec=pltpu.PrefetchScalarGridSpec(
            num_scalar_prefetch=2,
            grid=(BATCH, len(pairs)),
            in_specs=[
                pl.BlockSpec((MLA_HEADS, tq, KV_RANK), qmap),
                pl.BlockSpec((MLA_HEADS, tq, ROPE_DIM), qmap),
                pl.BlockSpec((tq, KV_RANK), kmap),
                pl.BlockSpec((tq, ROPE_DIM), kmap),
                pl.BlockSpec((None, MLA_HEADS, KV_RANK, V_DIM), lambda b, p, qi, kj: (li, 0, 0, 0)),
            ],
            out_specs=pl.BlockSpec((tq, MLA_HEADS * V_DIM), lambda b, p, qi, kj: (b * nq + qi[p], 0)),
            scratch_shapes=[pltpu.VMEM((rows, LANES), F32), pltpu.VMEM((rows, LANES), F32),
                            pltpu.VMEM((rows, KV_RANK), F32)],
        ),
        compiler_params=_params("parallel", "arbitrary"),
        name="attn_prompt",
    )(q_of, k_of, qlat, qpe, latb, kpeb, w_uvT)


def _attn_sample_kernel(qlat_ref, qpe_ref, ck_ref, ckp_ref, nk_ref, nkp_ref, wuv_ref, y_ref,
                        m_ref, l_ref, acc_ref):
    j = pl.program_id(1)
    last = pl.num_programs(1) - 1

    @pl.when(j == 0)
    def _():
        _attn_init(m_ref, l_ref, acc_ref)

    def step(k, kp, kp_transposed=False):
        _attn_keys_step(qlat_ref, qpe_ref, k, kp, m_ref, l_ref, acc_ref, DEC_SEQ, MLA_HEADS,
                        kp_transposed=kp_transposed)

    @pl.when(j < last)
    def _():
        for c in range(0, ck_ref.shape[0], ATTN_SAMPLE_SUB_KEYS):
            keys = pl.ds(c, ATTN_SAMPLE_SUB_KEYS)
            step(ck_ref[keys, :].astype(BF16), ckp_ref[:, keys].astype(BF16), kp_transposed=True)

    @pl.when(j == last)
    def _():
        step(nk_ref[...], nkp_ref[...])
        _attn_finish(wuv_ref, y_ref, l_ref, acc_ref, DEC_SEQ)


def _attn_sample(qlat, qpe, cache_lat, cache_kpe_t, latb, kpeb, w_uvT, li, tk=1024):
    assert PAST_LEN % CHUNK == 0 and DEC_SEQ <= CHUNK and PAST_LEN % tk == 0
    T = latb.shape[0]
    nc = PAST_LEN // tk
    qmap = lambda b, j: (0, b, 0)
    cmap = lambda b, j: (li, b, jnp.minimum(j, nc - 1), 0)
    rows = MLA_HEADS * DEC_SEQ
    return pl.pallas_call(
        _attn_sample_kernel,
        out_shape=jax.ShapeDtypeStruct((T, MLA_HEADS * V_DIM), BF16),
        grid=(DEC_BATCH, nc + 1),
        in_specs=[
            pl.BlockSpec((MLA_HEADS, DEC_SEQ, KV_RANK), qmap),
            pl.BlockSpec((MLA_HEADS, DEC_SEQ, ROPE_DIM), qmap),
            pl.BlockSpec((None, None, tk, KV_RANK), cmap),
            pl.BlockSpec((None, None, ROPE_DIM, tk), lambda b, j: (li, b, 0, jnp.minimum(j, nc - 1))),
            pl.BlockSpec((DEC_SEQ, KV_RANK), lambda b, j: (b, 0)),
            pl.BlockSpec((DEC_SEQ, ROPE_DIM), lambda b, j: (b, 0)),
            pl.BlockSpec((None, MLA_HEADS, KV_RANK, V_DIM), lambda b, j: (li, 0, 0, 0)),
        ],
        out_specs=pl.BlockSpec((DEC_SEQ, MLA_HEADS * V_DIM), lambda b, j: (b, 0)),
        scratch_shapes=[pltpu.VMEM((rows, LANES), F32), pltpu.VMEM((rows, LANES), F32),
                        pltpu.VMEM((rows, KV_RANK), F32)],
        compiler_params=_params("parallel", "arbitrary"),
        name="attn_sample",
    )(qlat, qpe, cache_lat, cache_kpe_t, latb, kpeb, w_uvT)


POOL_HALO = 2 * SUBLANES


def _pool_kernel(*refs, tp, from_state):
    if from_state:
        z_ref, past_ref, w_ref, scale_ref, y_ref, ext_ref = refs
        ext_ref[0:1, :] = jnp.zeros((1, POOL_WIDTH), F32)
        ext_ref[1:POOL_HALO, :] = past_ref[...]
    else:
        z_ref, w_ref, scale_ref, y_ref, ext_ref = refs
        t = pl.program_id(1)

        @pl.when(t == 0)
        def _():
            ext_ref[0:POOL_HALO, :] = jnp.zeros((POOL_HALO, POOL_WIDTH), F32)

    ext_ref[POOL_HALO:POOL_HALO + tp, :] = z_ref[...]
    for gi, w in enumerate(POOL_WINDOWS):
        lanes = slice(POOL_GROUP_DIM * gi, POOL_GROUP_DIM * (gi + 1))
        tok = ext_ref[POOL_HALO:POOL_HALO + tp, lanes]
        acc = tok
        for d in range(1, w):
            acc = acc + ext_ref[POOL_HALO - d:POOL_HALO - d + tp, lanes]
        if from_state:
            mean = acc / float(w)
        else:
            pos = t * tp + lax.broadcasted_iota(jnp.int32, (tp, 1), 0)
            mean = acc / jnp.minimum(pos + 1, w).astype(F32)
        p = (mean - tok).astype(BF16)
        y_ref[:, lanes] = (_dot(p, w_ref[gi]) * scale_ref[:, lanes]).astype(BF16)
    if not from_state:
        ext_ref[0:POOL_HALO, :] = ext_ref[tp:tp + POOL_HALO, :]


def _pool_prompt(z, w_pool, pool_scale, li, tp=512):
    T = z.shape[0]
    nt = SEQ // tp
    return pl.pallas_call(
        functools.partial(_pool_kernel, tp=tp, from_state=False),
        out_shape=jax.ShapeDtypeStruct((T, POOL_WIDTH), BF16),
        grid=(BATCH, nt),
        in_specs=[
            pl.BlockSpec((tp, POOL_WIDTH), lambda b, t: (b * nt + t, 0)),
            pl.BlockSpec((None, len(POOL_WINDOWS), POOL_GROUP_DIM, POOL_GROUP_DIM), lambda b, t: (li, 0, 0, 0)),
            pl.BlockSpec((None, 1, POOL_WIDTH), lambda b, t: (li, 0, 0)),
        ],
        out_specs=pl.BlockSpec((tp, POOL_WIDTH), lambda b, t: (b * nt + t, 0)),
        scratch_shapes=[pltpu.VMEM((POOL_HALO + tp, POOL_WIDTH), F32)],
        compiler_params=_params("arbitrary", "arbitrary"),
        name="pool_prompt",
    )(z, w_pool, pool_scale)


def _pool_sample(z, state_pool, w_pool, pool_scale, li):
    T = z.shape[0]
    return pl.pallas_call(
        functools.partial(_pool_kernel, tp=DEC_SEQ, from_state=True),
        out_shape=jax.ShapeDtypeStruct((T, POOL_WIDTH), BF16),
        grid=(DEC_BATCH,),
        in_specs=[
            pl.BlockSpec((DEC_SEQ, POOL_WIDTH), lambda b: (b, 0)),
            pl.BlockSpec((None, None, POOL_KEEP, POOL_WIDTH), lambda b: (li, b, 0, 0)),
            pl.BlockSpec((None, len(POOL_WINDOWS), POOL_GROUP_DIM, POOL_GROUP_DIM), lambda b: (li, 0, 0, 0)),
            pl.BlockSpec((None, 1, POOL_WIDTH), lambda b: (li, 0, 0)),
        ],
        out_specs=pl.BlockSpec((DEC_SEQ, POOL_WIDTH), lambda b: (b, 0)),
        scratch_shapes=[pltpu.VMEM((POOL_HALO + DEC_SEQ, POOL_WIDTH), F32)],
        compiler_params=_params("parallel"),
        name="pool_sample",
    )(z, state_pool, w_pool, pool_scale)


def _proj_res_norm_kernel(*refs, n_a, head_major, emit_x):
    a_refs = refs[:n_a]
    w_refs = refs[n_a:2 * n_a]
    x_ref, g_ref = refs[2 * n_a:2 * n_a + 2]
    outs = refs[2 * n_a + 2:]
    if head_major:
        (a_ref,), (w_ref,) = a_refs, w_refs
        xo_ref, xn_ref, cat_ref = outs
        for h in range(C_HEADS):
            cat_ref[:, HEAD_I * h:HEAD_I * (h + 1)] = a_ref[h]
        acc = _dot(cat_ref[...], w_ref[...])
    else:
        xo_ref, xn_ref = outs if emit_x else (None, outs[0])
        acc = _dot(a_refs[0][...], w_refs[0][...])
        for a_ref, w_ref in zip(a_refs[1:], w_refs[1:]):
            acc = acc + _dot(a_ref[...], w_ref[...])
    xnew = x_ref[...] + acc
    if xo_ref is not None:
        xo_ref[...] = xnew
    xn_ref[...] = _rms(xnew, g_ref[...]).astype(xn_ref.dtype)


def _proj_res_norm(a_list, w, x, g, li, head_major=False, tm=512):
    T = x.shape[0]
    row = lambda i: (i, 0)
    n_a = len(a_list)
    in_specs = []
    for a in a_list:
        if head_major:
            in_specs.append(pl.BlockSpec((C_HEADS, tm, HEAD_I), lambda i: (0, i, 0)))
        else:
            in_specs.append(pl.BlockSpec((tm, a.shape[1]), row))
    for kb, a in enumerate(a_list):
        kdim = C_HEADS * HEAD_I if head_major else a.shape[1]
        in_specs.append(pl.BlockSpec((None, kdim, D_MODEL), lambda i, kb=kb: (li, kb, 0)))
    in_specs += [pl.BlockSpec((tm, D_MODEL), row), pl.BlockSpec((1, D_MODEL), lambda i: (0, 0))]
    scratch = [pltpu.VMEM((tm, D_MODEL), BF16)] if head_major else []
    return pl.pallas_call(
        functools.partial(_proj_res_norm_kernel, n_a=n_a, head_major=head_major, emit_x=True),
        out_shape=(jax.ShapeDtypeStruct((T, D_MODEL), F32), jax.ShapeDtypeStruct((T, D_MODEL), BF16)),
        grid=(T // tm,),
        in_specs=in_specs,
        out_specs=(pl.BlockSpec((tm, D_MODEL), row), pl.BlockSpec((tm, D_MODEL), row)),
        scratch_shapes=scratch,
        compiler_params=_params("parallel"),
        name="proj_res_norm",
    )(*a_list, *([w] * n_a), x, g)


def _odd_in_kernel(xn_ref, w_ref, lbp_ref, o_ref, *, li):
    s = pl.program_id(0)
    h = _dot(xn_ref[...], w_ref[...])
    sg = jax.nn.sigmoid(h)
    p = lbp_ref[...]
    e = jnp.exp(p - jnp.max(p, axis=0, keepdims=True))
    sm = e / jnp.sum(e, axis=0, keepdims=True)
    lb = jnp.clip(jnp.sum(sm[:li + 1], axis=0, keepdims=True) - sm[0:1], 0.0, 1.0)
    f = lb + (1.0 - lb) * sg
    out = jnp.where(s == 1, f, jnp.where(s == 2, h, h * sg))
    for hd in range(C_HEADS):
        o_ref[hd] = out[:, HEAD_I * hd:HEAD_I * (hd + 1)]


def _odd_in(xn, w_in_c, lb_param, li, tm=1024):
    T = xn.shape[0]
    return pl.pallas_call(
        functools.partial(_odd_in_kernel, li=li),
        out_shape=jax.ShapeDtypeStruct((4, C_HEADS, T, HEAD_I), F32),
        grid=(4, T // tm),
        in_specs=[
            pl.BlockSpec((tm, D_MODEL), lambda s, i: (i, 0)),
            pl.BlockSpec((None, D_MODEL, D_MODEL), lambda s, i: (li, 0, s)),
            pl.BlockSpec(lb_param.shape, lambda s, i: (0, 0)),
        ],
        out_specs=pl.BlockSpec((None, C_HEADS, tm, HEAD_I), lambda s, i: (s, 0, i, 0)),
        compiler_params=_params("parallel", "parallel"),
        name="odd_in",
    )(xn, w_in_c, lb_param)


def _gla_kernel(*refs, has_s0):
    if has_s0:
        q_ref, f_ref, v_ref, gs_ref, gon_ref, s0_ref, og_ref, sout_ref, st_ref = refs
    else:
        q_ref, f_ref, v_ref, gs_ref, gon_ref, og_ref, sout_ref, st_ref = refs
    C = GLA_CHUNK
    n = pl.program_id(1)
    last = pl.num_programs(1) - 1

    @pl.when(n == 0)
    def _():
        if has_s0:
            st_ref[...] = s0_ref[...]
        else:
            st_ref[...] = jnp.zeros(st_ref.shape, F32)

    r2 = lax.broadcasted_iota(jnp.int32, (C, C), 0)
    c2 = lax.broadcasted_iota(jnp.int32, (C, C), 1)
    tri = (c2 <= r2).astype(BF16)
    ones = jnp.ones((HEAD_F, LANES), BF16)
    gon = gon_ref[...]
    levels = []
    m = GLA_SUB
    while m < C:
        keep = (r2 // (2 * m) == c2 // (2 * m)) & (r2 % (2 * m) >= m) & (c2 % (2 * m) < m)
        levels.append((m, keep))
        m *= 2
    near = [(c2 == r2 - delta) & (r2 % GLA_SUB >= delta) for delta in range(GLA_SUB)]

    def bcast_row(a, period, r):
        a3 = a.reshape(C // period, period, a.shape[-1])
        return jnp.broadcast_to(a3[:, r:r + 1, :], a3.shape).reshape(C, a.shape[-1])

    def shift_in_group(a, delta):
        if delta == 0:
            return a
        a3 = a.reshape(C // GLA_SUB, GLA_SUB, a.shape[-1])
        return pltpu.roll(a3, delta, 1).reshape(C, a.shape[-1])

    def split3(g):
        g1 = g.astype(BF16)
        r1 = g - g1.astype(F32)
        g2 = r1.astype(BF16)
        return g1, g2, (r1 - g2.astype(F32)).astype(BF16)

    def head_group(hg, row0):
        hs = [hg * GLA_HEADS_PER_ITER + u for u in range(GLA_HEADS_PER_ITER)]
        U = range(len(hs))
        rows = pl.ds(row0, C)
        q = [q_ref[h, rows, :] for h in hs]
        f = [f_ref[h, rows, :] for h in hs]
        v_bf = [v_ref[h, rows, :].astype(BF16) for h in hs]
        fc = [jnp.maximum(x, F_MIN) for x in f]
        kk = [1.0 - x for x in f]
        gp = [split3(jnp.log2(x)) for x in fc]
        b = [_dot(tri, gp[u][0]) + _dot(tri, gp[u][1]) + _dot(tri, gp[u][2]) for u in U]
        st = [st_ref[h] for h in hs]
        o = [_dot((q[u] * jnp.exp2(b[u])).astype(BF16), st[u].astype(BF16)) for u in U]
        b_last = [x[C - 1:C, :] for x in b]
        k_dec = [(kk[u] * jnp.exp2(b_last[u] - b[u])).astype(BF16) for u in U]
        for u in U:
            d_rows = jnp.broadcast_to(jnp.exp2(b_last[u]), (LANES, HEAD_F)).T
            st_ref[hs[u]] = st[u] * d_rows + _dot_tn(k_dec[u], v_bf[u])

        a = [jnp.zeros((C, C), F32) for _ in U]
        for m, keep in levels:
            for u in U:
                d = b[u] - bcast_row(b[u], 2 * m, m - 1)
                qt = (q[u] * jnp.exp2(jnp.minimum(d, 0.0))).astype(BF16)
                kt = (kk[u] * jnp.exp2(jnp.minimum(-d, 0.0))).astype(BF16)
                a[u] = jnp.where(keep, _dot_nt(qt, kt), a[u])
        kd = list(kk)
        for delta in range(GLA_SUB):
            for u in U:
                if delta > 0:
                    kd[u] = shift_in_group(kd[u], 1) * fc[u]
                p = (q[u] * kd[u]).astype(BF16)
                a[u] = jnp.where(near[delta], _dot(p, ones)[:, :C], a[u])
        for u in U:
            y = _rms(o[u] + _dot(a[u].astype(BF16), v_bf[u]), gon) * gs_ref[hs[u], rows, :]
            og_ref[hs[u], rows, :] = y.astype(BF16)
        return row0

    def chunk(c, carry):
        lax.fori_loop(0, C_HEADS // GLA_HEADS_PER_ITER, head_group, pl.multiple_of(c * C, C))
        return carry

    lax.fori_loop(0, q_ref.shape[1] // C, chunk, 0)

    @pl.when(n == last)
    def _():
        sout_ref[...] = st_ref[...]


def _gla(qfvg, g_onorm, s0, li, n_streams, seq):
    T = qfvg.shape[2]
    C = min(GLA_STEP_ROWS, seq)
    nc = seq // C
    has_s0 = s0 is not None
    sec = lambda k: pl.BlockSpec((None, C_HEADS, C, HEAD_F), lambda b, n, k=k: (k, 0, b * nc + n, 0))
    in_specs = [sec(0), sec(1), sec(2), sec(3),
                pl.BlockSpec((None, 1, HEAD_I), lambda b, n: (li, 0, 0))]
    args = [qfvg, qfvg, qfvg, qfvg, g_onorm]
    if has_s0:
        in_specs.append(pl.BlockSpec((None, None, C_HEADS, HEAD_F, HEAD_I), lambda b, n: (li, b, 0, 0, 0)))
        args.append(s0)
    return pl.pallas_call(
        functools.partial(_gla_kernel, has_s0=has_s0),
        out_shape=(jax.ShapeDtypeStruct((C_HEADS, T, HEAD_I), BF16),
                   jax.ShapeDtypeStruct((n_streams, C_HEADS, HEAD_F, HEAD_I), F32)),
        grid=(n_streams, nc),
        in_specs=in_specs,
        out_specs=(pl.BlockSpec((C_HEADS, C, HEAD_I), lambda b, n: (0, b * nc + n, 0)),
                   pl.BlockSpec((None, C_HEADS, HEAD_F, HEAD_I), lambda b, n: (b, 0, 0, 0))),
        scratch_shapes=[pltpu.VMEM((C_HEADS, HEAD_I, HEAD_F), F32)],
        compiler_params=_params("parallel", "arbitrary"),
        name="gla",
    )(*args)


FFN_PAD = SUBLANES
FFN_SUB_ROWS = 256


def _ffn_up_kernel(*refs, tm, seg, from_state, tiles_per_stream):
    if from_state:
        (xn_ref, wa_ref, wb_ref, cwa_ref, cwb_ref, cba_ref, cbb_ref, pa_ref, pb_ref,
         act_ref, hla_ref, hlb_ref, w_ref) = refs
        carry_ref = None
    else:
        (xn_ref, wa_ref, wb_ref, cwa_ref, cwb_ref, cba_ref, cbb_ref,
         act_ref, hla_ref, hlb_ref, w_ref, carry_ref) = refs
        pa_ref = pb_ref = None
    j = pl.program_id(0)
    i = pl.program_id(1)
    sub = min(FFN_SUB_ROWS, tm)
    assert (seg % sub == 0 and not from_state) or (sub % seg == 0 and sub > seg)
    halves = ((0, cwa_ref, cba_ref, pa_ref, hla_ref), (1, cwb_ref, cbb_ref, pb_ref, hlb_ref))

    @pl.when(i == 0)
    def _():
        w_ref[0] = wa_ref[...].astype(BF16)
        w_ref[1] = wb_ref[...].astype(BF16)

    def conv(front, h, cw, cb):
        n = h.shape[0]
        ext = jnp.concatenate([front, h], axis=0)
        return cb + (ext[FFN_PAD - 2:FFN_PAD - 2 + n] * cw[0:1] + ext[FFN_PAD - 1:FFN_PAD - 1 + n] * cw[1:2] +
                     h * cw[2:3])

    fronts = [None, None]
    if not from_state:
        @pl.when(i % tiles_per_stream == 0)
        def _():
            carry_ref[j] = jnp.zeros((2, FFN_PAD, FFN_TN), F32)
        fronts = [carry_ref[j, 0], carry_ref[j, 1]]

    for r0 in range(0, tm, sub):
        convs = []
        for hidx, cw_ref, cb_ref, past_ref, hl_ref in halves:
            h = _dot(xn_ref[r0:r0 + sub, :], w_ref[hidx])
            cw = cw_ref[...]
            cb = cb_ref[...]
            if seg >= sub:
                convs.append([conv(fronts[hidx], h, cw, cb)])
                fronts[hidx] = h[sub - FFN_PAD:sub]
                if (r0 + sub) % seg == 0:
                    hl_ref[(r0 + sub) // seg - 1] = h[sub - 2:sub]
            else:
                pieces = []
                for k in range(sub // seg):
                    s = r0 // seg + k
                    hk = h[k * seg:(k + 1) * seg]
                    front = jnp.concatenate([jnp.zeros((FFN_PAD - 2, FFN_TN), F32), past_ref[s]], axis=0)
                    pieces.append(conv(front, hk, cw, cb))
                    hl_ref[s] = hk[seg - 2:seg]
                convs.append(pieces)
        for n, (a, b) in enumerate(zip(*convs)):
            rows = a.shape[0]
            o = r0 + n * rows
            half = 0.5 * a
            act_ref[o:o + rows, :] = ((half + half * jnp.tanh(half)) * b).astype(BF16)

    if not from_state:
        carry_ref[j, 0] = fronts[0]
        carry_ref[j, 1] = fronts[1]


def _ffn_up(xn, w_up, conv_w, conv_b, conv_past, layer, tm, seg):
    T = xn.shape[0]
    tn = FFN_TN
    nj = D_FF // tn
    nseg = tm // seg
    from_state = conv_past is not None
    wmap_a = lambda j, i: (layer, 0, j)
    wmap_b = lambda j, i: (layer, 0, nj + j)
    in_specs = [
        pl.BlockSpec((tm, D_MODEL), lambda j, i: (i, 0)),
        pl.BlockSpec((None, D_MODEL, tn), wmap_a), pl.BlockSpec((None, D_MODEL, tn), wmap_b),
        pl.BlockSpec((None, CONV_W, tn), wmap_a), pl.BlockSpec((None, CONV_W, tn), wmap_b),
        pl.BlockSpec((None, 1, tn), wmap_a), pl.BlockSpec((None, 1, tn), wmap_b),
    ]
    args = [xn, w_up, w_up, conv_w, conv_w, conv_b, conv_b]
    scratch = [pltpu.VMEM((2, D_MODEL, tn), BF16)]
    if from_state:
        in_specs += [pl.BlockSpec((None, nseg, CONV_W - 1, tn), lambda j, i: (layer, i, 0, j)),
                     pl.BlockSpec((None, nseg, CONV_W - 1, tn), lambda j, i: (layer, i, 0, nj + j))]
        args += [conv_past, conv_past]
        tiles_per_stream = 1
    else:
        scratch.append(pltpu.VMEM((nj, 2, FFN_PAD, tn), F32))
        tiles_per_stream = SEQ // tm
    hl_shape = jax.ShapeDtypeStruct((T // seg, CONV_W - 1, D_FF), F32)
    hl_spec = pl.BlockSpec((nseg, CONV_W - 1, tn), lambda j, i: (i, 0, j))
    return pl.pallas_call(
        functools.partial(_ffn_up_kernel, tm=tm, seg=seg, from_state=from_state,
                          tiles_per_stream=tiles_per_stream),
        out_shape=(jax.ShapeDtypeStruct((T, D_FF), BF16), hl_shape, hl_shape),
        grid=(nj, T // tm),
        in_specs=in_specs,
        out_specs=(pl.BlockSpec((tm, tn), lambda j, i: (i, j)), hl_spec, hl_spec),
        scratch_shapes=scratch,
        compiler_params=_params("arbitrary", "arbitrary"),
        name="ffn_up",
    )(*args)


def _ffn_down_kernel(a_ref, w_ref, x_ref, g_ref, *outs, emit_x):
    if emit_x:
        acc_ref, xn_ref = outs
    else:
        xn_ref, acc_ref = outs
    k = pl.program_id(1)

    @pl.when(k == 0)
    def _():
        acc_ref[...] = x_ref[...]

    acc_ref[...] += _dot(a_ref[...], w_ref[...])

    @pl.when(k == pl.num_programs(1) - 1)
    def _():
        xn_ref[...] = _rms(acc_ref[...], g_ref[...]).astype(xn_ref.dtype)


def _ffn_down(act, w_down, x, g, layer, emit_x, xn_dtype, tm=1024, nk=11):
    T = x.shape[0]
    tk = D_FF // nk
    row = lambda i, k: (i, 0)
    out_shape = [jax.ShapeDtypeStruct((T, D_MODEL), xn_dtype)]
    out_specs = [pl.BlockSpec((tm, D_MODEL), row)]
    scratch = [pltpu.VMEM((tm, D_MODEL), F32)]
    if emit_x:
        out_shape.insert(0, jax.ShapeDtypeStruct((T, D_MODEL), F32))
        out_specs.insert(0, pl.BlockSpec((tm, D_MODEL), row))
        scratch = []
    res = pl.pallas_call(
        functools.partial(_ffn_down_kernel, emit_x=emit_x),
        out_shape=tuple(out_shape),
        grid=(T // tm, nk),
        in_specs=[
            pl.BlockSpec((tm, tk), lambda i, k: (i, k)),
            pl.BlockSpec((None, tk, D_MODEL), lambda i, k: (layer, k, 0)),
            pl.BlockSpec((tm, D_MODEL), row),
            pl.BlockSpec((1, D_MODEL), lambda i, k: (0, 0)),
        ],
        out_specs=tuple(out_specs),
        scratch_shapes=scratch,
        compiler_params=_params("parallel", "arbitrary"),
        name="ffn_down",
    )(act, w_down, x, g)
    return res if emit_x else (None, res[0])


def _rope_tables(pos):
    inv = ROPE_THETA ** (-jnp.arange(0, ROPE_DIM, 2, dtype=F32) / ROPE_DIM)
    ang = pos.astype(F32)[:, None] * inv[None, :]
    cos, sin = jnp.cos(ang), jnp.sin(ang)
    reps = LANES // ROPE_DIM
    return (jnp.tile(jnp.concatenate([cos, cos], axis=-1), (1, reps)),
            jnp.tile(jnp.concatenate([-sin, sin], axis=-1), (1, reps)))


def kernel(x_prompt, x_sample, cache_mla_latent, cache_mla_krope, state_pool, state_hgrn, state_ffn_conv,
           g_mix, g_ffn, g_final, w_in_a, g_qa, w_qb, g_kva, w_uk, w_uv, w_pool, pool_scale, w_out_a,
           w_in_c, lb_param, g_onorm, w_out_c, w_up, conv_w, conv_b, w_down):
    n_even = w_in_a.shape[0]
    n_odd = w_in_c.shape[0]
    Tp, Ts = BATCH * SEQ, DEC_BATCH * DEC_SEQ

    o1, o2, o3 = Q_RANK, Q_RANK + KV_RANK, Q_RANK + KV_RANK + ROPE_DIM
    w_in_e = jnp.concatenate(
        [w_in_a[..., :o2], w_in_a[..., o3:], w_in_a[..., o2:o3],
         jnp.zeros((n_even, D_MODEL, LANES - ROPE_DIM), w_in_a.dtype)], axis=-1).astype(BF16)
    w_qb4 = w_qb.reshape(n_even, Q_RANK, MLA_HEADS, NOPE_DIM + ROPE_DIM)
    w_qb_e = jnp.concatenate(
        [w_qb4[..., :NOPE_DIM].reshape(n_even, Q_RANK, Q_NOPE_ALL),
         w_qb4[..., NOPE_DIM:].reshape(n_even, Q_RANK, MLA_HEADS * ROPE_DIM)], axis=-1).astype(BF16)
    w_ukT = jnp.transpose(w_uk, (0, 2, 3, 1)).astype(BF16)
    w_uvT = jnp.transpose(w_uv, (0, 2, 1, 3)).astype(BF16)
    w_pool_b = w_pool.astype(BF16)
    w_out_a_b = w_out_a.astype(BF16)
    w_in_c_b = w_in_c.astype(BF16)
    w_out_c_b = w_out_c.astype(BF16)
    w_down_b = w_down.astype(BF16)
    g_qa3 = g_qa[:, None, :]
    g_kva3 = g_kva[:, None, :]
    pool_scale3 = pool_scale[:, None, :]
    g_onorm3 = g_onorm[:, None, :]
    conv_b3 = conv_b[:, None, :]

    krope_t = jnp.swapaxes(cache_mla_krope, 2, 3)
    cos_p, sin_p = _rope_tables(jnp.tile(jnp.arange(SEQ), BATCH))
    cos_s, sin_s = _rope_tables(jnp.tile(PAST_LEN + jnp.arange(DEC_SEQ), DEC_BATCH))

    xp = x_prompt.reshape(Tp, D_MODEL)
    xs = x_sample.reshape(Ts, D_MODEL)
    xnp = _norm(xp, g_mix[0:1])
    xns = _norm(xs, g_mix[0:1])

    lat_p, kpe_p, pool_p, hg_p, cv_p = [], [], [], [], []
    lat_s, kpe_s, pool_s, hg_s, cv_s = [], [], [], [], []
    for layer in range(DEPTH):
        li = layer // 2
        g_next = g_ffn[layer:layer + 1]
        if layer % 2 == 0:
            wa = (w_in_e, g_qa3, g_kva3, w_qb_e, w_ukT)
            lat, latb, kpe, kpeb, z, qlat, qpe = _even_in(xnp, *wa, cos_p, sin_p, li)
            y_mla = _attn_prompt(qlat, qpe, latb, kpeb, w_uvT, li)
            y_pool = _pool_prompt(z, w_pool_b, pool_scale3, li)
            xp, xnp = _proj_res_norm([y_mla, y_pool], w_out_a_b, xp, g_next, li)
            lat_p.append(lat.reshape(BATCH, SEQ, KV_RANK))
            kpe_p.append(kpe.reshape(BATCH, SEQ, ROPE_DIM))
            pool_p.append(z.reshape(BATCH, SEQ, POOL_WIDTH)[:, SEQ - POOL_KEEP:])

            lat, latb, kpe, kpeb, z, qlat, qpe = _even_in(xns, *wa, cos_s, sin_s, li)
            y_mla = _attn_sample(qlat, qpe, cache_mla_latent, krope_t, latb, kpeb, w_uvT, li)
            y_pool = _pool_sample(z, state_pool, w_pool_b, pool_scale3, li)
            xs, xns = _proj_res_norm([y_mla, y_pool], w_out_a_b, xs, g_next, li)
            lat_s.append(lat.reshape(DEC_BATCH, DEC_SEQ, KV_RANK))
            kpe_s.append(kpe.reshape(DEC_BATCH, DEC_SEQ, ROPE_DIM))
            pool_s.append(z.reshape(DEC_BATCH, DEC_SEQ, POOL_WIDTH)[:, DEC_SEQ - POOL_KEEP:])
        else:
            qfvg = _odd_in(xnp, w_in_c_b, lb_param, li)
            og, s_fin = _gla(qfvg, g_onorm3, None, li, BATCH, SEQ)
            xp, xnp = _proj_res_norm([og], w_out_c_b, xp, g_next, li, head_major=True)
            hg_p.append(s_fin)

            qfvg = _odd_in(xns, w_in_c_b, lb_param, li)
            og, s_fin = _gla(qfvg, g_onorm3, state_hgrn, li, DEC_BATCH, DEC_SEQ)
            xs, xns = _proj_res_norm([og], w_out_c_b, xs, g_next, li, head_major=True)
            hg_s.append(s_fin)

        final = layer == DEPTH - 1
        g_after = g_final[None, :] if final else g_mix[layer + 1:layer + 2]
        xn_dtype = F32 if final else BF16

        tm_p = SEQ
        act, hla, hlb = _ffn_up(xnp, w_up, conv_w, conv_b3, None, layer, tm=tm_p, seg=tm_p)
        xp, xnp = _ffn_down(act, w_down_b, xp, g_after, layer, not final, xn_dtype)
        hl = jnp.concatenate([hla, hlb], axis=-1).reshape(BATCH, SEQ // tm_p, CONV_W - 1, 2 * D_FF)
        cv_p.append(hl[:, -1])

        act, hla, hlb = _ffn_up(xns, w_up, conv_w, conv_b3, state_ffn_conv, layer, tm=512, seg=DEC_SEQ)
        xs, xns = _ffn_down(act, w_down_b, xs, g_after, layer, not final, xn_dtype)
        cv_s.append(jnp.concatenate([hla, hlb], axis=-1))

    return (xnp.reshape(BATCH, SEQ, D_MODEL), xns.reshape(DEC_BATCH, DEC_SEQ, D_MODEL),
            jnp.stack(lat_p), jnp.stack(kpe_p), jnp.stack(pool_p), jnp.stack(hg_p), jnp.stack(cv_p),
            jnp.stack(lat_s), jnp.stack(kpe_s), jnp.stack(pool_s), jnp.stack(hg_s), jnp.stack(cv_s))
```

```python
import functools

import jax
import jax.numpy as jnp
from jax import lax
from jax.experimental import pallas as pl
from jax.experimental.pallas import tpu as pltpu

D_MODEL = 2048
BATCH = 8
SEQ = 2048
DEPTH = 4
DEC_BATCH = 32
DEC_SEQ = 64
PAST_LEN = 4096
CHUNK = 64
MLA_HEADS = 8
Q_RANK = 512
KV_RANK = 512
NOPE_DIM = 128
ROPE_DIM = 64
V_DIM = 128
ROPE_THETA = 10000.0
MLA_SCALE = (NOPE_DIM + ROPE_DIM) ** -0.5
LOG2_E = 1.4426950408889634
Q_SCALE = MLA_SCALE * LOG2_E
MASK_NEG = -1e30
POOL_WINDOWS = (2, 4, 8, 16)
POOL_WIDTH = D_MODEL - MLA_HEADS * V_DIM
POOL_GROUP_DIM = POOL_WIDTH // len(POOL_WINDOWS)
POOL_KEEP = max(POOL_WINDOWS) - 1
C_HEADS = 16
HEAD_F = 128
HEAD_I = D_MODEL // C_HEADS
F_MIN = 1e-30
D_FF = 5632
CONV_W = 3
EPS = 1e-6

F32 = jnp.float32
BF16 = jnp.bfloat16

LANES = 128
SUBLANES = 8
VMEM_LIMIT = 56 * 1024 * 1024

IN_EVEN_PAD = Q_RANK + KV_RANK + POOL_WIDTH + LANES
Q_NOPE_ALL = MLA_HEADS * NOPE_DIM
GLA_CHUNK = 64
GLA_SUB = SUBLANES
GLA_HEADS_PER_ITER = 16
GLA_STEP_ROWS = 64
FFN_TN = 512


def _params(*sem):
    return pltpu.CompilerParams(dimension_semantics=sem, vmem_limit_bytes=VMEM_LIMIT)


def _dot(a, b):
    return jnp.dot(a, b, preferred_element_type=F32)


def _dot_nt(a, b):
    return lax.dot_general(a, b, (((1,), (1,)), ((), ())), preferred_element_type=F32)


def _dot_tn(a, b):
    return lax.dot_general(a, b, (((0,), (0,)), ((), ())), preferred_element_type=F32)


def _rms(x, g):
    return x * lax.rsqrt(jnp.mean(x * x, axis=-1, keepdims=True) + EPS) * g


def _rope(s, cos, sin, first_half):
    swapped = jnp.where(first_half, pltpu.roll(s, 96, 1), pltpu.roll(s, 32, 1))
    return s * cos + swapped * sin


def _even_in_kernel(x_ref, gin_ref, w_ref, gqa_ref, gkva_ref, wqb_ref, wuk_ref, cos_ref, sin_ref,
                    lat_ref, latb_ref, kpe_ref, kpeb_ref, z_ref, qlat_ref, qpe_ref):
    tm = x_ref.shape[0]
    xn = x_ref[...] if x_ref.dtype == BF16 else _rms(x_ref[...], gin_ref[...]).astype(BF16)
    acc = _dot(xn, w_ref[...])
    o1, o2, o3 = Q_RANK, Q_RANK + KV_RANK, Q_RANK + KV_RANK + POOL_WIDTH
    z_ref[...] = acc[:, o2:o3]
    lat = _rms(acc[:, o1:o2], gkva_ref[...])
    lat_ref[...] = lat
    latb_ref[...] = lat.astype(BF16)
    cos = cos_ref[...]
    sin = sin_ref[...]
    lane = lax.broadcasted_iota(jnp.int32, (tm, LANES), 1)
    first_half = (lane % ROPE_DIM) < (ROPE_DIM // 2)
    kr = _rope(acc[:, o3:o3 + LANES], cos, sin, first_half)
    kpe_ref[...] = kr[:, :ROPE_DIM]
    kpeb_ref[...] = kr[:, :ROPE_DIM].astype(BF16)
    cqn = _rms(acc[:, :o1], gqa_ref[...]).astype(BF16)
    q = _dot(cqn, wqb_ref[...])
    for s in range(MLA_HEADS // 2):
        lo = Q_NOPE_ALL + LANES * s
        r = (_rope(q[:, lo:lo + LANES], cos, sin, first_half) * Q_SCALE).astype(BF16)
        qpe_ref[2 * s] = r[:, :ROPE_DIM]
        qpe_ref[2 * s + 1] = r[:, ROPE_DIM:]
    for h in range(MLA_HEADS):
        ql = _dot(q[:, NOPE_DIM * h:NOPE_DIM * (h + 1)].astype(BF16), wuk_ref[h])
        qlat_ref[h] = (ql * Q_SCALE).astype(BF16)


def _even_in(x, g_in, w_in, g_qa, g_kva, w_qb, w_ukT, cos, sin, li, tm=256):
    T = x.shape[0]
    row = lambda i: (i, 0)
    fix2 = lambda i: (0, 0)
    out_shape = (
        jax.ShapeDtypeStruct((T, KV_RANK), F32), jax.ShapeDtypeStruct((T, KV_RANK), BF16),
        jax.ShapeDtypeStruct((T, ROPE_DIM), F32), jax.ShapeDtypeStruct((T, ROPE_DIM), BF16),
        jax.ShapeDtypeStruct((T, POOL_WIDTH), F32),
        jax.ShapeDtypeStruct((MLA_HEADS, T, KV_RANK), BF16),
        jax.ShapeDtypeStruct((MLA_HEADS, T, ROPE_DIM), BF16),
    )
    return pl.pallas_call(
        _even_in_kernel,
        out_shape=out_shape,
        grid=(T // tm,),
        in_specs=[
            pl.BlockSpec((tm, D_MODEL), row),
            pl.BlockSpec((1, D_MODEL), fix2),
            pl.BlockSpec((None, D_MODEL, IN_EVEN_PAD), lambda i: (li, 0, 0)),
            pl.BlockSpec((None, 1, Q_RANK), lambda i: (li, 0, 0)),
            pl.BlockSpec((None, 1, KV_RANK), lambda i: (li, 0, 0)),
            pl.BlockSpec((None, Q_RANK, MLA_HEADS * (NOPE_DIM + ROPE_DIM)), lambda i: (li, 0, 0)),
            pl.BlockSpec((None, MLA_HEADS, NOPE_DIM, KV_RANK), lambda i: (li, 0, 0, 0)),
            pl.BlockSpec((tm, LANES), row),
            pl.BlockSpec((tm, LANES), row),
        ],
        out_specs=(
            pl.BlockSpec((tm, KV_RANK), row), pl.BlockSpec((tm, KV_RANK), row),
            pl.BlockSpec((tm, ROPE_DIM), row), pl.BlockSpec((tm, ROPE_DIM), row),
            pl.BlockSpec((tm, POOL_WIDTH), row),
            pl.BlockSpec((MLA_HEADS, tm, KV_RANK), lambda i: (0, i, 0)),
            pl.BlockSpec((MLA_HEADS, tm, ROPE_DIM), lambda i: (0, i, 0)),
        ),
        compiler_params=_params("parallel"),
        name="even_in",
    )(x, g_in, w_in, g_qa, g_kva, w_qb, w_ukT, cos, sin)


ATTN_HEAD_GROUP = 4
ATTN_SAMPLE_SUB_KEYS = 512


def _lane_tile(x, width):
    if width <= LANES:
        return x[:, :width]
    return jnp.concatenate([x] * (width // LANES), axis=1)


def _attn_keys_step(qlat_ref, qpe_ref, k, kp, m_ref, l_ref, acc_ref, tq, group, visible=None,
                    kp_transposed=False):
    tk = k.shape[0]
    n = group * tq

    def scores(g):
        ql = qlat_ref[g * group:(g + 1) * group].reshape(n, KV_RANK)
        qp = qpe_ref[g * group:(g + 1) * group].reshape(n, ROPE_DIM)
        s = _dot_nt(ql, k) + (_dot(qp, kp) if kp_transposed else _dot_nt(qp, kp))
        return s if visible is None else jnp.where(visible, s, MASK_NEG)

    s_next = scores(0)
    for g in range(MLA_HEADS // group):
        s = s_next
        if g + 1 < MLA_HEADS // group:
            s_next = scores(g + 1)
        rows = pl.ds(g * n, n)
        m_prev = m_ref[rows, :]
        m_new = jnp.maximum(m_prev, jnp.max(s, axis=1, keepdims=True))
        alpha = jnp.exp2(m_prev - m_new)
        p = jnp.exp2(s - _lane_tile(m_new, tk))
        l_ref[rows, :] = alpha * l_ref[rows, :] + jnp.sum(p, axis=1, keepdims=True)
        acc_ref[rows, :] = _lane_tile(alpha, KV_RANK) * acc_ref[rows, :] + _dot(p.astype(BF16), k)
        m_ref[rows, :] = m_new


def _attn_init(m_ref, l_ref, acc_ref):
    m_ref[...] = jnp.full(m_ref.shape, MASK_NEG, F32)
    l_ref[...] = jnp.zeros(l_ref.shape, F32)
    acc_ref[...] = jnp.zeros(acc_ref.shape, F32)


def _attn_finish(wuv_ref, y_ref, l_ref, acc_ref, tq):
    for h in range(MLA_HEADS):
        rows = pl.ds(h * tq, tq)
        o = (acc_ref[rows, :] / _lane_tile(l_ref[rows, :], KV_RANK)).astype(BF16)
        y_ref[:, V_DIM * h:V_DIM * (h + 1)] = _dot(o, wuv_ref[h]).astype(BF16)


def _attn_prompt_kernel(qi_ref, kj_ref, qlat_ref, qpe_ref, k_ref, kp_ref, wuv_ref, y_ref,
                        m_ref, l_ref, acc_ref, *, tq):
    p = pl.program_id(1)
    i = qi_ref[p]
    j = kj_ref[p]

    @pl.when(j == 0)
    def _():
        _attn_init(m_ref, l_ref, acc_ref)

    @pl.when(j < i)
    def _():
        _attn_keys_step(qlat_ref, qpe_ref, k_ref[...], kp_ref[...], m_ref, l_ref, acc_ref, tq,
                        ATTN_HEAD_GROUP)

    @pl.when(j == i)
    def _():
        n = ATTN_HEAD_GROUP * tq
        row = lax.broadcasted_iota(jnp.int32, (n, tq), 0) % tq
        col = lax.broadcasted_iota(jnp.int32, (n, tq), 1)
        visible = col // CHUNK <= row // CHUNK
        _attn_keys_step(qlat_ref, qpe_ref, k_ref[...], kp_ref[...], m_ref, l_ref, acc_ref, tq,
                        ATTN_HEAD_GROUP, visible)
        _attn_finish(wuv_ref, y_ref, l_ref, acc_ref, tq)


def _attn_prompt(qlat, qpe, latb, kpeb, w_uvT, li, tq=256):
    assert tq % CHUNK == 0
    T = latb.shape[0]
    nq = SEQ // tq
    pairs = [(i, j) for i in range(nq) for j in range(i + 1)]
    q_of = jnp.asarray([i for i, _ in pairs], jnp.int32)
    k_of = jnp.asarray([j for _, j in pairs], jnp.int32)
    qmap = lambda b, p, qi, kj: (0, b * nq + qi[p], 0)
    kmap = lambda b, p, qi, kj: (b * nq + kj[p], 0)
    rows = MLA_HEADS * tq
    return pl.pallas_call(
        functools.partial(_attn_prompt_kernel, tq=tq),
        out_shape=jax.ShapeDtypeStruct((T, MLA_HEADS * V_DIM), BF16),
        grid_spec=pltpu.PrefetchScalarGridSpec(
            num_scalar_prefetch=2,
            grid=(BATCH, len(pairs)),
            in_specs=[
                pl.BlockSpec((MLA_HEADS, tq, KV_RANK), qmap),
                pl.BlockSpec((MLA_HEADS, tq, ROPE_DIM), qmap),
                pl.BlockSpec((tq, KV_RANK), kmap),
                pl.BlockSpec((tq, ROPE_DIM), kmap),
                pl.BlockSpec((None, MLA_HEADS, KV_RANK, V_DIM), lambda b, p, qi, kj: (li, 0, 0, 0)),
            ],
            out_specs=pl.BlockSpec((tq, MLA_HEADS * V_DIM), lambda b, p, qi, kj: (b * nq + qi[p], 0)),
            scratch_shapes=[pltpu.VMEM((rows, LANES), F32), pltpu.VMEM((rows, LANES), F32),
                            pltpu.VMEM((rows, KV_RANK), F32)],
        ),
        compiler_params=_params("parallel", "arbitrary"),
        name="attn_prompt",
    )(q_of, k_of, qlat, qpe, latb, kpeb, w_uvT)


def _attn_sample_kernel(qlat_ref, qpe_ref, ck_ref, ckp_ref, nk_ref, nkp_ref, wuv_ref, y_ref,
                        m_ref, l_ref, acc_ref):
    j = pl.program_id(1)
    last = pl.num_programs(1) - 1

    @pl.when(j == 0)
    def _():
        _attn_init(m_ref, l_ref, acc_ref)

    def step(k, kp, kp_transposed=False):
        _attn_keys_step(qlat_ref, qpe_ref, k, kp, m_ref, l_ref, acc_ref, DEC_SEQ, MLA_HEADS,
                        kp_transposed=kp_transposed)

    @pl.when(j < last)
    def _():
        for c in range(0, ck_ref.shape[0], ATTN_SAMPLE_SUB_KEYS):
            keys = pl.ds(c, ATTN_SAMPLE_SUB_KEYS)
            step(ck_ref[keys, :].astype(BF16), ckp_ref[:, keys].astype(BF16), kp_transposed=True)

    @pl.when(j == last)
    def _():
        step(nk_ref[...], nkp_ref[...])
        _attn_finish(wuv_ref, y_ref, l_ref, acc_ref, DEC_SEQ)


def _attn_sample(qlat, qpe, cache_lat, cache_kpe_t, latb, kpeb, w_uvT, li, tk=1024):
    assert PAST_LEN % CHUNK == 0 and DEC_SEQ <= CHUNK and PAST_LEN % tk == 0
    T = latb.shape[0]
    nc = PAST_LEN // tk
    qmap = lambda b, j: (0, b, 0)
    cmap = lambda b, j: (li, b, jnp.minimum(j, nc - 1), 0)
    rows = MLA_HEADS * DEC_SEQ
    return pl.pallas_call(
        _attn_sample_kernel,
        out_shape=jax.ShapeDtypeStruct((T, MLA_HEADS * V_DIM), BF16),
        grid=(DEC_BATCH, nc + 1),
        in_specs=[
            pl.BlockSpec((MLA_HEADS, DEC_SEQ, KV_RANK), qmap),
            pl.BlockSpec((MLA_HEADS, DEC_SEQ, ROPE_DIM), qmap),
            pl.BlockSpec((None, None, tk, KV_RANK), cmap),
            pl.BlockSpec((None, None, ROPE_DIM, tk), lambda b, j: (li, b, 0, jnp.minimum(j, nc - 1))),
            pl.BlockSpec((DEC_SEQ, KV_RANK), lambda b, j: (b, 0)),
            pl.BlockSpec((DEC_SEQ, ROPE_DIM), lambda b, j: (b, 0)),
            pl.BlockSpec((None, MLA_HEADS, KV_RANK, V_DIM), lambda b, j: (li, 0, 0, 0)),
        ],
        out_specs=pl.BlockSpec((DEC_SEQ, MLA_HEADS * V_DIM), lambda b, j: (b, 0)),
        scratch_shapes=[pltpu.VMEM((rows, LANES), F32), pltpu.VMEM((rows, LANES), F32),
                        pltpu.VMEM((rows, KV_RANK), F32)],
        compiler_params=_params("parallel", "arbitrary"),
        name="attn_sample",
    )(qlat, qpe, cache_lat, cache_kpe_t, latb, kpeb, w_uvT)


POOL_HALO = 2 * SUBLANES


def _pool_kernel(*refs, tp, from_state):
    if from_state:
        z_ref, past_ref, w_ref, scale_ref, y_ref, ext_ref = refs
        ext_ref[0:1, :] = jnp.zeros((1, POOL_WIDTH), F32)
        ext_ref[1:POOL_HALO, :] = past_ref[...]
    else:
        z_ref, w_ref, scale_ref, y_ref, ext_ref = refs
        t = pl.program_id(1)

        @pl.when(t == 0)
        def _():
            ext_ref[0:POOL_HALO, :] = jnp.zeros((POOL_HALO, POOL_WIDTH), F32)

    ext_ref[POOL_HALO:POOL_HALO + tp, :] = z_ref[...]
    for gi, w in enumerate(POOL_WINDOWS):
        lanes = slice(POOL_GROUP_DIM * gi, POOL_GROUP_DIM * (gi + 1))
        tok = ext_ref[POOL_HALO:POOL_HALO + tp, lanes]
        acc = tok
        for d in range(1, w):
            acc = acc + ext_ref[POOL_HALO - d:POOL_HALO - d + tp, lanes]
        if from_state:
            mean = acc / float(w)
        else:
            pos = t * tp + lax.broadcasted_iota(jnp.int32, (tp, 1), 0)
            mean = acc / jnp.minimum(pos + 1, w).astype(F32)
        p = (mean - tok).astype(BF16)
        y_ref[:, lanes] = (_dot(p, w_ref[gi]) * scale_ref[:, lanes]).astype(BF16)
    if not from_state:
        ext_ref[0:POOL_HALO, :] = ext_ref[tp:tp + POOL_HALO, :]


def _pool_prompt(z, w_pool, pool_scale, li, tp=512):
    T = z.shape[0]
    nt = SEQ // tp
    return pl.pallas_call(
        functools.partial(_pool_kernel, tp=tp, from_state=False),
        out_shape=jax.ShapeDtypeStruct((T, POOL_WIDTH), BF16),
        grid=(BATCH, nt),
        in_specs=[
            pl.BlockSpec((tp, POOL_WIDTH), lambda b, t: (b * nt + t, 0)),
            pl.BlockSpec((None, len(POOL_WINDOWS), POOL_GROUP_DIM, POOL_GROUP_DIM), lambda b, t: (li, 0, 0, 0)),
            pl.BlockSpec((None, 1, POOL_WIDTH), lambda b, t: (li, 0, 0)),
        ],
        out_specs=pl.BlockSpec((tp, POOL_WIDTH), lambda b, t: (b * nt + t, 0)),
        scratch_shapes=[pltpu.VMEM((POOL_HALO + tp, POOL_WIDTH), F32)],
        compiler_params=_params("arbitrary", "arbitrary"),
        name="pool_prompt",
    )(z, w_pool, pool_scale)


def _pool_sample(z, state_pool, w_pool, pool_scale, li):
    T = z.shape[0]
    return pl.pallas_call(
        functools.partial(_pool_kernel, tp=DEC_SEQ, from_state=True),
        out_shape=jax.ShapeDtypeStruct((T, POOL_WIDTH), BF16),
        grid=(DEC_BATCH,),
        in_specs=[
            pl.BlockSpec((DEC_SEQ, POOL_WIDTH), lambda b: (b, 0)),
            pl.BlockSpec((None, None, POOL_KEEP, POOL_WIDTH), lambda b: (li, b, 0, 0)),
            pl.BlockSpec((None, len(POOL_WINDOWS), POOL_GROUP_DIM, POOL_GROUP_DIM), lambda b: (li, 0, 0, 0)),
            pl.BlockSpec((None, 1, POOL_WIDTH), lambda b: (li, 0, 0)),
        ],
        out_specs=pl.BlockSpec((DEC_SEQ, POOL_WIDTH), lambda b: (b, 0)),
        scratch_shapes=[pltpu.VMEM((POOL_HALO + DEC_SEQ, POOL_WIDTH), F32)],
        compiler_params=_params("parallel"),
        name="pool_sample",
    )(z, state_pool, w_pool, pool_scale)


def _proj_res_norm_kernel(*refs, n_a, head_major, emit_x):
    a_refs = refs[:n_a]
    w_refs = refs[n_a:2 * n_a]
    x_ref, g_ref = refs[2 * n_a:2 * n_a + 2]
    outs = refs[2 * n_a + 2:]
    if head_major:
        (a_ref,), (w_ref,) = a_refs, w_refs
        xo_ref, xn_ref, cat_ref = outs
        for h in range(C_HEADS):
            cat_ref[:, HEAD_I * h:HEAD_I * (h + 1)] = a_ref[h]
        acc = _dot(cat_ref[...], w_ref[...])
    else:
        xo_ref, xn_ref = outs if emit_x else (None, outs[0])
        acc = _dot(a_refs[0][...], w_refs[0][...])
        for a_ref, w_ref in zip(a_refs[1:], w_refs[1:]):
            acc = acc + _dot(a_ref[...], w_ref[...])
    xnew = x_ref[...] + acc
    if xo_ref is not None:
        xo_ref[...] = xnew
    xn_ref[...] = _rms(xnew, g_ref[...]).astype(xn_ref.dtype)


def _proj_res_norm(a_list, w, x, g, li, head_major=False, tm=512):
    T = x.shape[0]
    row = lambda i: (i, 0)
    n_a = len(a_list)
    in_specs = []
    for a in a_list:
        if head_major:
            in_specs.append(pl.BlockSpec((C_HEADS, tm, HEAD_I), lambda i: (0, i, 0)))
        else:
            in_specs.append(pl.BlockSpec((tm, a.shape[1]), row))
    for kb, a in enumerate(a_list):
        kdim = C_HEADS * HEAD_I if head_major else a.shape[1]
        in_specs.append(pl.BlockSpec((None, kdim, D_MODEL), lambda i, kb=kb: (li, kb, 0)))
    in_specs += [pl.BlockSpec((tm, D_MODEL), row), pl.BlockSpec((1, D_MODEL), lambda i: (0, 0))]
    scratch = [pltpu.VMEM((tm, D_MODEL), BF16)] if head_major else []
    return pl.pallas_call(
        functools.partial(_proj_res_norm_kernel, n_a=n_a, head_major=head_major, emit_x=True),
        out_shape=(jax.ShapeDtypeStruct((T, D_MODEL), F32), jax.ShapeDtypeStruct((T, D_MODEL), BF16)),
        grid=(T // tm,),
        in_specs=in_specs,
        out_specs=(pl.BlockSpec((tm, D_MODEL), row), pl.BlockSpec((tm, D_MODEL), row)),
        scratch_shapes=scratch,
        compiler_params=_params("parallel"),
        name="proj_res_norm",
    )(*a_list, *([w] * n_a), x, g)


def _odd_in_kernel(xn_ref, w_ref, lbp_ref, o_ref, *, li):
    s = pl.program_id(0)
    h = _dot(xn_ref[...], w_ref[...])
    sg = jax.nn.sigmoid(h)
    p = lbp_ref[...]
    e = jnp.exp(p - jnp.max(p, axis=0, keepdims=True))
    sm = e / jnp.sum(e, axis=0, keepdims=True)
    lb = jnp.clip(jnp.sum(sm[:li + 1], axis=0, keepdims=True) - sm[0:1], 0.0, 1.0)
    f = lb + (1.0 - lb) * sg
    out = jnp.where(s == 1, f, jnp.where(s == 2, h, h * sg))
    for hd in range(C_HEADS):
        o_ref[hd] = out[:, HEAD_I * hd:HEAD_I * (hd + 1)]


def _odd_in(xn, w_in_c, lb_param, li, tm=1024):
    T = xn.shape[0]
    return pl.pallas_call(
        functools.partial(_odd_in_kernel, li=li),
        out_shape=jax.ShapeDtypeStruct((4, C_HEADS, T, HEAD_I), F32),
        grid=(4, T // tm),
        in_specs=[
            pl.BlockSpec((tm, D_MODEL), lambda s, i: (i, 0)),
            pl.BlockSpec((None, D_MODEL, D_MODEL), lambda s, i: (li, 0, s)),
            pl.BlockSpec(lb_param.shape, lambda s, i: (0, 0)),
        ],
        out_specs=pl.BlockSpec((None, C_HEADS, tm, HEAD_I), lambda s, i: (s, 0, i, 0)),
        compiler_params=_params("parallel", "parallel"),
        name="odd_in",
    )(xn, w_in_c, lb_param)


def _gla_kernel(*refs, has_s0):
    if has_s0:
        q_ref, f_ref, v_ref, gs_ref, gon_ref, s0_ref, og_ref, sout_ref, st_ref = refs
    else:
        q_ref, f_ref, v_ref, gs_ref, gon_ref, og_ref, sout_ref, st_ref = refs
    C = GLA_CHUNK
    n = pl.program_id(1)
    last = pl.num_programs(1) - 1

    @pl.when(n == 0)
    def _():
        if has_s0:
            st_ref[...] = s0_ref[...]
        else:
            st_ref[...] = jnp.zeros(st_ref.shape, F32)

    r2 = lax.broadcasted_iota(jnp.int32, (C, C), 0)
    c2 = lax.broadcasted_iota(jnp.int32, (C, C), 1)
    tri = (c2 <= r2).astype(BF16)
    ones = jnp.ones((HEAD_F, LANES), BF16)
    gon = gon_ref[...]
    levels = []
    m = GLA_SUB
    while m < C:
        keep = (r2 // (2 * m) == c2 // (2 * m)) & (r2 % (2 * m) >= m) & (c2 % (2 * m) < m)
        levels.append((m, keep))
        m *= 2
    near = [(c2 == r2 - delta) & (r2 % GLA_SUB >= delta) for delta in range(GLA_SUB)]

    def bcast_row(a, period, r):
        a3 = a.reshape(C // period, period, a.shape[-1])
        return jnp.broadcast_to(a3[:, r:r + 1, :], a3.shape).reshape(C, a.shape[-1])

    def shift_in_group(a, delta):
        if delta == 0:
            return a
        a3 = a.reshape(C // GLA_SUB, GLA_SUB, a.shape[-1])
        return pltpu.roll(a3, delta, 1).reshape(C, a.shape[-1])

    def split3(g):
        g1 = g.astype(BF16)
        r1 = g - g1.astype(F32)
        g2 = r1.astype(BF16)
        return g1, g2, (r1 - g2.astype(F32)).astype(BF16)

    def head_group(hg, row0):
        hs = [hg * GLA_HEADS_PER_ITER + u for u in range(GLA_HEADS_PER_ITER)]
        U = range(len(hs))
        rows = pl.ds(row0, C)
        q = [q_ref[h, rows, :] for h in hs]
        f = [f_ref[h, rows, :] for h in hs]
        v_bf = [v_ref[h, rows, :].astype(BF16) for h in hs]
        fc = [jnp.maximum(x, F_MIN) for x in f]
        kk = [1.0 - x for x in f]
        gp = [split3(jnp.log2(x)) for x in fc]
        b = [_dot(tri, gp[u][0]) + _dot(tri, gp[u][1]) + _dot(tri, gp[u][2]) for u in U]
        st = [st_ref[h] for h in hs]
        o = [_dot((q[u] * jnp.exp2(b[u])).astype(BF16), st[u].astype(BF16)) for u in U]
        b_last = [x[C - 1:C, :] for x in b]
        k_dec = [(kk[u] * jnp.exp2(b_last[u] - b[u])).astype(BF16) for u in U]
        for u in U:
            d_rows = jnp.broadcast_to(jnp.exp2(b_last[u]), (LANES, HEAD_F)).T
            st_ref[hs[u]] = st[u] * d_rows + _dot_tn(k_dec[u], v_bf[u])

        a = [jnp.zeros((C, C), F32) for _ in U]
        for m, keep in levels:
            for u in U:
                d = b[u] - bcast_row(b[u], 2 * m, m - 1)
                qt = (q[u] * jnp.exp2(jnp.minimum(d, 0.0))).astype(BF16)
                kt = (kk[u] * jnp.exp2(jnp.minimum(-d, 0.0))).astype(BF16)
                a[u] = jnp.where(keep, _dot_nt(qt, kt), a[u])
        kd = list(kk)
        for delta in range(GLA_SUB):
            for u in U:
                if delta > 0:
                    kd[u] = shift_in_group(kd[u], 1) * fc[u]
                p = (q[u] * kd[u]).astype(BF16)
                a[u] = jnp.where(near[delta], _dot(p, ones)[:, :C], a[u])
        for u in U:
            y = _rms(o[u] + _dot(a[u].astype(BF16), v_bf[u]), gon) * gs_ref[hs[u], rows, :]
            og_ref[hs[u], rows, :] = y.astype(BF16)
        return row0

    def chunk(c, carry):
        lax.fori_loop(0, C_HEADS // GLA_HEADS_PER_ITER, head_group, pl.multiple_of(c * C, C))
        return carry

    lax.fori_loop(0, q_ref.shape[1] // C, chunk, 0)

    @pl.when(n == last)
    def _():
        sout_ref[...] = st_ref[...]


def _gla(qfvg, g_onorm, s0, li, n_streams, seq):
    T = qfvg.shape[2]
    C = min(GLA_STEP_ROWS, seq)
    nc = seq // C
    has_s0 = s0 is not None
    sec = lambda k: pl.BlockSpec((None, C_HEADS, C, HEAD_F), lambda b, n, k=k: (k, 0, b * nc + n, 0))
    in_specs = [sec(0), sec(1), sec(2), sec(3),
                pl.BlockSpec((None, 1, HEAD_I), lambda b, n: (li, 0, 0))]
    args = [qfvg, qfvg, qfvg, qfvg, g_onorm]
    if has_s0:
        in_specs.append(pl.BlockSpec((None, None, C_HEADS, HEAD_F, HEAD_I), lambda b, n: (li, b, 0, 0, 0)))
        args.append(s0)
    return pl.pallas_call(
        functools.partial(_gla_kernel, has_s0=has_s0),
        out_shape=(jax.ShapeDtypeStruct((C_HEADS, T, HEAD_I), BF16),
                   jax.ShapeDtypeStruct((n_streams, C_HEADS, HEAD_F, HEAD_I), F32)),
        grid=(n_streams, nc),
        in_specs=in_specs,
        out_specs=(pl.BlockSpec((C_HEADS, C, HEAD_I), lambda b, n: (0, b * nc + n, 0)),
                   pl.BlockSpec((None, C_HEADS, HEAD_F, HEAD_I), lambda b, n: (b, 0, 0, 0))),
        scratch_shapes=[pltpu.VMEM((C_HEADS, HEAD_I, HEAD_F), F32)],
        compiler_params=_params("parallel", "arbitrary"),
        name="gla",
    )(*args)


FFN_PAD = SUBLANES
FFN_SUB_ROWS = 256


def _ffn_up_kernel(*refs, tm, seg, from_state, tiles_per_stream):
    if from_state:
        (xn_ref, wa_ref, wb_ref, cwa_ref, cwb_ref, cba_ref, cbb_ref, pa_ref, pb_ref,
         act_ref, hla_ref, hlb_ref, w_ref) = refs
        carry_ref = None
    else:
        (xn_ref, wa_ref, wb_ref, cwa_ref, cwb_ref, cba_ref, cbb_ref,
         act_ref, hla_ref, hlb_ref, w_ref, carry_ref) = refs
        pa_ref = pb_ref = None
    j = pl.program_id(0)
    i = pl.program_id(1)
    sub = min(FFN_SUB_ROWS, tm)
    assert (seg % sub == 0 and not from_state) or (sub % seg == 0 and sub > seg)
    halves = ((0, cwa_ref, cba_ref, pa_ref, hla_ref), (1, cwb_ref, cbb_ref, pb_ref, hlb_ref))

    @pl.when(i == 0)
    def _():
        w_ref[0] = wa_ref[...].astype(BF16)
        w_ref[1] = wb_ref[...].astype(BF16)

    def conv(front, h, cw, cb):
        n = h.shape[0]
        ext = jnp.concatenate([front, h], axis=0)
        return cb + (ext[FFN_PAD - 2:FFN_PAD - 2 + n] * cw[0:1] + ext[FFN_PAD - 1:FFN_PAD - 1 + n] * cw[1:2] +
                     h * cw[2:3])

    fronts = [None, None]
    if not from_state:
        @pl.when(i % tiles_per_stream == 0)
        def _():
            carry_ref[j] = jnp.zeros((2, FFN_PAD, FFN_TN), F32)
        fronts = [carry_ref[j, 0], carry_ref[j, 1]]

    for r0 in range(0, tm, sub):
        convs = []
        for hidx, cw_ref, cb_ref, past_ref, hl_ref in halves:
            h = _dot(xn_ref[r0:r0 + sub, :], w_ref[hidx])
            cw = cw_ref[...]
            cb = cb_ref[...]
            if seg >= sub:
                convs.append([conv(fronts[hidx], h, cw, cb)])
                fronts[hidx] = h[sub - FFN_PAD:sub]
                if (r0 + sub) % seg == 0:
                    hl_ref[(r0 + sub) // seg - 1] = h[sub - 2:sub]
            else:
                pieces = []
                for k in range(sub // seg):
                    s = r0 // seg + k
                    hk = h[k * seg:(k + 1) * seg]
                    front = jnp.concatenate([jnp.zeros((FFN_PAD - 2, FFN_TN), F32), past_ref[s]], axis=0)
                    pieces.append(conv(front, hk, cw, cb))
                    hl_ref[s] = hk[seg - 2:seg]
                convs.append(pieces)
        for n, (a, b) in enumerate(zip(*convs)):
            rows = a.shape[0]
            o = r0 + n * rows
            half = 0.5 * a
            act_ref[o:o + rows, :] = ((half + half * jnp.tanh(half)) * b).astype(BF16)

    if not from_state:
        carry_ref[j, 0] = fronts[0]
        carry_ref[j, 1] = fronts[1]


def _ffn_up(xn, w_up, conv_w, conv_b, conv_past, layer, tm, seg):
    T = xn.shape[0]
    tn = FFN_TN
    nj = D_FF // tn
    nseg = tm // seg
    from_state = conv_past is not None
    wmap_a = lambda j, i: (layer, 0, j)
    wmap_b = lambda j, i: (layer, 0, nj + j)
    in_specs = [
        pl.BlockSpec((tm, D_MODEL), lambda j, i: (i, 0)),
        pl.BlockSpec((None, D_MODEL, tn), wmap_a), pl.BlockSpec((None, D_MODEL, tn), wmap_b),
        pl.BlockSpec((None, CONV_W, tn), wmap_a), pl.BlockSpec((None, CONV_W, tn), wmap_b),
        pl.BlockSpec((None, 1, tn), wmap_a), pl.BlockSpec((None, 1, tn), wmap_b),
    ]
    args = [xn, w_up, w_up, conv_w, conv_w, conv_b, conv_b]
    scratch = [pltpu.VMEM((2, D_MODEL, tn), BF16)]
    if from_state:
        in_specs += [pl.BlockSpec((None, nseg, CONV_W - 1, tn), lambda j, i: (layer, i, 0, j)),
                     pl.BlockSpec((None, nseg, CONV_W - 1, tn), lambda j, i: (layer, i, 0, nj + j))]
        args += [conv_past, conv_past]
        tiles_per_stream = 1
    else:
        scratch.append(pltpu.VMEM((nj, 2, FFN_PAD, tn), F32))
        tiles_per_stream = SEQ // tm
    hl_shape = jax.ShapeDtypeStruct((T // seg, CONV_W - 1, D_FF), F32)
    hl_spec = pl.BlockSpec((nseg, CONV_W - 1, tn), lambda j, i: (i, 0, j))
    return pl.pallas_call(
        functools.partial(_ffn_up_kernel, tm=tm, seg=seg, from_state=from_state,
                          tiles_per_stream=tiles_per_stream),
        out_shape=(jax.ShapeDtypeStruct((T, D_FF), BF16), hl_shape, hl_shape),
        grid=(nj, T // tm),
        in_specs=in_specs,
        out_specs=(pl.BlockSpec((tm, tn), lambda j, i: (i, j)), hl_spec, hl_spec),
        scratch_shapes=scratch,
        compiler_params=_params("arbitrary", "arbitrary"),
        name="ffn_up",
    )(*args)


def _ffn_down_kernel(a_ref, w_ref, x_ref, g_ref, *outs, emit_x):
    if emit_x:
        acc_ref, xn_ref = outs
    else:
        xn_ref, acc_ref = outs
    k = pl.program_id(1)

    @pl.when(k == 0)
    def _():
        acc_ref[...] = x_ref[...]

    acc_ref[...] += _dot(a_ref[...], w_ref[...])

    @pl.when(k == pl.num_programs(1) - 1)
    def _():
        xn_ref[...] = _rms(acc_ref[...], g_ref[...]).astype(xn_ref.dtype)


def _ffn_down(act, w_down, x, g, layer, emit_x, xn_dtype, tm=1024, nk=11):
    T = x.shape[0]
    tk = D_FF // nk
    row = lambda i, k: (i, 0)
    out_shape = [jax.ShapeDtypeStruct((T, D_MODEL), xn_dtype)]
    out_specs = [pl.BlockSpec((tm, D_MODEL), row)]
    scratch = [pltpu.VMEM((tm, D_MODEL), F32)]
    if emit_x:
        out_shape.insert(0, jax.ShapeDtypeStruct((T, D_MODEL), F32))
        out_specs.insert(0, pl.BlockSpec((tm, D_MODEL), row))
        scratch = []
    res = pl.pallas_call(
        functools.partial(_ffn_down_kernel, emit_x=emit_x),
        out_shape=tuple(out_shape),
        grid=(T // tm, nk),
        in_specs=[
            pl.BlockSpec((tm, tk), lambda i, k: (i, k)),
            pl.BlockSpec((None, tk, D_MODEL), lambda i, k: (layer, k, 0)),
            pl.BlockSpec((tm, D_MODEL), row),
            pl.BlockSpec((1, D_MODEL), lambda i, k: (0, 0)),
        ],
        out_specs=tuple(out_specs),
        scratch_shapes=scratch,
        compiler_params=_params("parallel", "arbitrary"),
        name="ffn_down",
    )(act, w_down, x, g)
    return res if emit_x else (None, res[0])


def _rope_tables(pos):
    inv = ROPE_THETA ** (-jnp.arange(0, ROPE_DIM, 2, dtype=F32) / ROPE_DIM)
    ang = pos.astype(F32)[:, None] * inv[None, :]
    cos, sin = jnp.cos(ang), jnp.sin(ang)
    reps = LANES // ROPE_DIM
    return (jnp.tile(jnp.concatenate([cos, cos], axis=-1), (1, reps)),
            jnp.tile(jnp.concatenate([-sin, sin], axis=-1), (1, reps)))


def kernel(x_prompt, x_sample, cache_mla_latent, cache_mla_krope, state_pool, state_hgrn, state_ffn_conv,
           g_mix, g_ffn, g_final, w_in_a, g_qa, w_qb, g_kva, w_uk, w_uv, w_pool, pool_scale, w_out_a,
           w_in_c, lb_param, g_onorm, w_out_c, w_up, conv_w, conv_b, w_down):
    n_even = w_in_a.shape[0]
    n_odd = w_in_c.shape[0]
    Tp, Ts = BATCH * SEQ, DEC_BATCH * DEC_SEQ

    o1, o2, o3 = Q_RANK, Q_RANK + KV_RANK, Q_RANK + KV_RANK + ROPE_DIM
    w_in_e = jnp.concatenate(
        [w_in_a[..., :o2], w_in_a[..., o3:], w_in_a[..., o2:o3],
         jnp.zeros((n_even, D_MODEL, LANES - ROPE_DIM), w_in_a.dtype)], axis=-1).astype(BF16)
    w_qb4 = w_qb.reshape(n_even, Q_RANK, MLA_HEADS, NOPE_DIM + ROPE_DIM)
    w_qb_e = jnp.concatenate(
        [w_qb4[..., :NOPE_DIM].reshape(n_even, Q_RANK, Q_NOPE_ALL),
         w_qb4[..., NOPE_DIM:].reshape(n_even, Q_RANK, MLA_HEADS * ROPE_DIM)], axis=-1).astype(BF16)
    w_ukT = jnp.transpose(w_uk, (0, 2, 3, 1)).astype(BF16)
    w_uvT = jnp.transpose(w_uv, (0, 2, 1, 3)).astype(BF16)
    w_pool_b = w_pool.astype(BF16)
    w_out_a_b = w_out_a.astype(BF16)
    w_in_c_b = w_in_c.astype(BF16)
    w_out_c_b = w_out_c.astype(BF16)
    w_down_b = w_down.astype(BF16)
    g_qa3 = g_qa[:, None, :]
    g_kva3 = g_kva[:, None, :]
    pool_scale3 = pool_scale[:, None, :]
    g_onorm3 = g_onorm[:, None, :]
    conv_b3 = conv_b[:, None, :]

    krope_t = jnp.swapaxes(cache_mla_krope, 2, 3)
    cos_p, sin_p = _rope_tables(jnp.tile(jnp.arange(SEQ), BATCH))
    cos_s, sin_s = _rope_tables(jnp.tile(PAST_LEN + jnp.arange(DEC_SEQ), DEC_BATCH))

    xp = x_prompt.reshape(Tp, D_MODEL)
    xs = x_sample.reshape(Ts, D_MODEL)
    xnp, xns = xp, xs

    lat_p, kpe_p, pool_p, hg_p, cv_p = [], [], [], [], []
    lat_s, kpe_s, pool_s, hg_s, cv_s = [], [], [], [], []
    for layer in range(DEPTH):
        li = layer // 2
        g_next = g_ffn[layer:layer + 1]
        if layer % 2 == 0:
            wa = (w_in_e, g_qa3, g_kva3, w_qb_e, w_ukT)
            lat, latb, kpe, kpeb, z, qlat, qpe = _even_in(xnp, g_mix[layer:layer + 1], *wa, cos_p, sin_p, li)
            y_mla = _attn_prompt(qlat, qpe, latb, kpeb, w_uvT, li)
            y_pool = _pool_prompt(z, w_pool_b, pool_scale3, li)
            xp, xnp = _proj_res_norm([y_mla, y_pool], w_out_a_b, xp, g_next, li)
            lat_p.append(lat.reshape(BATCH, SEQ, KV_RANK))
            kpe_p.append(kpe.reshape(BATCH, SEQ, ROPE_DIM))
            pool_p.append(z.reshape(BATCH, SEQ, POOL_WIDTH)[:, SEQ - POOL_KEEP:])

            lat, latb, kpe, kpeb, z, qlat, qpe = _even_in(xns, g_mix[layer:layer + 1], *wa, cos_s, sin_s, li)
            y_mla = _attn_sample(qlat, qpe, cache_mla_latent, krope_t, latb, kpeb, w_uvT, li)
            y_pool = _pool_sample(z, state_pool, w_pool_b, pool_scale3, li)
            xs, xns = _proj_res_norm([y_mla, y_pool], w_out_a_b, xs, g_next, li)
            lat_s.append(lat.reshape(DEC_BATCH, DEC_SEQ, KV_RANK))
            kpe_s.append(kpe.reshape(DEC_BATCH, DEC_SEQ, ROPE_DIM))
            pool_s.append(z.reshape(DEC_BATCH, DEC_SEQ, POOL_WIDTH)[:, DEC_SEQ - POOL_KEEP:])
        else:
            qfvg = _odd_in(xnp, w_in_c_b, lb_param, li)
            og, s_fin = _gla(qfvg, g_onorm3, None, li, BATCH, SEQ)
            xp, xnp = _proj_res_norm([og], w_out_c_b, xp, g_next, li, head_major=True)
            hg_p.append(s_fin)

            qfvg = _odd_in(xns, w_in_c_b, lb_param, li)
            og, s_fin = _gla(qfvg, g_onorm3, state_hgrn, li, DEC_BATCH, DEC_SEQ)
            xs, xns = _proj_res_norm([og], w_out_c_b, xs, g_next, li, head_major=True)
            hg_s.append(s_fin)

        final = layer == DEPTH - 1
        g_after = g_final[None, :] if final else g_mix[layer + 1:layer + 2]
        xn_dtype = F32 if final else BF16

        tm_p = SEQ
        act, hla, hlb = _ffn_up(xnp, w_up, conv_w, conv_b3, None, layer, tm=tm_p, seg=tm_p)
        xp, xnp = _ffn_down(act, w_down_b, xp, g_after, layer, not final, xn_dtype)
        hl = jnp.concatenate([hla, hlb], axis=-1).reshape(BATCH, SEQ // tm_p, CONV_W - 1, 2 * D_FF)
        cv_p.append(hl[:, -1])

        act, hla, hlb = _ffn_up(xns, w_up, conv_w, conv_b3, state_ffn_conv, layer, tm=512, seg=DEC_SEQ)
        xs, xns = _ffn_down(act, w_down_b, xs, g_after, layer, not final, xn_dtype)
        cv_s.append(jnp.concatenate([hla, hlb], axis=-1))

    return (xnp.reshape(BATCH, SEQ, D_MODEL), xns.reshape(DEC_BATCH, DEC_SEQ, D_MODEL),
            jnp.stack(lat_p), jnp.stack(kpe_p), jnp.stack(pool_p), jnp.stack(hg_p), jnp.stack(cv_p),
            jnp.stack(lat_s), jnp.stack(kpe_s), jnp.stack(pool_s), jnp.stack(hg_s), jnp.stack(cv_s))
```

```python
import functools

import jax
import jax.numpy as jnp
from jax import lax
from jax.experimental import pallas as pl
from jax.experimental.pallas import tpu as pltpu

D_MODEL = 2048
BATCH = 8
SEQ = 2048
DEPTH = 4
DEC_BATCH = 32
DEC_SEQ = 64
PAST_LEN = 4096
CHUNK = 64
MLA_HEADS = 8
Q_RANK = 512
KV_RANK = 512
NOPE_DIM = 128
ROPE_DIM = 64
V_DIM = 128
ROPE_THETA = 10000.0
MLA_SCALE = (NOPE_DIM + ROPE_DIM) ** -0.5
LOG2_E = 1.4426950408889634
Q_SCALE = MLA_SCALE * LOG2_E
MASK_NEG = -1e30
POOL_WINDOWS = (2, 4, 8, 16)
POOL_WIDTH = D_MODEL - MLA_HEADS * V_DIM
POOL_GROUP_DIM = POOL_WIDTH // len(POOL_WINDOWS)
POOL_KEEP = max(POOL_WINDOWS) - 1
C_HEADS = 16
HEAD_F = 128
HEAD_I = D_MODEL // C_HEADS
F_MIN = 1e-30
D_FF = 5632
CONV_W = 3
EPS = 1e-6

F32 = jnp.float32
BF16 = jnp.bfloat16

LANES = 128
SUBLANES = 8
VMEM_LIMIT = 56 * 1024 * 1024

IN_EVEN_PAD = Q_RANK + KV_RANK + POOL_WIDTH + LANES
Q_NOPE_ALL = MLA_HEADS * NOPE_DIM
GLA_CHUNK = 64
GLA_SUB = SUBLANES
GLA_HEADS_PER_ITER = 16
GLA_STEP_ROWS = 64
FFN_TN = 512


def _params(*sem):
    return pltpu.CompilerParams(dimension_semantics=sem, vmem_limit_bytes=VMEM_LIMIT)


def _dot(a, b):
    return jnp.dot(a, b, preferred_element_type=F32)


def _dot_nt(a, b):
    return lax.dot_general(a, b, (((1,), (1,)), ((), ())), preferred_element_type=F32)


def _dot_tn(a, b):
    return lax.dot_general(a, b, (((0,), (0,)), ((), ())), preferred_element_type=F32)


def _rms(x, g):
    return x * lax.rsqrt(jnp.mean(x * x, axis=-1, keepdims=True) + EPS) * g


def _rope(s, cos, sin, first_half):
    swapped = jnp.where(first_half, pltpu.roll(s, 96, 1), pltpu.roll(s, 32, 1))
    return s * cos + swapped * sin


def _even_in_kernel(x_ref, gin_ref, w_ref, gqa_ref, gkva_ref, wqb_ref, wuk_ref, cos_ref, sin_ref,
                    lat_ref, latb_ref, kpe_ref, kpeb_ref, z_ref, qlat_ref, qpe_ref):
    tm = x_ref.shape[0]
    xn = x_ref[...] if x_ref.dtype == BF16 else _rms(x_ref[...], gin_ref[...]).astype(BF16)
    acc = _dot(xn, w_ref[...])
    o1, o2, o3 = Q_RANK, Q_RANK + KV_RANK, Q_RANK + KV_RANK + POOL_WIDTH
    z_ref[...] = acc[:, o2:o3]
    lat = _rms(acc[:, o1:o2], gkva_ref[...])
    lat_ref[...] = lat
    latb_ref[...] = lat.astype(BF16)
    cos = cos_ref[...]
    sin = sin_ref[...]
    lane = lax.broadcasted_iota(jnp.int32, (tm, LANES), 1)
    first_half = (lane % ROPE_DIM) < (ROPE_DIM // 2)
    kr = _rope(acc[:, o3:o3 + LANES], cos, sin, first_half)
    kpe_ref[...] = kr[:, :ROPE_DIM]
    kpeb_ref[...] = kr[:, :ROPE_DIM].astype(BF16)
    cqn = _rms(acc[:, :o1], gqa_ref[...]).astype(BF16)
    q = _dot(cqn, wqb_ref[...])
    for s in range(MLA_HEADS // 2):
        lo = Q_NOPE_ALL + LANES * s
        r = (_rope(q[:, lo:lo + LANES], cos, sin, first_half) * Q_SCALE).astype(BF16)
        qpe_ref[2 * s] = r[:, :ROPE_DIM]
        qpe_ref[2 * s + 1] = r[:, ROPE_DIM:]
    for h in range(MLA_HEADS):
        ql = _dot(q[:, NOPE_DIM * h:NOPE_DIM * (h + 1)].astype(BF16), wuk_ref[h])
        qlat_ref[h] = (ql * Q_SCALE).astype(BF16)


def _even_in(x, g_in, w_in, g_qa, g_kva, w_qb, w_ukT, cos, sin, li, tm=256):
    T = x.shape[0]
    row = lambda i: (i, 0)
    fix2 = lambda i: (0, 0)
    out_shape = (
        jax.ShapeDtypeStruct((T, KV_RANK), F32), jax.ShapeDtypeStruct((T, KV_RANK), BF16),
        jax.ShapeDtypeStruct((T, ROPE_DIM), F32), jax.ShapeDtypeStruct((T, ROPE_DIM), BF16),
        jax.ShapeDtypeStruct((T, POOL_WIDTH), F32),
        jax.ShapeDtypeStruct((MLA_HEADS, T, KV_RANK), BF16),
        jax.ShapeDtypeStruct((MLA_HEADS, T, ROPE_DIM), BF16),
    )
    return pl.pallas_call(
        _even_in_kernel,
        out_shape=out_shape,
        grid=(T // tm,),
        in_specs=[
            pl.BlockSpec((tm, D_MODEL), row),
            pl.BlockSpec((1, D_MODEL), fix2),
            pl.BlockSpec((None, D_MODEL, IN_EVEN_PAD), lambda i: (li, 0, 0)),
            pl.BlockSpec((None, 1, Q_RANK), lambda i: (li, 0, 0)),
            pl.BlockSpec((None, 1, KV_RANK), lambda i: (li, 0, 0)),
            pl.BlockSpec((None, Q_RANK, MLA_HEADS * (NOPE_DIM + ROPE_DIM)), lambda i: (li, 0, 0)),
            pl.BlockSpec((None, MLA_HEADS, NOPE_DIM, KV_RANK), lambda i: (li, 0, 0, 0)),
            pl.BlockSpec((tm, LANES), row),
            pl.BlockSpec((tm, LANES), row),
        ],
        out_specs=(
            pl.BlockSpec((tm, KV_RANK), row), pl.BlockSpec((tm, KV_RANK), row),
            pl.BlockSpec((tm, ROPE_DIM), row), pl.BlockSpec((tm, ROPE_DIM), row),
            pl.BlockSpec((tm, POOL_WIDTH), row),
            pl.BlockSpec((MLA_HEADS, tm, KV_RANK), lambda i: (0, i, 0)),
            pl.BlockSpec((MLA_HEADS, tm, ROPE_DIM), lambda i: (0, i, 0)),
        ),
        compiler_params=_params("parallel"),
        name="even_in",
    )(x, g_in, w_in, g_qa, g_kva, w_qb, w_ukT, cos, sin)


ATTN_HEAD_GROUP = 4
ATTN_SAMPLE_SUB_KEYS = 512


def _lane_tile(x, width):
    if width <= LANES:
        return x[:, :width]
    return jnp.concatenate([x] * (width // LANES), axis=1)


def _attn_keys_step(qlat_ref, qpe_ref, k, kp, m_ref, l_ref, acc_ref, tq, group, visible=None,
                    kp_transposed=False):
    tk = k.shape[0]
    n = group * tq

    def scores(g):
        ql = qlat_ref[g * group:(g + 1) * group].reshape(n, KV_RANK)
        qp = qpe_ref[g * group:(g + 1) * group].reshape(n, ROPE_DIM)
        s = _dot_nt(ql, k) + (_dot(qp, kp) if kp_transposed else _dot_nt(qp, kp))
        return s if visible is None else jnp.where(visible, s, MASK_NEG)

    s_next = scores(0)
    for g in range(MLA_HEADS // group):
        s = s_next
        if g + 1 < MLA_HEADS // group:
            s_next = scores(g + 1)
        rows = pl.ds(g * n, n)
        m_prev = m_ref[rows, :]
        m_new = jnp.maximum(m_prev, jnp.max(s, axis=1, keepdims=True))
        alpha = jnp.exp2(m_prev - m_new)
        p = jnp.exp2(s - _lane_tile(m_new, tk))
        l_ref[rows, :] = alpha * l_ref[rows, :] + jnp.sum(p, axis=1, keepdims=True)
        acc_ref[rows, :] = _lane_tile(alpha, KV_RANK) * acc_ref[rows, :] + _dot(p.astype(BF16), k)
        m_ref[rows, :] = m_new


def _attn_init(m_ref, l_ref, acc_ref):
    m_ref[...] = jnp.full(m_ref.shape, MASK_NEG, F32)
    l_ref[...] = jnp.zeros(l_ref.shape, F32)
    acc_ref[...] = jnp.zeros(acc_ref.shape, F32)


def _attn_finish(wuv_ref, y_ref, l_ref, acc_ref, tq):
    for h in range(MLA_HEADS):
        rows = pl.ds(h * tq, tq)
        o = (acc_ref[rows, :] / _lane_tile(l_ref[rows, :], KV_RANK)).astype(BF16)
        y_ref[:, V_DIM * h:V_DIM * (h + 1)] = _dot(o, wuv_ref[h]).astype(BF16)


def _attn_prompt_kernel(qi_ref, kj_ref, qlat_ref, qpe_ref, k_ref, kp_ref, wuv_ref, y_ref,
                        m_ref, l_ref, acc_ref, *, tq):
    p = pl.program_id(1)
    i = qi_ref[p]
    j = kj_ref[p]

    @pl.when(j == 0)
    def _():
        _attn_init(m_ref, l_ref, acc_ref)

    @pl.when(j < i)
    def _():
        _attn_keys_step(qlat_ref, qpe_ref, k_ref[...], kp_ref[...], m_ref, l_ref, acc_ref, tq,
                        ATTN_HEAD_GROUP)

    @pl.when(j == i)
    def _():
        n = ATTN_HEAD_GROUP * tq
        row = lax.broadcasted_iota(jnp.int32, (n, tq), 0) % tq
        col = lax.broadcasted_iota(jnp.int32, (n, tq), 1)
        visible = col // CHUNK <= row // CHUNK
        _attn_keys_step(qlat_ref, qpe_ref, k_ref[...], kp_ref[...], m_ref, l_ref, acc_ref, tq,
                        ATTN_HEAD_GROUP, visible)
        _attn_finish(wuv_ref, y_ref, l_ref, acc_ref, tq)


def _attn_prompt(qlat, qpe, latb, kpeb, w_uvT, li, tq=256):
    assert tq % CHUNK == 0
    T = latb.shape[0]
    nq = SEQ // tq
    pairs = [(i, j) for i in range(nq) for j in range(i + 1)]
    q_of = jnp.asarray([i for i, _ in pairs], jnp.int32)
    k_of = jnp.asarray([j for _, j in pairs], jnp.int32)
    qmap = lambda b, p, qi, kj: (0, b * nq + qi[p], 0)
    kmap = lambda b, p, qi, kj: (b * nq + kj[p], 0)
    rows = MLA_HEADS * tq
    return pl.pallas_call(
        functools.partial(_attn_prompt_kernel, tq=tq),
        out_shape=jax.ShapeDtypeStruct((T, MLA_HEADS * V_DIM), BF16),
        grid_spec=pltpu.PrefetchScalarGridSpec(
            num_scalar_prefetch=2,
            grid=(BATCH, len(pairs)),
            in_specs=[
                pl.BlockSpec((MLA_HEADS, tq, KV_RANK), qmap),
                pl.BlockSpec((MLA_HEADS, tq, ROPE_DIM), qmap),
                pl.BlockSpec((tq, KV_RANK), kmap),
                pl.BlockSpec((tq, ROPE_DIM), kmap),
                pl.BlockSpec((None, MLA_HEADS, KV_RANK, V_DIM), lambda b, p, qi, kj: (li, 0, 0, 0)),
            ],
            out_specs=pl.BlockSpec((tq, MLA_HEADS * V_DIM), lambda b, p, qi, kj: (b * nq + qi[p], 0)),
            scratch_shapes=[pltpu.VMEM((rows, LANES), F32), pltpu.VMEM((rows, LANES), F32),
                            pltpu.VMEM((rows, KV_RANK), F32)],
        ),
        compiler_params=_params("parallel", "arbitrary"),
        name="attn_prompt",
    )(q_of, k_of, qlat, qpe, latb, kpeb, w_uvT)


def _attn_sample_kernel(qlat_ref, qpe_ref, ck_ref, ckp_ref, nk_ref, nkp_ref, wuv_ref, y_ref,
                        m_ref, l_ref, acc_ref):
    j = pl.program_id(1)
    last = pl.num_programs(1) - 1

    @pl.when(j == 0)
    def _():
        _attn_init(m_ref, l_ref, acc_ref)

    def step(k, kp, kp_transposed=False):
        _attn_keys_step(qlat_ref, qpe_ref, k, kp, m_ref, l_ref, acc_ref, DEC_SEQ, MLA_HEADS,
                        kp_transposed=kp_transposed)

    @pl.when(j < last)
    def _():
        for c in range(0, ck_ref.shape[0], ATTN_SAMPLE_SUB_KEYS):
            keys = pl.ds(c, ATTN_SAMPLE_SUB_KEYS)
            step(ck_ref[keys, :].astype(BF16), ckp_ref[:, keys].astype(BF16), kp_transposed=True)

    @pl.when(j == last)
    def _():
        step(nk_ref[...], nkp_ref[...])
        _attn_finish(wuv_ref, y_ref, l_ref, acc_ref, DEC_SEQ)


def _attn_sample(qlat, qpe, cache_lat, cache_kpe_t, latb, kpeb, w_uvT, li, tk=1024):
    assert PAST_LEN % CHUNK == 0 and DEC_SEQ <= CHUNK and PAST_LEN % tk == 0
    T = latb.shape[0]
    nc = PAST_LEN // tk
    qmap = lambda b, j: (0, b, 0)
    cmap = lambda b, j: (li, b, jnp.minimum(j, nc - 1), 0)
    rows = MLA_HEADS * DEC_SEQ
    return pl.pallas_call(
        _attn_sample_kernel,
        out_shape=jax.ShapeDtypeStruct((T, MLA_HEADS * V_DIM), BF16),
        grid=(DEC_BATCH, nc + 1),
        in_specs=[
            pl.BlockSpec((MLA_HEADS, DEC_SEQ, KV_RANK), qmap),
            pl.BlockSpec((MLA_HEADS, DEC_SEQ, ROPE_DIM), qmap),
            pl.BlockSpec((None, None, tk, KV_RANK), cmap),
            pl.BlockSpec((None, None, ROPE_DIM, tk), lambda b, j: (li, b, 0, jnp.minimum(j, nc - 1))),
            pl.BlockSpec((DEC_SEQ, KV_RANK), lambda b, j: (b, 0)),
            pl.BlockSpec((DEC_SEQ, ROPE_DIM), lambda b, j: (b, 0)),
            pl.BlockSpec((None, MLA_HEADS, KV_RANK, V_DIM), lambda b, j: (li, 0, 0, 0)),
        ],
        out_specs=pl.BlockSpec((DEC_SEQ, MLA_HEADS * V_DIM), lambda b, j: (b, 0)),
        scratch_shapes=[pltpu.VMEM((rows, LANES), F32), pltpu.VMEM((rows, LANES), F32),
                        pltpu.VMEM((rows, KV_RANK), F32)],
        compiler_params=_params("parallel", "arbitrary"),
        name="attn_sample",
    )(qlat, qpe, cache_lat, cache_kpe_t, latb, kpeb, w_uvT)


POOL_HALO = 2 * SUBLANES


def _pool_kernel(*refs, tp, from_state):
    if from_state:
        z_ref, past_ref, w_ref, scale_ref, y_ref, ext_ref = refs
        ext_ref[0:1, :] = jnp.zeros((1, POOL_WIDTH), F32)
        ext_ref[1:POOL_HALO, :] = past_ref[...]
    else:
        z_ref, w_ref, scale_ref, y_ref, ext_ref = refs
        t = pl.program_id(1)

        @pl.when(t == 0)
        def _():
            ext_ref[0:POOL_HALO, :] = jnp.zeros((POOL_HALO, POOL_WIDTH), F32)

    ext_ref[POOL_HALO:POOL_HALO + tp, :] = z_ref[...]
    for gi, w in enumerate(POOL_WINDOWS):
        lanes = slice(POOL_GROUP_DIM * gi, POOL_GROUP_DIM * (gi + 1))
        tok = ext_ref[POOL_HALO:POOL_HALO + tp, lanes]
        acc = tok
        for d in range(1, w):
            acc = acc + ext_ref[POOL_HALO - d:POOL_HALO - d + tp, lanes]
        if from_state:
            mean = acc / float(w)
        else:
            pos = t * tp + lax.broadcasted_iota(jnp.int32, (tp, 1), 0)
            mean = acc / jnp.minimum(pos + 1, w).astype(F32)
        p = (mean - tok).astype(BF16)
        y_ref[:, lanes] = (_dot(p, w_ref[gi]) * scale_ref[:, lanes]).astype(BF16)
    if not from_state:
        ext_ref[0:POOL_HALO, :] = ext_ref[tp:tp + POOL_HALO, :]


def _pool_prompt(z, w_pool, pool_scale, li, tp=512):
    T = z.shape[0]
    nt = SEQ // tp
    return pl.pallas_call(
        functools.partial(_pool_kernel, tp=tp, from_state=False),
        out_shape=jax.ShapeDtypeStruct((T, POOL_WIDTH), BF16),
        grid=(BATCH, nt),
        in_specs=[
            pl.BlockSpec((tp, POOL_WIDTH), lambda b, t: (b * nt + t, 0)),
            pl.BlockSpec((None, len(POOL_WINDOWS), POOL_GROUP_DIM, POOL_GROUP_DIM), lambda b, t: (li, 0, 0, 0)),
            pl.BlockSpec((None, 1, POOL_WIDTH), lambda b, t: (li, 0, 0)),
        ],
        out_specs=pl.BlockSpec((tp, POOL_WIDTH), lambda b, t: (b * nt + t, 0)),
        scratch_shapes=[pltpu.VMEM((POOL_HALO + tp, POOL_WIDTH), F32)],
        compiler_params=_params("arbitrary", "arbitrary"),
        name="pool_prompt",
    )(z, w_pool, pool_scale)


def _pool_sample(z, state_pool, w_pool, pool_scale, li):
    T = z.shape[0]
    return pl.pallas_call(
        functools.partial(_pool_kernel, tp=DEC_SEQ, from_state=True),
        out_shape=jax.ShapeDtypeStruct((T, POOL_WIDTH), BF16),
        grid=(DEC_BATCH,),
        in_specs=[
            pl.BlockSpec((DEC_SEQ, POOL_WIDTH), lambda b: (b, 0)),
            pl.BlockSpec((None, None, POOL_KEEP, POOL_WIDTH), lambda b: (li, b, 0, 0)),
            pl.BlockSpec((None, len(POOL_WINDOWS), POOL_GROUP_DIM, POOL_GROUP_DIM), lambda b: (li, 0, 0, 0)),
            pl.BlockSpec((None, 1, POOL_WIDTH), lambda b: (li, 0, 0)),
        ],
        out_specs=pl.BlockSpec((DEC_SEQ, POOL_WIDTH), lambda b: (b, 0)),
        scratch_shapes=[pltpu.VMEM((POOL_HALO + DEC_SEQ, POOL_WIDTH), F32)],
        compiler_params=_params("parallel"),
        name="pool_sample",
    )(z, state_pool, w_pool, pool_scale)


def _proj_res_norm_kernel(*refs, n_a, head_major, emit_x):
    a_refs = refs[:n_a]
    w_refs = refs[n_a:2 * n_a]
    x_ref, g_ref = refs[2 * n_a:2 * n_a + 2]
    outs = refs[2 * n_a + 2:]
    if head_major:
        (a_ref,), (w_ref,) = a_refs, w_refs
        xo_ref, xn_ref, cat_ref = outs
        for h in range(C_HEADS):
            cat_ref[:, HEAD_I * h:HEAD_I * (h + 1)] = a_ref[h]
        acc = _dot(cat_ref[...], w_ref[...])
    else:
        xo_ref, xn_ref = outs if emit_x else (None, outs[0])
        acc = _dot(a_refs[0][...], w_refs[0][...])
        for a_ref, w_ref in zip(a_refs[1:], w_refs[1:]):
            acc = acc + _dot(a_ref[...], w_ref[...])
    xnew = x_ref[...] + acc
    if xo_ref is not None:
        xo_ref[...] = xnew
    xn_ref[...] = _rms(xnew, g_ref[...]).astype(xn_ref.dtype)


def _proj_res_norm(a_list, w, x, g, li, head_major=False, tm=512):
    T = x.shape[0]
    row = lambda i: (i, 0)
    n_a = len(a_list)
    in_specs = []
    for a in a_list:
        if head_major:
            in_specs.append(pl.BlockSpec((C_HEADS, tm, HEAD_I), lambda i: (0, i, 0)))
        else:
            in_specs.append(pl.BlockSpec((tm, a.shape[1]), row))
    for kb, a in enumerate(a_list):
        kdim = C_HEADS * HEAD_I if head_major else a.shape[1]
        in_specs.append(pl.BlockSpec((None, kdim, D_MODEL), lambda i, kb=kb: (li, kb, 0)))
    in_specs += [pl.BlockSpec((tm, D_MODEL), row), pl.BlockSpec((1, D_MODEL), lambda i: (0, 0))]
    scratch = [pltpu.VMEM((tm, D_MODEL), BF16)] if head_major else []
    return pl.pallas_call(
        functools.partial(_proj_res_norm_kernel, n_a=n_a, head_major=head_major, emit_x=True),
        out_shape=(jax.ShapeDtypeStruct((T, D_MODEL), F32), jax.ShapeDtypeStruct((T, D_MODEL), BF16)),
        grid=(T // tm,),
        in_specs=in_specs,
        out_specs=(pl.BlockSpec((tm, D_MODEL), row), pl.BlockSpec((tm, D_MODEL), row)),
        scratch_shapes=scratch,
        compiler_params=_params("parallel"),
        name="proj_res_norm",
    )(*a_list, *([w] * n_a), x, g)


def _odd_in_kernel(xn_ref, w_ref, lbp_ref, o_ref, *, li):
    s = pl.program_id(0)

    def emit(epilogue):
        out = epilogue(_dot(xn_ref[...], w_ref[...]))
        for hd in range(C_HEADS):
            o_ref[hd] = out[:, HEAD_I * hd:HEAD_I * (hd + 1)]

    def silu(h):
        half = 0.5 * h
        return half + half * jnp.tanh(half)

    def forget(h):
        p = lbp_ref[...]
        e = jnp.exp(p - jnp.max(p, axis=0, keepdims=True))
        sm = e / jnp.sum(e, axis=0, keepdims=True)
        lb = jnp.clip(jnp.sum(sm[:li + 1], axis=0, keepdims=True) - sm[0:1], 0.0, 1.0)
        gain = 0.5 * (1.0 - lb)
        return (lb + gain) + gain * jnp.tanh(0.5 * h)

    for sec, epilogue in enumerate((silu, forget, lambda h: h, silu)):
        @pl.when(s == sec)
        def _(epilogue=epilogue):
            emit(epilogue)


def _odd_in(xn, w_in_c, lb_param, li, tm=1024):
    T = xn.shape[0]
    return pl.pallas_call(
        functools.partial(_odd_in_kernel, li=li),
        out_shape=jax.ShapeDtypeStruct((4, C_HEADS, T, HEAD_I), F32),
        grid=(4, T // tm),
        in_specs=[
            pl.BlockSpec((tm, D_MODEL), lambda s, i: (i, 0)),
            pl.BlockSpec((None, D_MODEL, D_MODEL), lambda s, i: (li, 0, s)),
            pl.BlockSpec(lb_param.shape, lambda s, i: (0, 0)),
        ],
        out_specs=pl.BlockSpec((None, C_HEADS, tm, HEAD_I), lambda s, i: (s, 0, i, 0)),
        compiler_params=_params("parallel", "parallel"),
        name="odd_in",
    )(xn, w_in_c, lb_param)


def _gla_kernel(*refs, has_s0):
    if has_s0:
        q_ref, f_ref, v_ref, gs_ref, gon_ref, s0_ref, og_ref, sout_ref, st_ref = refs
    else:
        q_ref, f_ref, v_ref, gs_ref, gon_ref, og_ref, sout_ref, st_ref = refs
    C = GLA_CHUNK
    n = pl.program_id(1)
    last = pl.num_programs(1) - 1

    @pl.when(n == 0)
    def _():
        if has_s0:
            st_ref[...] = s0_ref[...]
        else:
            st_ref[...] = jnp.zeros(st_ref.shape, F32)

    r2 = lax.broadcasted_iota(jnp.int32, (C, C), 0)
    c2 = lax.broadcasted_iota(jnp.int32, (C, C), 1)
    tri = (c2 <= r2).astype(BF16)
    ones = jnp.ones((HEAD_F, LANES), BF16)
    gon = gon_ref[...]
    levels = []
    m = GLA_SUB
    while m < C:
        keep = (r2 // (2 * m) == c2 // (2 * m)) & (r2 % (2 * m) >= m) & (c2 % (2 * m) < m)
        levels.append((m, keep))
        m *= 2
    near = [(c2 == r2 - delta) & (r2 % GLA_SUB >= delta) for delta in range(GLA_SUB)]

    def bcast_row(a, period, r):
        a3 = a.reshape(C // period, period, a.shape[-1])
        return jnp.broadcast_to(a3[:, r:r + 1, :], a3.shape).reshape(C, a.shape[-1])

    def shift_in_group(a, delta):
        if delta == 0:
            return a
        a3 = a.reshape(C // GLA_SUB, GLA_SUB, a.shape[-1])
        return pltpu.roll(a3, delta, 1).reshape(C, a.shape[-1])

    def split3(g):
        g1 = g.astype(BF16)
        r1 = g - g1.astype(F32)
        g2 = r1.astype(BF16)
        return g1, g2, (r1 - g2.astype(F32)).astype(BF16)

    def head_group(hg, row0):
        hs = [hg * GLA_HEADS_PER_ITER + u for u in range(GLA_HEADS_PER_ITER)]
        U = range(len(hs))
        rows = pl.ds(row0, C)
        q = [q_ref[h, rows, :] for h in hs]
        f = [f_ref[h, rows, :] for h in hs]
        v_bf = [v_ref[h, rows, :].astype(BF16) for h in hs]
        fc = [jnp.maximum(x, F_MIN) for x in f]
        kk = [1.0 - x for x in f]
        gp = [split3(jnp.log2(x)) for x in fc]
        b = [_dot(tri, gp[u][0]) + _dot(tri, gp[u][1]) + _dot(tri, gp[u][2]) for u in U]
        st = [st_ref[h] for h in hs]
        o = [_dot((q[u] * jnp.exp2(b[u])).astype(BF16), st[u].astype(BF16)) for u in U]
        b_last = [x[C - 1:C, :] for x in b]
        k_dec = [(kk[u] * jnp.exp2(b_last[u] - b[u])).astype(BF16) for u in U]
        for u in U:
            d_rows = jnp.broadcast_to(jnp.exp2(b_last[u]), (LANES, HEAD_F)).T
            st_ref[hs[u]] = st[u] * d_rows + _dot_tn(k_dec[u], v_bf[u])

        a = [jnp.zeros((C, C), F32) for _ in U]
        for m, keep in levels:
            for u in U:
                d = b[u] - bcast_row(b[u], 2 * m, m - 1)
                qt = (q[u] * jnp.exp2(jnp.minimum(d, 0.0))).astype(BF16)
                kt = (kk[u] * jnp.exp2(jnp.minimum(-d, 0.0))).astype(BF16)
                a[u] = jnp.where(keep, _dot_nt(qt, kt), a[u])
        kd = list(kk)
        for delta in range(GLA_SUB):
            for u in U:
                if delta > 0:
                    kd[u] = shift_in_group(kd[u], 1) * fc[u]
                p = (q[u] * kd[u]).astype(BF16)
                a[u] = jnp.where(near[delta], _dot(p, ones)[:, :C], a[u])
        for u in U:
            y = _rms(o[u] + _dot(a[u].astype(BF16), v_bf[u]), gon) * gs_ref[hs[u], rows, :]
            og_ref[hs[u], rows, :] = y.astype(BF16)
        return row0

    def chunk(c, carry):
        lax.fori_loop(0, C_HEADS // GLA_HEADS_PER_ITER, head_group, pl.multiple_of(c * C, C))
        return carry

    lax.fori_loop(0, q_ref.shape[1] // C, chunk, 0)

    @pl.when(n == last)
    def _():
        sout_ref[...] = st_ref[...]


def _gla(qfvg, g_onorm, s0, li, n_streams, seq):
    T = qfvg.shape[2]
    C = min(GLA_STEP_ROWS, seq)
    nc = seq // C
    has_s0 = s0 is not None
    sec = lambda k: pl.BlockSpec((None, C_HEADS, C, HEAD_F), lambda b, n, k=k: (k, 0, b * nc + n, 0))
    in_specs = [sec(0), sec(1), sec(2), sec(3),
                pl.BlockSpec((None, 1, HEAD_I), lambda b, n: (li, 0, 0))]
    args = [qfvg, qfvg, qfvg, qfvg, g_onorm]
    if has_s0:
        in_specs.append(pl.BlockSpec((None, None, C_HEADS, HEAD_F, HEAD_I), lambda b, n: (li, b, 0, 0, 0)))
        args.append(s0)
    return pl.pallas_call(
        functools.partial(_gla_kernel, has_s0=has_s0),
        out_shape=(jax.ShapeDtypeStruct((C_HEADS, T, HEAD_I), BF16),
                   jax.ShapeDtypeStruct((n_streams, C_HEADS, HEAD_F, HEAD_I), F32)),
        grid=(n_streams, nc),
        in_specs=in_specs,
        out_specs=(pl.BlockSpec((C_HEADS, C, HEAD_I), lambda b, n: (0, b * nc + n, 0)),
                   pl.BlockSpec((None, C_HEADS, HEAD_F, HEAD_I), lambda b, n: (b, 0, 0, 0))),
        scratch_shapes=[pltpu.VMEM((C_HEADS, HEAD_I, HEAD_F), F32)],
        compiler_params=_params("parallel", "arbitrary"),
        name="gla",
    )(*args)


FFN_PAD = SUBLANES
FFN_SUB_ROWS = 256


def _ffn_up_kernel(*refs, tm, seg, from_state, tiles_per_stream):
    if from_state:
        (xn_ref, wa_ref, wb_ref, cwa_ref, cwb_ref, cba_ref, cbb_ref, pa_ref, pb_ref,
         act_ref, hla_ref, hlb_ref, w_ref) = refs
        carry_ref = None
    else:
        (xn_ref, wa_ref, wb_ref, cwa_ref, cwb_ref, cba_ref, cbb_ref,
         act_ref, hla_ref, hlb_ref, w_ref, carry_ref) = refs
        pa_ref = pb_ref = None
    j = pl.program_id(0)
    i = pl.program_id(1)
    sub = min(FFN_SUB_ROWS, tm)
    assert (seg % sub == 0 and not from_state) or (sub % seg == 0 and sub > seg)
    halves = ((0, cwa_ref, cba_ref, pa_ref, hla_ref), (1, cwb_ref, cbb_ref, pb_ref, hlb_ref))

    @pl.when(i == 0)
    def _():
        w_ref[0] = wa_ref[...].astype(BF16)
        w_ref[1] = wb_ref[...].astype(BF16)

    def conv(front, h, cw, cb):
        n = h.shape[0]
        ext = jnp.concatenate([front, h], axis=0)
        return cb + (ext[FFN_PAD - 2:FFN_PAD - 2 + n] * cw[0:1] + ext[FFN_PAD - 1:FFN_PAD - 1 + n] * cw[1:2] +
                     h * cw[2:3])

    fronts = [None, None]
    if not from_state:
        @pl.when(i % tiles_per_stream == 0)
        def _():
            carry_ref[j] = jnp.zeros((2, FFN_PAD, FFN_TN), F32)
        fronts = [carry_ref[j, 0], carry_ref[j, 1]]

    for r0 in range(0, tm, sub):
        convs = []
        for hidx, cw_ref, cb_ref, past_ref, hl_ref in halves:
            h = _dot(xn_ref[r0:r0 + sub, :], w_ref[hidx])
            cw = cw_ref[...]
            cb = cb_ref[...]
            if seg >= sub:
                convs.append([conv(fronts[hidx], h, cw, cb)])
                fronts[hidx] = h[sub - FFN_PAD:sub]
                if (r0 + sub) % seg == 0:
                    hl_ref[(r0 + sub) // seg - 1] = h[sub - 2:sub]
            else:
                pieces = []
                for k in range(sub // seg):
                    s = r0 // seg + k
                    hk = h[k * seg:(k + 1) * seg]
                    front = jnp.concatenate([jnp.zeros((FFN_PAD - 2, FFN_TN), F32), past_ref[s]], axis=0)
                    pieces.append(conv(front, hk, cw, cb))
                    hl_ref[s] = hk[seg - 2:seg]
                convs.append(pieces)
        for n, (a, b) in enumerate(zip(*convs)):
            rows = a.shape[0]
            o = r0 + n * rows
            half = 0.5 * a
            act_ref[o:o + rows, :] = ((half + half * jnp.tanh(half)) * b).astype(BF16)

    if not from_state:
        carry_ref[j, 0] = fronts[0]
        carry_ref[j, 1] = fronts[1]


def _ffn_up(xn, w_up, conv_w, conv_b, conv_past, layer, tm, seg):
    T = xn.shape[0]
    tn = FFN_TN
    nj = D_FF // tn
    nseg = tm // seg
    from_state = conv_past is not None
    wmap_a = lambda j, i: (layer, 0, j)
    wmap_b = lambda j, i: (layer, 0, nj + j)
    in_specs = [
        pl.BlockSpec((tm, D_MODEL), lambda j, i: (i, 0)),
        pl.BlockSpec((None, D_MODEL, tn), wmap_a), pl.BlockSpec((None, D_MODEL, tn), wmap_b),
        pl.BlockSpec((None, CONV_W, tn), wmap_a), pl.BlockSpec((None, CONV_W, tn), wmap_b),
        pl.BlockSpec((None, 1, tn), wmap_a), pl.BlockSpec((None, 1, tn), wmap_b),
    ]
    args = [xn, w_up, w_up, conv_w, conv_w, conv_b, conv_b]
    scratch = [pltpu.VMEM((2, D_MODEL, tn), BF16)]
    if from_state:
        in_specs += [pl.BlockSpec((None, nseg, CONV_W - 1, tn), lambda j, i: (layer, i, 0, j)),
                     pl.BlockSpec((None, nseg, CONV_W - 1, tn), lambda j, i: (layer, i, 0, nj + j))]
        args += [conv_past, conv_past]
        tiles_per_stream = 1
    else:
        scratch.append(pltpu.VMEM((nj, 2, FFN_PAD, tn), F32))
        tiles_per_stream = SEQ // tm
    hl_shape = jax.ShapeDtypeStruct((T // seg, CONV_W - 1, D_FF), F32)
    hl_spec = pl.BlockSpec((nseg, CONV_W - 1, tn), lambda j, i: (i, 0, j))
    return pl.pallas_call(
        functools.partial(_ffn_up_kernel, tm=tm, seg=seg, from_state=from_state,
                          tiles_per_stream=tiles_per_stream),
        out_shape=(jax.ShapeDtypeStruct((T, D_FF), BF16), hl_shape, hl_shape),
        grid=(nj, T // tm),
        in_specs=in_specs,
        out_specs=(pl.BlockSpec((tm, tn), lambda j, i: (i, j)), hl_spec, hl_spec),
        scratch_shapes=scratch,
        compiler_params=_params("arbitrary", "arbitrary"),
        name="ffn_up",
    )(*args)


def _ffn_down_kernel(a_ref, w_ref, x_ref, g_ref, *outs, emit_x):
    if emit_x:
        acc_ref, xn_ref = outs
    else:
        xn_ref, acc_ref = outs
    k = pl.program_id(1)

    @pl.when(k == 0)
    def _():
        acc_ref[...] = x_ref[...]

    acc_ref[...] += _dot(a_ref[...], w_ref[...])

    @pl.when(k == pl.num_programs(1) - 1)
    def _():
        xn_ref[...] = _rms(acc_ref[...], g_ref[...]).astype(xn_ref.dtype)


def _ffn_down(act, w_down, x, g, layer, emit_x, xn_dtype, tm=1024, nk=11):
    T = x.shape[0]
    tk = D_FF // nk
    row = lambda i, k: (i, 0)
    out_shape = [jax.ShapeDtypeStruct((T, D_MODEL), xn_dtype)]
    out_specs = [pl.BlockSpec((tm, D_MODEL), row)]
    scratch = [pltpu.VMEM((tm, D_MODEL), F32)]
    if emit_x:
        out_shape.insert(0, jax.ShapeDtypeStruct((T, D_MODEL), F32))
        out_specs.insert(0, pl.BlockSpec((tm, D_MODEL), row))
        scratch = []
    res = pl.pallas_call(
        functools.partial(_ffn_down_kernel, emit_x=emit_x),
        out_shape=tuple(out_shape),
        grid=(T // tm, nk),
        in_specs=[
            pl.BlockSpec((tm, tk), lambda i, k: (i, k)),
            pl.BlockSpec((None, tk, D_MODEL), lambda i, k: (layer, k, 0)),
            pl.BlockSpec((tm, D_MODEL), row),
            pl.BlockSpec((1, D_MODEL), lambda i, k: (0, 0)),
        ],
        out_specs=tuple(out_specs),
        scratch_shapes=scratch,
        compiler_params=_params("parallel", "arbitrary"),
        name="ffn_down",
    )(act, w_down, x, g)
    return res if emit_x else (None, res[0])


def _rope_tables(pos):
    inv = ROPE_THETA ** (-jnp.arange(0, ROPE_DIM, 2, dtype=F32) / ROPE_DIM)
    ang = pos.astype(F32)[:, None] * inv[None, :]
    cos, sin = jnp.cos(ang), jnp.sin(ang)
    reps = LANES // ROPE_DIM
    return (jnp.tile(jnp.concatenate([cos, cos], axis=-1), (1, reps)),
            jnp.tile(jnp.concatenate([-sin, sin], axis=-1), (1, reps)))


def kernel(x_prompt, x_sample, cache_mla_latent, cache_mla_krope, state_pool, state_hgrn, state_ffn_conv,
           g_mix, g_ffn, g_final, w_in_a, g_qa, w_qb, g_kva, w_uk, w_uv, w_pool, pool_scale, w_out_a,
           w_in_c, lb_param, g_onorm, w_out_c, w_up, conv_w, conv_b, w_down):
    n_even = w_in_a.shape[0]
    n_odd = w_in_c.shape[0]
    Tp, Ts = BATCH * SEQ, DEC_BATCH * DEC_SEQ

    o1, o2, o3 = Q_RANK, Q_RANK + KV_RANK, Q_RANK + KV_RANK + ROPE_DIM
    w_in_e = jnp.concatenate(
        [w_in_a[..., :o2], w_in_a[..., o3:], w_in_a[..., o2:o3],
         jnp.zeros((n_even, D_MODEL, LANES - ROPE_DIM), w_in_a.dtype)], axis=-1).astype(BF16)
    w_qb4 = w_qb.reshape(n_even, Q_RANK, MLA_HEADS, NOPE_DIM + ROPE_DIM)
    w_qb_e = jnp.concatenate(
        [w_qb4[..., :NOPE_DIM].reshape(n_even, Q_RANK, Q_NOPE_ALL),
         w_qb4[..., NOPE_DIM:].reshape(n_even, Q_RANK, MLA_HEADS * ROPE_DIM)], axis=-1).astype(BF16)
    w_ukT = jnp.transpose(w_uk, (0, 2, 3, 1)).astype(BF16)
    w_uvT = jnp.transpose(w_uv, (0, 2, 1, 3)).astype(BF16)
    w_pool_b = w_pool.astype(BF16)
    w_out_a_b = w_out_a.astype(BF16)
    w_in_c_b = w_in_c.astype(BF16)
    w_out_c_b = w_out_c.astype(BF16)
    w_down_b = w_down.astype(BF16)
    g_qa3 = g_qa[:, None, :]
    g_kva3 = g_kva[:, None, :]
    pool_scale3 = pool_scale[:, None, :]
    g_onorm3 = g_onorm[:, None, :]
    conv_b3 = conv_b[:, None, :]

    krope_t = jnp.swapaxes(cache_mla_krope, 2, 3)
    cos_p, sin_p = _rope_tables(jnp.tile(jnp.arange(SEQ), BATCH))
    cos_s, sin_s = _rope_tables(jnp.tile(PAST_LEN + jnp.arange(DEC_SEQ), DEC_BATCH))

    xp = x_prompt.reshape(Tp, D_MODEL)
    xs = x_sample.reshape(Ts, D_MODEL)
    xnp, xns = xp, xs

    lat_p, kpe_p, pool_p, hg_p, cv_p = [], [], [], [], []
    lat_s, kpe_s, pool_s, hg_s, cv_s = [], [], [], [], []
    for layer in range(DEPTH):
        li = layer // 2
        g_next = g_ffn[layer:layer + 1]
        if layer % 2 == 0:
            wa = (w_in_e, g_qa3, g_kva3, w_qb_e, w_ukT)
            lat, latb, kpe, kpeb, z, qlat, qpe = _even_in(xnp, g_mix[layer:layer + 1], *wa, cos_p, sin_p, li)
            y_mla = _attn_prompt(qlat, qpe, latb, kpeb, w_uvT, li)
            y_pool = _pool_prompt(z, w_pool_b, pool_scale3, li)
            xp, xnp = _proj_res_norm([y_mla, y_pool], w_out_a_b, xp, g_next, li)
            lat_p.append(lat.reshape(BATCH, SEQ, KV_RANK))
            kpe_p.append(kpe.reshape(BATCH, SEQ, ROPE_DIM))
            pool_p.append(z.reshape(BATCH, SEQ, POOL_WIDTH)[:, SEQ - POOL_KEEP:])

            lat, latb, kpe, kpeb, z, qlat, qpe = _even_in(xns, g_mix[layer:layer + 1], *wa, cos_s, sin_s, li)
            y_mla = _attn_sample(qlat, qpe, cache_mla_latent, krope_t, latb, kpeb, w_uvT, li)
            y_pool = _pool_sample(z, state_pool, w_pool_b, pool_scale3, li)
            xs, xns = _proj_res_norm([y_mla, y_pool], w_out_a_b, xs, g_next, li)
            lat_s.append(lat.reshape(DEC_BATCH, DEC_SEQ, KV_RANK))
            kpe_s.append(kpe.reshape(DEC_BATCH, DEC_SEQ, ROPE_DIM))
            pool_s.append(z.reshape(DEC_BATCH, DEC_SEQ, POOL_WIDTH)[:, DEC_SEQ - POOL_KEEP:])
        else:
            qfvg = _odd_in(xnp, w_in_c_b, lb_param, li)
            og, s_fin = _gla(qfvg, g_onorm3, None, li, BATCH, SEQ)
            xp, xnp = _proj_res_norm([og], w_out_c_b, xp, g_next, li, head_major=True)
            hg_p.append(s_fin)

            qfvg = _odd_in(xns, w_in_c_b, lb_param, li)
            og, s_fin = _gla(qfvg, g_onorm3, state_hgrn, li, DEC_BATCH, DEC_SEQ)
            xs, xns = _proj_res_norm([og], w_out_c_b, xs, g_next, li, head_major=True)
            hg_s.append(s_fin)

        final = layer == DEPTH - 1
        g_after = g_final[None, :] if final else g_mix[layer + 1:layer + 2]
        xn_dtype = F32 if final else BF16

        tm_p = SEQ
        act, hla, hlb = _ffn_up(xnp, w_up, conv_w, conv_b3, None, layer, tm=tm_p, seg=tm_p)
        xp, xnp = _ffn_down(act, w_down_b, xp, g_after, layer, not final, xn_dtype)
        hl = jnp.concatenate([hla, hlb], axis=-1).reshape(BATCH, SEQ // tm_p, CONV_W - 1, 2 * D_FF)
        cv_p.append(hl[:, -1])

        act, hla, hlb = _ffn_up(xns, w_up, conv_w, conv_b3, state_ffn_conv, layer, tm=512, seg=DEC_SEQ)
        xs, xns = _ffn_down(act, w_down_b, xs, g_after, layer, not final, xn_dtype)
        cv_s.append(jnp.concatenate([hla, hlb], axis=-1))

    return (xnp.reshape(BATCH, SEQ, D_MODEL), xns.reshape(DEC_BATCH, DEC_SEQ, D_MODEL),
            jnp.stack(lat_p), jnp.stack(kpe_p), jnp.stack(pool_p), jnp.stack(hg_p), jnp.stack(cv_p),
            jnp.stack(lat_s), jnp.stack(kpe_s), jnp.stack(pool_s), jnp.stack(hg_s), jnp.stack(cv_s))
```

```python
import functools

import jax
import jax.numpy as jnp
import numpy as np
from jax import lax
from jax.experimental import pallas as pl
from jax.experimental.pallas import tpu as pltpu

D_MODEL = 2048
BATCH = 8
SEQ = 2048
DEPTH = 4
DEC_BATCH = 32
DEC_SEQ = 64
PAST_LEN = 4096
CHUNK = 64
MLA_HEADS = 8
Q_RANK = 512
KV_RANK = 512
NOPE_DIM = 128
ROPE_DIM = 64
V_DIM = 128
ROPE_THETA = 10000.0
MLA_SCALE = (NOPE_DIM + ROPE_DIM) ** -0.5
LOG2_E = 1.4426950408889634
Q_SCALE = MLA_SCALE * LOG2_E
MASK_NEG = -1e30
POOL_WINDOWS = (2, 4, 8, 16)
POOL_WIDTH = D_MODEL - MLA_HEADS * V_DIM
POOL_GROUP_DIM = POOL_WIDTH // len(POOL_WINDOWS)
POOL_KEEP = max(POOL_WINDOWS) - 1
C_HEADS = 16
HEAD_F = 128
HEAD_I = D_MODEL // C_HEADS
F_MIN = 1e-30
D_FF = 5632
CONV_W = 3
EPS = 1e-6

F32 = jnp.float32
BF16 = jnp.bfloat16

LANES = 128
SUBLANES = 8
VMEM_LIMIT = 56 * 1024 * 1024

IN_EVEN_PAD = Q_RANK + KV_RANK + POOL_WIDTH + LANES
Q_NOPE_ALL = MLA_HEADS * NOPE_DIM
GLA_CHUNK = 64
GLA_SUB = SUBLANES
GLA_HEADS_PER_ITER = 16
GLA_STEP_ROWS = 64
FFN_TN = 512

EVEN_IN_TM = 256
ATTN_TQ = 256
ATTN_SAMPLE_TK = 1024
POOL_TP = 512
PROJ_TM = 512
ODD_IN_TM = 1024
FFN_UP_TM_SAMPLE = 512
FFN_DOWN_TM = 1024
FFN_DOWN_NK = 11


def _params(*sem):
    return pltpu.CompilerParams(dimension_semantics=sem, vmem_limit_bytes=VMEM_LIMIT)


def _dot(a, b):
    return jnp.dot(a, b, preferred_element_type=F32)


def _dot_nt(a, b):
    return lax.dot_general(a, b, (((1,), (1,)), ((), ())), preferred_element_type=F32)


def _dot_tn(a, b):
    return lax.dot_general(a, b, (((0,), (0,)), ((), ())), preferred_element_type=F32)


def _rms(x, g):
    return x * lax.rsqrt(jnp.mean(x * x, axis=-1, keepdims=True) + EPS) * g


def _rope(s, cos, sin, first_half):
    swapped = jnp.where(first_half, pltpu.roll(s, 96, 1), pltpu.roll(s, 32, 1))
    return s * cos + swapped * sin


def _even_in_kernel(x_ref, gin_ref, w_ref, gqa_ref, gkva_ref, wqb_ref, wuk_ref, cos_ref, sin_ref,
                    lat_ref, latb_ref, kpe_ref, kpeb_ref, z_ref, qlat_ref, qpe_ref):
    tm = x_ref.shape[0]
    xn = x_ref[...] if x_ref.dtype == BF16 else _rms(x_ref[...], gin_ref[...]).astype(BF16)
    acc = _dot(xn, w_ref[...])
    o1, o2, o3 = Q_RANK, Q_RANK + KV_RANK, Q_RANK + KV_RANK + POOL_WIDTH
    z_ref[...] = acc[:, o2:o3]
    lat = _rms(acc[:, o1:o2], gkva_ref[...])
    lat_ref[...] = lat
    latb_ref[...] = lat.astype(BF16)
    cos = cos_ref[...]
    sin = sin_ref[...]
    lane = lax.broadcasted_iota(jnp.int32, (tm, LANES), 1)
    first_half = (lane % ROPE_DIM) < (ROPE_DIM // 2)
    kr = _rope(acc[:, o3:o3 + LANES], cos, sin, first_half)
    kpe_ref[...] = kr[:, :ROPE_DIM]
    kpeb_ref[...] = kr[:, :ROPE_DIM].astype(BF16)
    cqn = _rms(acc[:, :o1], gqa_ref[...]).astype(BF16)
    q = _dot(cqn, wqb_ref[...])
    for s in range(MLA_HEADS // 2):
        lo = Q_NOPE_ALL + LANES * s
        r = (_rope(q[:, lo:lo + LANES], cos, sin, first_half) * Q_SCALE).astype(BF16)
        qpe_ref[2 * s] = r[:, :ROPE_DIM]
        qpe_ref[2 * s + 1] = r[:, ROPE_DIM:]
    for h in range(MLA_HEADS):
        ql = _dot(q[:, NOPE_DIM * h:NOPE_DIM * (h + 1)].astype(BF16), wuk_ref[h])
        qlat_ref[h] = (ql * Q_SCALE).astype(BF16)


def _even_in(x, g_in, w_in, g_qa, g_kva, w_qb, w_ukT, cos, sin, li, tm=EVEN_IN_TM):
    T = x.shape[0]
    row = lambda i: (i, 0)
    fix2 = lambda i: (0, 0)
    out_shape = (
        jax.ShapeDtypeStruct((T, KV_RANK), F32), jax.ShapeDtypeStruct((T, KV_RANK), BF16),
        jax.ShapeDtypeStruct((T, ROPE_DIM), F32), jax.ShapeDtypeStruct((T, ROPE_DIM), BF16),
        jax.ShapeDtypeStruct((T, POOL_WIDTH), F32),
        jax.ShapeDtypeStruct((MLA_HEADS, T, KV_RANK), BF16),
        jax.ShapeDtypeStruct((MLA_HEADS, T, ROPE_DIM), BF16),
    )
    return pl.pallas_call(
        _even_in_kernel,
        out_shape=out_shape,
        grid=(T // tm,),
        in_specs=[
            pl.BlockSpec((tm, D_MODEL), row),
            pl.BlockSpec((1, D_MODEL), fix2),
            pl.BlockSpec((None, D_MODEL, IN_EVEN_PAD), lambda i: (li, 0, 0)),
            pl.BlockSpec((None, 1, Q_RANK), lambda i: (li, 0, 0)),
            pl.BlockSpec((None, 1, KV_RANK), lambda i: (li, 0, 0)),
            pl.BlockSpec((None, Q_RANK, MLA_HEADS * (NOPE_DIM + ROPE_DIM)), lambda i: (li, 0, 0)),
            pl.BlockSpec((None, MLA_HEADS, NOPE_DIM, KV_RANK), lambda i: (li, 0, 0, 0)),
            pl.BlockSpec((tm, LANES), row),
            pl.BlockSpec((tm, LANES), row),
        ],
        out_specs=(
            pl.BlockSpec((tm, KV_RANK), row), pl.BlockSpec((tm, KV_RANK), row),
            pl.BlockSpec((tm, ROPE_DIM), row), pl.BlockSpec((tm, ROPE_DIM), row),
            pl.BlockSpec((tm, POOL_WIDTH), row),
            pl.BlockSpec((MLA_HEADS, tm, KV_RANK), lambda i: (0, i, 0)),
            pl.BlockSpec((MLA_HEADS, tm, ROPE_DIM), lambda i: (0, i, 0)),
        ),
        compiler_params=_params("parallel"),
        name="even_in",
    )(x, g_in, w_in, g_qa, g_kva, w_qb, w_ukT, cos, sin)


ATTN_HEAD_GROUP = 4
ATTN_SAMPLE_SUB_KEYS = 512


def _lane_tile(x, width):
    if width <= LANES:
        return x[:, :width]
    return jnp.concatenate([x] * (width // LANES), axis=1)


def _attn_keys_step(qlat_ref, qpe_ref, k, kp, m_ref, l_ref, acc_ref, tq, group, visible=None,
                    kp_transposed=False):
    tk = k.shape[0]
    n = group * tq

    def scores(g):
        ql = qlat_ref[g * group:(g + 1) * group].reshape(n, KV_RANK)
        qp = qpe_ref[g * group:(g + 1) * group].reshape(n, ROPE_DIM)
        s = _dot_nt(ql, k) + (_dot(qp, kp) if kp_transposed else _dot_nt(qp, kp))
        return s if visible is None else jnp.where(visible, s, MASK_NEG)

    s_next = scores(0)
    for g in range(MLA_HEADS // group):
        s = s_next
        if g + 1 < MLA_HEADS // group:
            s_next = scores(g + 1)
        rows = pl.ds(g * n, n)
        m_prev = m_ref[rows, :]
        m_new = jnp.maximum(m_prev, jnp.max(s, axis=1, keepdims=True))
        alpha = jnp.exp2(m_prev - m_new)
        p = jnp.exp2(s - _lane_tile(m_new, tk))
        l_ref[rows, :] = alpha * l_ref[rows, :] + jnp.sum(p, axis=1, keepdims=True)
        acc_ref[rows, :] = _lane_tile(alpha, KV_RANK) * acc_ref[rows, :] + _dot(p.astype(BF16), k)
        m_ref[rows, :] = m_new


def _attn_init(m_ref, l_ref, acc_ref):
    m_ref[...] = jnp.full(m_ref.shape, MASK_NEG, F32)
    l_ref[...] = jnp.zeros(l_ref.shape, F32)
    acc_ref[...] = jnp.zeros(acc_ref.shape, F32)


def _attn_finish(wuv_ref, y_ref, l_ref, acc_ref, tq):
    for h in range(MLA_HEADS):
        rows = pl.ds(h * tq, tq)
        o = (acc_ref[rows, :] / _lane_tile(l_ref[rows, :], KV_RANK)).astype(BF16)
        y_ref[:, V_DIM * h:V_DIM * (h + 1)] = _dot(o, wuv_ref[h]).astype(BF16)


def _attn_prompt_kernel(qi_ref, kj_ref, qlat_ref, qpe_ref, k_ref, kp_ref, wuv_ref, y_ref,
                        m_ref, l_ref, acc_ref, *, tq):
    p = pl.program_id(1)
    i = qi_ref[p]
    j = kj_ref[p]

    @pl.when(j == 0)
    def _():
        _attn_init(m_ref, l_ref, acc_ref)

    @pl.when(j < i)
    def _():
        _attn_keys_step(qlat_ref, qpe_ref, k_ref[...], kp_ref[...], m_ref, l_ref, acc_ref, tq,
                        ATTN_HEAD_GROUP)

    @pl.when(j == i)
    def _():
        n = ATTN_HEAD_GROUP * tq
        row = lax.broadcasted_iota(jnp.int32, (n, tq), 0) % tq
        col = lax.broadcasted_iota(jnp.int32, (n, tq), 1)
        visible = col // CHUNK <= row // CHUNK
        _attn_keys_step(qlat_ref, qpe_ref, k_ref[...], kp_ref[...], m_ref, l_ref, acc_ref, tq,
                        ATTN_HEAD_GROUP, visible)
        _attn_finish(wuv_ref, y_ref, l_ref, acc_ref, tq)


def _attn_prompt(qlat, qpe, latb, kpeb, w_uvT, li, tq=ATTN_TQ):
    assert tq % CHUNK == 0
    T = latb.shape[0]
    nq = SEQ // tq
    pairs = [(i, j) for i in range(nq) for j in range(i + 1)]
    q_of = jnp.asarray([i for i, _ in pairs], jnp.int32)
    k_of = jnp.asarray([j for _, j in pairs], jnp.int32)
    qmap = lambda b, p, qi, kj: (0, b * nq + qi[p], 0)
    kmap = lambda b, p, qi, kj: (b * nq + kj[p], 0)
    rows = MLA_HEADS * tq
    return pl.pallas_call(
        functools.partial(_attn_prompt_kernel, tq=tq),
        out_shape=jax.ShapeDtypeStruct((T, MLA_HEADS * V_DIM), BF16),
        grid_spec=pltpu.PrefetchScalarGridSpec(
            num_scalar_prefetch=2,
            grid=(BATCH, len(pairs)),
            in_specs=[
                pl.BlockSpec((MLA_HEADS, tq, KV_RANK), qmap),
                pl.BlockSpec((MLA_HEADS, tq, ROPE_DIM), qmap),
                pl.BlockSpec((tq, KV_RANK), kmap),
                pl.BlockSpec((tq, ROPE_DIM), kmap),
                pl.BlockSpec((None, MLA_HEADS, KV_RANK, V_DIM), lambda b, p, qi, kj: (li, 0, 0, 0)),
            ],
            out_specs=pl.BlockSpec((tq, MLA_HEADS * V_DIM), lambda b, p, qi, kj: (b * nq + qi[p], 0)),
            scratch_shapes=[pltpu.VMEM((rows, LANES), F32), pltpu.VMEM((rows, LANES), F32),
                            pltpu.VMEM((rows, KV_RANK), F32)],
        ),
        compiler_params=_params("parallel", "arbitrary"),
        name="attn_prompt",
    )(q_of, k_of, qlat, qpe, latb, kpeb, w_uvT)


def _attn_sample_kernel(qlat_ref, qpe_ref, ck_ref, ckp_ref, nk_ref, nkp_ref, wuv_ref, y_ref,
                        m_ref, l_ref, acc_ref):
    j = pl.program_id(1)
    last = pl.num_programs(1) - 1

    @pl.when(j == 0)
    def _():
        _attn_init(m_ref, l_ref, acc_ref)

    def step(k, kp, kp_transposed=False):
        _attn_keys_step(qlat_ref, qpe_ref, k, kp, m_ref, l_ref, acc_ref, DEC_SEQ, MLA_HEADS,
                        kp_transposed=kp_transposed)

    @pl.when(j < last)
    def _():
        for c in range(0, ck_ref.shape[0], ATTN_SAMPLE_SUB_KEYS):
            keys = pl.ds(c, ATTN_SAMPLE_SUB_KEYS)
            step(ck_ref[keys, :].astype(BF16), ckp_ref[:, keys].astype(BF16), kp_transposed=True)

    @pl.when(j == last)
    def _():
        step(nk_ref[...], nkp_ref[...])
        _attn_finish(wuv_ref, y_ref, l_ref, acc_ref, DEC_SEQ)


def _attn_sample(qlat, qpe, cache_lat, cache_kpe_t, latb, kpeb, w_uvT, li, tk=ATTN_SAMPLE_TK):
    assert PAST_LEN % CHUNK == 0 and DEC_SEQ <= CHUNK and PAST_LEN % tk == 0
    T = latb.shape[0]
    nc = PAST_LEN // tk
    qmap = lambda b, j: (0, b, 0)
    cmap = lambda b, j: (li, b, jnp.minimum(j, nc - 1), 0)
    rows = MLA_HEADS * DEC_SEQ
    return pl.pallas_call(
        _attn_sample_kernel,
        out_shape=jax.ShapeDtypeStruct((T, MLA_HEADS * V_DIM), BF16),
        grid=(DEC_BATCH, nc + 1),
        in_specs=[
            pl.BlockSpec((MLA_HEADS, DEC_SEQ, KV_RANK), qmap),
            pl.BlockSpec((MLA_HEADS, DEC_SEQ, ROPE_DIM), qmap),
            pl.BlockSpec((None, None, tk, KV_RANK), cmap),
            pl.BlockSpec((None, None, ROPE_DIM, tk), lambda b, j: (li, b, 0, jnp.minimum(j, nc - 1))),
            pl.BlockSpec((DEC_SEQ, KV_RANK), lambda b, j: (b, 0)),
            pl.BlockSpec((DEC_SEQ, ROPE_DIM), lambda b, j: (b, 0)),
            pl.BlockSpec((None, MLA_HEADS, KV_RANK, V_DIM), lambda b, j: (li, 0, 0, 0)),
        ],
        out_specs=pl.BlockSpec((DEC_SEQ, MLA_HEADS * V_DIM), lambda b, j: (b, 0)),
        scratch_shapes=[pltpu.VMEM((rows, LANES), F32), pltpu.VMEM((rows, LANES), F32),
                        pltpu.VMEM((rows, KV_RANK), F32)],
        compiler_params=_params("parallel", "arbitrary"),
        name="attn_sample",
    )(qlat, qpe, cache_lat, cache_kpe_t, latb, kpeb, w_uvT)


POOL_HALO = 2 * SUBLANES


def _pool_kernel(*refs, tp, from_state):
    if from_state:
        z_ref, past_ref, w_ref, scale_ref, y_ref, ext_ref = refs
        ext_ref[0:1, :] = jnp.zeros((1, POOL_WIDTH), F32)
        ext_ref[1:POOL_HALO, :] = past_ref[...]
    else:
        z_ref, w_ref, scale_ref, y_ref, ext_ref = refs
        t = pl.program_id(1)

        @pl.when(t == 0)
        def _():
            ext_ref[0:POOL_HALO, :] = jnp.zeros((POOL_HALO, POOL_WIDTH), F32)

    ext_ref[POOL_HALO:POOL_HALO + tp, :] = z_ref[...]
    for gi, w in enumerate(POOL_WINDOWS):
        lanes = slice(POOL_GROUP_DIM * gi, POOL_GROUP_DIM * (gi + 1))
        tok = ext_ref[POOL_HALO:POOL_HALO + tp, lanes]
        acc = tok
        for d in range(1, w):
            acc = acc + ext_ref[POOL_HALO - d:POOL_HALO - d + tp, lanes]
        if from_state:
            mean = acc / float(w)
        else:
            pos = t * tp + lax.broadcasted_iota(jnp.int32, (tp, 1), 0)
            mean = acc / jnp.minimum(pos + 1, w).astype(F32)
        p = (mean - tok).astype(BF16)
        y_ref[:, lanes] = (_dot(p, w_ref[gi]) * scale_ref[:, lanes]).astype(BF16)
    if not from_state:
        ext_ref[0:POOL_HALO, :] = ext_ref[tp:tp + POOL_HALO, :]


def _pool_prompt(z, w_pool, pool_scale, li, tp=POOL_TP):
    T = z.shape[0]
    nt = SEQ // tp
    return pl.pallas_call(
        functools.partial(_pool_kernel, tp=tp, from_state=False),
        out_shape=jax.ShapeDtypeStruct((T, POOL_WIDTH), BF16),
        grid=(BATCH, nt),
        in_specs=[
            pl.BlockSpec((tp, POOL_WIDTH), lambda b, t: (b * nt + t, 0)),
            pl.BlockSpec((None, len(POOL_WINDOWS), POOL_GROUP_DIM, POOL_GROUP_DIM), lambda b, t: (li, 0, 0, 0)),
            pl.BlockSpec((None, 1, POOL_WIDTH), lambda b, t: (li, 0, 0)),
        ],
        out_specs=pl.BlockSpec((tp, POOL_WIDTH), lambda b, t: (b * nt + t, 0)),
        scratch_shapes=[pltpu.VMEM((POOL_HALO + tp, POOL_WIDTH), F32)],
        compiler_params=_params("arbitrary", "arbitrary"),
        name="pool_prompt",
    )(z, w_pool, pool_scale)


def _pool_sample(z, state_pool, w_pool, pool_scale, li):
    T = z.shape[0]
    return pl.pallas_call(
        functools.partial(_pool_kernel, tp=DEC_SEQ, from_state=True),
        out_shape=jax.ShapeDtypeStruct((T, POOL_WIDTH), BF16),
        grid=(DEC_BATCH,),
        in_specs=[
            pl.BlockSpec((DEC_SEQ, POOL_WIDTH), lambda b: (b, 0)),
            pl.BlockSpec((None, None, POOL_KEEP, POOL_WIDTH), lambda b: (li, b, 0, 0)),
            pl.BlockSpec((None, len(POOL_WINDOWS), POOL_GROUP_DIM, POOL_GROUP_DIM), lambda b: (li, 0, 0, 0)),
            pl.BlockSpec((None, 1, POOL_WIDTH), lambda b: (li, 0, 0)),
        ],
        out_specs=pl.BlockSpec((DEC_SEQ, POOL_WIDTH), lambda b: (b, 0)),
        scratch_shapes=[pltpu.VMEM((POOL_HALO + DEC_SEQ, POOL_WIDTH), F32)],
        compiler_params=_params("parallel"),
        name="pool_sample",
    )(z, state_pool, w_pool, pool_scale)


def _proj_res_norm_kernel(*refs, n_a, head_major, emit_x):
    a_refs = refs[:n_a]
    w_refs = refs[n_a:2 * n_a]
    x_ref, g_ref = refs[2 * n_a:2 * n_a + 2]
    outs = refs[2 * n_a + 2:]
    if head_major:
        (a_ref,), (w_ref,) = a_refs, w_refs
        xo_ref, xn_ref, cat_ref = outs
        for h in range(C_HEADS):
            cat_ref[:, HEAD_I * h:HEAD_I * (h + 1)] = a_ref[h]
        acc = _dot(cat_ref[...], w_ref[...])
    else:
        xo_ref, xn_ref = outs if emit_x else (None, outs[0])
        acc = _dot(a_refs[0][...], w_refs[0][...])
        for a_ref, w_ref in zip(a_refs[1:], w_refs[1:]):
            acc = acc + _dot(a_ref[...], w_ref[...])
    xnew = x_ref[...] + acc
    if xo_ref is not None:
        xo_ref[...] = xnew
    xn_ref[...] = _rms(xnew, g_ref[...]).astype(xn_ref.dtype)


def _proj_res_norm(a_list, w, x, g, li, head_major=False, tm=PROJ_TM):
    T = x.shape[0]
    row = lambda i: (i, 0)
    n_a = len(a_list)
    in_specs = []
    for a in a_list:
        if head_major:
            in_specs.append(pl.BlockSpec((C_HEADS, tm, HEAD_I), lambda i: (0, i, 0)))
        else:
            in_specs.append(pl.BlockSpec((tm, a.shape[1]), row))
    for kb, a in enumerate(a_list):
        kdim = C_HEADS * HEAD_I if head_major else a.shape[1]
        in_specs.append(pl.BlockSpec((None, kdim, D_MODEL), lambda i, kb=kb: (li, kb, 0)))
    in_specs += [pl.BlockSpec((tm, D_MODEL), row), pl.BlockSpec((1, D_MODEL), lambda i: (0, 0))]
    scratch = [pltpu.VMEM((tm, D_MODEL), BF16)] if head_major else []
    return pl.pallas_call(
        functools.partial(_proj_res_norm_kernel, n_a=n_a, head_major=head_major, emit_x=True),
        out_shape=(jax.ShapeDtypeStruct((T, D_MODEL), F32), jax.ShapeDtypeStruct((T, D_MODEL), BF16)),
        grid=(T // tm,),
        in_specs=in_specs,
        out_specs=(pl.BlockSpec((tm, D_MODEL), row), pl.BlockSpec((tm, D_MODEL), row)),
        scratch_shapes=scratch,
        compiler_params=_params("parallel"),
        name="proj_res_norm",
    )(*a_list, *([w] * n_a), x, g)


def _odd_in_kernel(xn_ref, w_ref, lbp_ref, o_ref, *, li):
    s = pl.program_id(0)

    def emit(epilogue):
        out = epilogue(_dot(xn_ref[...], w_ref[...]))
        for hd in range(C_HEADS):
            o_ref[hd] = out[:, HEAD_I * hd:HEAD_I * (hd + 1)]

    def silu(h):
        half = 0.5 * h
        return half + half * jnp.tanh(half)

    def forget(h):
        p = lbp_ref[...]
        e = jnp.exp(p - jnp.max(p, axis=0, keepdims=True))
        sm = e / jnp.sum(e, axis=0, keepdims=True)
        lb = jnp.clip(jnp.sum(sm[:li + 1], axis=0, keepdims=True) - sm[0:1], 0.0, 1.0)
        gain = 0.5 * (1.0 - lb)
        return (lb + gain) + gain * jnp.tanh(0.5 * h)

    for sec, epilogue in enumerate((silu, forget, lambda h: h, silu)):
        @pl.when(s == sec)
        def _(epilogue=epilogue):
            emit(epilogue)


def _odd_in(xn, w_in_c, lb_param, li, tm=ODD_IN_TM):
    T = xn.shape[0]
    return pl.pallas_call(
        functools.partial(_odd_in_kernel, li=li),
        out_shape=jax.ShapeDtypeStruct((4, C_HEADS, T, HEAD_I), F32),
        grid=(4, T // tm),
        in_specs=[
            pl.BlockSpec((tm, D_MODEL), lambda s, i: (i, 0)),
            pl.BlockSpec((None, D_MODEL, D_MODEL), lambda s, i: (li, 0, s)),
            pl.BlockSpec(lb_param.shape, lambda s, i: (0, 0)),
        ],
        out_specs=pl.BlockSpec((None, C_HEADS, tm, HEAD_I), lambda s, i: (s, 0, i, 0)),
        compiler_params=_params("parallel", "parallel"),
        name="odd_in",
    )(xn, w_in_c, lb_param)


def _gla_kernel(*refs, has_s0):
    if has_s0:
        q_ref, f_ref, v_ref, gs_ref, gon_ref, cls_ref, s0_ref, og_ref, sout_ref, st_ref = refs
    else:
        q_ref, f_ref, v_ref, gs_ref, gon_ref, cls_ref, og_ref, sout_ref, st_ref = refs
    C = GLA_CHUNK
    n = pl.program_id(1)
    last = pl.num_programs(1) - 1

    @pl.when(n == 0)
    def _():
        if has_s0:
            st_ref[...] = s0_ref[...]
        else:
            st_ref[...] = jnp.zeros(st_ref.shape, F32)

    cls = cls_ref[...]
    tri = (cls >= 0).astype(BF16)
    ones = jnp.ones((HEAD_F, LANES), BF16)
    gon = gon_ref[...]
    near = [cls == delta for delta in range(GLA_SUB)]
    levels = [(GLA_SUB << i, cls == GLA_SUB + i) for i in range((C // GLA_SUB).bit_length() - 1)]

    def bcast_row(a, period, r):
        a3 = a.reshape(C // period, period, a.shape[-1])
        return jnp.broadcast_to(a3[:, r:r + 1, :], a3.shape).reshape(C, a.shape[-1])

    def shift_in_group(a, delta):
        if delta == 0:
            return a
        a3 = a.reshape(C // GLA_SUB, GLA_SUB, a.shape[-1])
        return pltpu.roll(a3, delta, 1).reshape(C, a.shape[-1])

    def split3(g):
        g1 = g.astype(BF16)
        r1 = g - g1.astype(F32)
        g2 = r1.astype(BF16)
        return g1, g2, (r1 - g2.astype(F32)).astype(BF16)

    def head_group(hg, row0):
        hs = [hg * GLA_HEADS_PER_ITER + u for u in range(GLA_HEADS_PER_ITER)]
        U = range(len(hs))
        rows = pl.ds(row0, C)
        q = [q_ref[h, rows, :] for h in hs]
        f = [f_ref[h, rows, :] for h in hs]
        v_bf = [v_ref[h, rows, :].astype(BF16) for h in hs]
        fc = [jnp.maximum(x, F_MIN) for x in f]
        kk = [1.0 - x for x in f]
        gp = [split3(jnp.log2(x)) for x in fc]
        b = [_dot(tri, gp[u][0]) + _dot(tri, gp[u][1]) + _dot(tri, gp[u][2]) for u in U]
        st = [st_ref[h] for h in hs]
        o = [_dot((q[u] * jnp.exp2(b[u])).astype(BF16), st[u].astype(BF16)) for u in U]
        b_last = [x[C - 1:C, :] for x in b]
        k_dec = [(kk[u] * jnp.exp2(b_last[u] - b[u])).astype(BF16) for u in U]
        for u in U:
            d_rows = jnp.broadcast_to(jnp.exp2(b_last[u]), (LANES, HEAD_F)).T
            st_ref[hs[u]] = st[u] * d_rows + _dot_tn(k_dec[u], v_bf[u])

        a = [jnp.zeros((C, C), F32) for _ in U]
        for m, keep in levels:
            for u in U:
                d = b[u] - bcast_row(b[u], 2 * m, m - 1)
                qt = (q[u] * jnp.exp2(jnp.minimum(d, 0.0))).astype(BF16)
                kt = (kk[u] * jnp.exp2(jnp.minimum(-d, 0.0))).astype(BF16)
                a[u] = jnp.where(keep, _dot_nt(qt, kt), a[u])
        kd = list(kk)
        for delta in range(GLA_SUB):
            for u in U:
                if delta > 0:
                    kd[u] = shift_in_group(kd[u], 1) * fc[u]
                p = (q[u] * kd[u]).astype(BF16)
                a[u] = jnp.where(near[delta], _dot(p, ones)[:, :C], a[u])
        for u in U:
            y = _rms(o[u] + _dot(a[u].astype(BF16), v_bf[u]), gon) * gs_ref[hs[u], rows, :]
            og_ref[hs[u], rows, :] = y.astype(BF16)
        return row0

    def chunk(c, carry):
        lax.fori_loop(0, C_HEADS // GLA_HEADS_PER_ITER, head_group, pl.multiple_of(c * C, C))
        return carry

    lax.fori_loop(0, q_ref.shape[1] // C, chunk, 0)

    @pl.when(n == last)
    def _():
        sout_ref[...] = st_ref[...]


def _gla_pair_classes():
    t = np.arange(GLA_CHUNK)[:, None]
    s = np.arange(GLA_CHUNK)[None, :]
    x = (t // GLA_SUB) ^ (s // GLA_SUB)
    far = GLA_SUB + np.floor(np.log2(np.maximum(x, 1))).astype(np.int64)
    return jnp.asarray(np.where(s > t, -1, np.where(x == 0, t - s, far)), jnp.int32)


def _gla(qfvg, g_onorm, s0, li, n_streams, seq):
    T = qfvg.shape[2]
    C = min(GLA_STEP_ROWS, seq)
    nc = seq // C
    has_s0 = s0 is not None
    sec = lambda k: pl.BlockSpec((None, C_HEADS, C, HEAD_F), lambda b, n, k=k: (k, 0, b * nc + n, 0))
    in_specs = [sec(0), sec(1), sec(2), sec(3),
                pl.BlockSpec((None, 1, HEAD_I), lambda b, n: (li, 0, 0)),
                pl.BlockSpec((GLA_CHUNK, GLA_CHUNK), lambda b, n: (0, 0))]
    args = [qfvg, qfvg, qfvg, qfvg, g_onorm, _gla_pair_classes()]
    if has_s0:
        in_specs.append(pl.BlockSpec((None, None, C_HEADS, HEAD_F, HEAD_I), lambda b, n: (li, b, 0, 0, 0)))
        args.append(s0)
    return pl.pallas_call(
        functools.partial(_gla_kernel, has_s0=has_s0),
        out_shape=(jax.ShapeDtypeStruct((C_HEADS, T, HEAD_I), BF16),
                   jax.ShapeDtypeStruct((n_streams, C_HEADS, HEAD_F, HEAD_I), F32)),
        grid=(n_streams, nc),
        in_specs=in_specs,
        out_specs=(pl.BlockSpec((C_HEADS, C, HEAD_I), lambda b, n: (0, b * nc + n, 0)),
                   pl.BlockSpec((None, C_HEADS, HEAD_F, HEAD_I), lambda b, n: (b, 0, 0, 0))),
        scratch_shapes=[pltpu.VMEM((C_HEADS, HEAD_I, HEAD_F), F32)],
        compiler_params=_params("parallel", "arbitrary"),
        name="gla",
    )(*args)


FFN_PAD = SUBLANES
FFN_SUB_ROWS = 256


def _ffn_up_kernel(*refs, tm, seg, from_state, tiles_per_stream):
    if from_state:
        (xn_ref, wa_ref, wb_ref, cwa_ref, cwb_ref, cba_ref, cbb_ref, pa_ref, pb_ref,
         act_ref, hla_ref, hlb_ref, w_ref) = refs
        carry_ref = None
    else:
        (xn_ref, wa_ref, wb_ref, cwa_ref, cwb_ref, cba_ref, cbb_ref,
         act_ref, hla_ref, hlb_ref, w_ref, carry_ref) = refs
        pa_ref = pb_ref = None
    j = pl.program_id(0)
    i = pl.program_id(1)
    sub = min(FFN_SUB_ROWS, tm)
    assert (seg % sub == 0 and not from_state) or (sub % seg == 0 and sub > seg)
    halves = ((0, cwa_ref, cba_ref, pa_ref, hla_ref), (1, cwb_ref, cbb_ref, pb_ref, hlb_ref))

    @pl.when(i == 0)
    def _():
        w_ref[0] = wa_ref[...].astype(BF16)
        w_ref[1] = wb_ref[...].astype(BF16)

    def conv(front, h, cw, cb):
        n = h.shape[0]
        ext = jnp.concatenate([front, h], axis=0)
        return cb + (ext[FFN_PAD - 2:FFN_PAD - 2 + n] * cw[0:1] + ext[FFN_PAD - 1:FFN_PAD - 1 + n] * cw[1:2] +
                     h * cw[2:3])

    fronts = [None, None]
    if not from_state:
        @pl.when(i % tiles_per_stream == 0)
        def _():
            carry_ref[j] = jnp.zeros((2, FFN_PAD, FFN_TN), F32)
        fronts = [carry_ref[j, 0], carry_ref[j, 1]]

    for r0 in range(0, tm, sub):
        convs = []
        for hidx, cw_ref, cb_ref, past_ref, hl_ref in halves:
            h = _dot(xn_ref[r0:r0 + sub, :], w_ref[hidx])
            cw = cw_ref[...]
            cb = cb_ref[...]
            if seg >= sub:
                convs.append([conv(fronts[hidx], h, cw, cb)])
                fronts[hidx] = h[sub - FFN_PAD:sub]
                if (r0 + sub) % seg == 0:
                    hl_ref[(r0 + sub) // seg - 1] = h[sub - 2:sub]
            else:
                pieces = []
                for k in range(sub // seg):
                    s = r0 // seg + k
                    hk = h[k * seg:(k + 1) * seg]
                    front = jnp.concatenate([jnp.zeros((FFN_PAD - 2, FFN_TN), F32), past_ref[s]], axis=0)
                    pieces.append(conv(front, hk, cw, cb))
                    hl_ref[s] = hk[seg - 2:seg]
                convs.append(pieces)
        for n, (a, b) in enumerate(zip(*convs)):
            rows = a.shape[0]
            o = r0 + n * rows
            half = 0.5 * a
            act_ref[o:o + rows, :] = ((half + half * jnp.tanh(half)) * b).astype(BF16)

    if not from_state:
        carry_ref[j, 0] = fronts[0]
        carry_ref[j, 1] = fronts[1]


def _ffn_up(xn, w_up, conv_w, conv_b, conv_past, layer, tm, seg):
    T = xn.shape[0]
    tn = FFN_TN
    nj = D_FF // tn
    nseg = tm // seg
    from_state = conv_past is not None
    wmap_a = lambda j, i: (layer, 0, j)
    wmap_b = lambda j, i: (layer, 0, nj + j)
    in_specs = [
        pl.BlockSpec((tm, D_MODEL), lambda j, i: (i, 0)),
        pl.BlockSpec((None, D_MODEL, tn), wmap_a), pl.BlockSpec((None, D_MODEL, tn), wmap_b),
        pl.BlockSpec((None, CONV_W, tn), wmap_a), pl.BlockSpec((None, CONV_W, tn), wmap_b),
        pl.BlockSpec((None, 1, tn), wmap_a), pl.BlockSpec((None, 1, tn), wmap_b),
    ]
    args = [xn, w_up, w_up, conv_w, conv_w, conv_b, conv_b]
    scratch = [pltpu.VMEM((2, D_MODEL, tn), BF16)]
    if from_state:
        in_specs += [pl.BlockSpec((None, nseg, CONV_W - 1, tn), lambda j, i: (layer, i, 0, j)),
                     pl.BlockSpec((None, nseg, CONV_W - 1, tn), lambda j, i: (layer, i, 0, nj + j))]
        args += [conv_past, conv_past]
        tiles_per_stream = 1
    else:
        scratch.append(pltpu.VMEM((nj, 2, FFN_PAD, tn), F32))
        tiles_per_stream = SEQ // tm
    hl_shape = jax.ShapeDtypeStruct((T // seg, CONV_W - 1, D_FF), F32)
    hl_spec = pl.BlockSpec((nseg, CONV_W - 1, tn), lambda j, i: (i, 0, j))
    return pl.pallas_call(
        functools.partial(_ffn_up_kernel, tm=tm, seg=seg, from_state=from_state,
                          tiles_per_stream=tiles_per_stream),
        out_shape=(jax.ShapeDtypeStruct((T, D_FF), BF16), hl_shape, hl_shape),
        grid=(nj, T // tm),
        in_specs=in_specs,
        out_specs=(pl.BlockSpec((tm, tn), lambda j, i: (i, j)), hl_spec, hl_spec),
        scratch_shapes=scratch,
        compiler_params=_params("arbitrary", "arbitrary"),
        name="ffn_up",
    )(*args)


def _ffn_down_kernel(a_ref, w_ref, x_ref, g_ref, *outs, emit_x):
    if emit_x:
        acc_ref, xn_ref = outs
    else:
        xn_ref, acc_ref = outs
    k = pl.program_id(1)

    @pl.when(k == 0)
    def _():
        acc_ref[...] = x_ref[...]

    acc_ref[...] += _dot(a_ref[...], w_ref[...])

    @pl.when(k == pl.num_programs(1) - 1)
    def _():
        xn_ref[...] = _rms(acc_ref[...], g_ref[...]).astype(xn_ref.dtype)


def _ffn_down(act, w_down, x, g, layer, emit_x, xn_dtype, tm=FFN_DOWN_TM, nk=FFN_DOWN_NK):
    T = x.shape[0]
    tk = D_FF // nk
    row = lambda i, k: (i, 0)
    out_shape = [jax.ShapeDtypeStruct((T, D_MODEL), xn_dtype)]
    out_specs = [pl.BlockSpec((tm, D_MODEL), row)]
    scratch = [pltpu.VMEM((tm, D_MODEL), F32)]
    if emit_x:
        out_shape.insert(0, jax.ShapeDtypeStruct((T, D_MODEL), F32))
        out_specs.insert(0, pl.BlockSpec((tm, D_MODEL), row))
        scratch = []
    res = pl.pallas_call(
        functools.partial(_ffn_down_kernel, emit_x=emit_x),
        out_shape=tuple(out_shape),
        grid=(T // tm, nk),
        in_specs=[
            pl.BlockSpec((tm, tk), lambda i, k: (i, k)),
            pl.BlockSpec((None, tk, D_MODEL), lambda i, k: (layer, k, 0)),
            pl.BlockSpec((tm, D_MODEL), row),
            pl.BlockSpec((1, D_MODEL), lambda i, k: (0, 0)),
        ],
        out_specs=tuple(out_specs),
        scratch_shapes=scratch,
        compiler_params=_params("parallel", "arbitrary"),
        name="ffn_down",
    )(act, w_down, x, g)
    return res if emit_x else (None, res[0])


def _rope_tables(pos):
    inv = ROPE_THETA ** (-jnp.arange(0, ROPE_DIM, 2, dtype=F32) / ROPE_DIM)
    ang = pos.astype(F32)[:, None] * inv[None, :]
    cos, sin = jnp.cos(ang), jnp.sin(ang)
    reps = LANES // ROPE_DIM
    return (jnp.tile(jnp.concatenate([cos, cos], axis=-1), (1, reps)),
            jnp.tile(jnp.concatenate([-sin, sin], axis=-1), (1, reps)))


def kernel(x_prompt, x_sample, cache_mla_latent, cache_mla_krope, state_pool, state_hgrn, state_ffn_conv,
           g_mix, g_ffn, g_final, w_in_a, g_qa, w_qb, g_kva, w_uk, w_uv, w_pool, pool_scale, w_out_a,
           w_in_c, lb_param, g_onorm, w_out_c, w_up, conv_w, conv_b, w_down):
    n_even = w_in_a.shape[0]
    n_odd = w_in_c.shape[0]
    Tp, Ts = BATCH * SEQ, DEC_BATCH * DEC_SEQ

    o1, o2, o3 = Q_RANK, Q_RANK + KV_RANK, Q_RANK + KV_RANK + ROPE_DIM
    w_in_e = jnp.concatenate(
        [w_in_a[..., :o2], w_in_a[..., o3:], w_in_a[..., o2:o3],
         jnp.zeros((n_even, D_MODEL, LANES - ROPE_DIM), w_in_a.dtype)], axis=-1).astype(BF16)
    w_qb4 = w_qb.reshape(n_even, Q_RANK, MLA_HEADS, NOPE_DIM + ROPE_DIM)
    w_qb_e = jnp.concatenate(
        [w_qb4[..., :NOPE_DIM].reshape(n_even, Q_RANK, Q_NOPE_ALL),
         w_qb4[..., NOPE_DIM:].reshape(n_even, Q_RANK, MLA_HEADS * ROPE_DIM)], axis=-1).astype(BF16)
    w_ukT = jnp.transpose(w_uk, (0, 2, 3, 1)).astype(BF16)
    w_uvT = jnp.transpose(w_uv, (0, 2, 1, 3)).astype(BF16)
    w_pool_b = w_pool.astype(BF16)
    w_out_a_b = w_out_a.astype(BF16)
    w_in_c_b = w_in_c.astype(BF16)
    w_out_c_b = w_out_c.astype(BF16)
    w_down_b = w_down.astype(BF16)
    g_qa3 = g_qa[:, None, :]
    g_kva3 = g_kva[:, None, :]
    pool_scale3 = pool_scale[:, None, :]
    g_onorm3 = g_onorm[:, None, :]
    conv_b3 = conv_b[:, None, :]

    krope_t = jnp.swapaxes(cache_mla_krope, 2, 3)
    cos_p, sin_p = _rope_tables(jnp.tile(jnp.arange(SEQ), BATCH))
    cos_s, sin_s = _rope_tables(jnp.tile(PAST_LEN + jnp.arange(DEC_SEQ), DEC_BATCH))

    xp = x_prompt.reshape(Tp, D_MODEL)
    xs = x_sample.reshape(Ts, D_MODEL)
    xnp, xns = xp, xs

    lat_p, kpe_p, pool_p, hg_p, cv_p = [], [], [], [], []
    lat_s, kpe_s, pool_s, hg_s, cv_s = [], [], [], [], []
    for layer in range(DEPTH):
        li = layer // 2
        g_next = g_ffn[layer:layer + 1]
        if layer % 2 == 0:
            wa = (w_in_e, g_qa3, g_kva3, w_qb_e, w_ukT)
            lat, latb, kpe, kpeb, z, qlat, qpe = _even_in(xnp, g_mix[layer:layer + 1], *wa, cos_p, sin_p, li)
            y_mla = _attn_prompt(qlat, qpe, latb, kpeb, w_uvT, li)
            y_pool = _pool_prompt(z, w_pool_b, pool_scale3, li)
            xp, xnp = _proj_res_norm([y_mla, y_pool], w_out_a_b, xp, g_next, li)
            lat_p.append(lat.reshape(BATCH, SEQ, KV_RANK))
            kpe_p.append(kpe.reshape(BATCH, SEQ, ROPE_DIM))
            pool_p.append(z.reshape(BATCH, SEQ, POOL_WIDTH)[:, SEQ - POOL_KEEP:])

            lat, latb, kpe, kpeb, z, qlat, qpe = _even_in(xns, g_mix[layer:layer + 1], *wa, cos_s, sin_s, li)
            y_mla = _attn_sample(qlat, qpe, cache_mla_latent, krope_t, latb, kpeb, w_uvT, li)
            y_pool = _pool_sample(z, state_pool, w_pool_b, pool_scale3, li)
            xs, xns = _proj_res_norm([y_mla, y_pool], w_out_a_b, xs, g_next, li)
            lat_s.append(lat.reshape(DEC_BATCH, DEC_SEQ, KV_RANK))
            kpe_s.append(kpe.reshape(DEC_BATCH, DEC_SEQ, ROPE_DIM))
            pool_s.append(z.reshape(DEC_BATCH, DEC_SEQ, POOL_WIDTH)[:, DEC_SEQ - POOL_KEEP:])
        else:
            qfvg = _odd_in(xnp, w_in_c_b, lb_param, li)
            og, s_fin = _gla(qfvg, g_onorm3, None, li, BATCH, SEQ)
            xp, xnp = _proj_res_norm([og], w_out_c_b, xp, g_next, li, head_major=True)
            hg_p.append(s_fin)

            qfvg = _odd_in(xns, w_in_c_b, lb_param, li)
            og, s_fin = _gla(qfvg, g_onorm3, state_hgrn, li, DEC_BATCH, DEC_SEQ)
            xs, xns = _proj_res_norm([og], w_out_c_b, xs, g_next, li, head_major=True)
            hg_s.append(s_fin)

        final = layer == DEPTH - 1
        g_after = g_final[None, :] if final else g_mix[layer + 1:layer + 2]
        xn_dtype = F32 if final else BF16

        act, hla, hlb = _ffn_up(xnp, w_up, conv_w, conv_b3, None, layer, tm=SEQ, seg=SEQ)
        xp, xnp = _ffn_down(act, w_down_b, xp, g_after, layer, not final, xn_dtype)
        cv_p.append(jnp.concatenate([hla, hlb], axis=-1))

        act, hla, hlb = _ffn_up(xns, w_up, conv_w, conv_b3, state_ffn_conv, layer,
                                tm=FFN_UP_TM_SAMPLE, seg=DEC_SEQ)
        xs, xns = _ffn_down(act, w_down_b, xs, g_after, layer, not final, xn_dtype)
        cv_s.append(jnp.concatenate([hla, hlb], axis=-1))

    return (xnp.reshape(BATCH, SEQ, D_MODEL), xns.reshape(DEC_BATCH, DEC_SEQ, D_MODEL),
            jnp.stack(lat_p), jnp.stack(kpe_p), jnp.stack(pool_p), jnp.stack(hg_p), jnp.stack(cv_p),
            jnp.stack(lat_s), jnp.stack(kpe_s), jnp.stack(pool_s), jnp.stack(hg_s), jnp.stack(cv_s))
```

```python
import functools

import jax
import jax.numpy as jnp
import numpy as np
from jax import lax
from jax.experimental import pallas as pl
from jax.experimental.pallas import tpu as pltpu

D_MODEL = 2048
BATCH = 8
SEQ = 2048
DEPTH = 4
DEC_BATCH = 32
DEC_SEQ = 64
PAST_LEN = 4096
CHUNK = 64
MLA_HEADS = 8
Q_RANK = 512
KV_RANK = 512
NOPE_DIM = 128
ROPE_DIM = 64
V_DIM = 128
ROPE_THETA = 10000.0
MLA_SCALE = (NOPE_DIM + ROPE_DIM) ** -0.5
LOG2_E = 1.4426950408889634
Q_SCALE = MLA_SCALE * LOG2_E
MASK_NEG = -1e30
POOL_WINDOWS = (2, 4, 8, 16)
POOL_WIDTH = D_MODEL - MLA_HEADS * V_DIM
POOL_GROUP_DIM = POOL_WIDTH // len(POOL_WINDOWS)
POOL_KEEP = max(POOL_WINDOWS) - 1
C_HEADS = 16
HEAD_F = 128
HEAD_I = D_MODEL // C_HEADS
F_MIN = 1e-30
D_FF = 5632
CONV_W = 3
EPS = 1e-6

F32 = jnp.float32
BF16 = jnp.bfloat16

LANES = 128
SUBLANES = 8
VMEM_LIMIT = 56 * 1024 * 1024

IN_EVEN_PAD = Q_RANK + KV_RANK + POOL_WIDTH + LANES
Q_NOPE_ALL = MLA_HEADS * NOPE_DIM
GLA_CHUNK = 64
GLA_SUB = SUBLANES
GLA_HEADS_PER_ITER = 16
GLA_STEP_ROWS = 64
FFN_TN = 512

EVEN_IN_TM = 256
ATTN_TQ = 256
ATTN_SAMPLE_TK = 2048
POOL_TP = 512
PROJ_TM = 512
ODD_IN_TM = 1024
FFN_UP_TM_SAMPLE = 512
FFN_DOWN_TM = 1024
FFN_DOWN_NK = 11


def _params(*sem):
    return pltpu.CompilerParams(dimension_semantics=sem, vmem_limit_bytes=VMEM_LIMIT)


def _dot(a, b):
    return jnp.dot(a, b, preferred_element_type=F32)


def _dot_nt(a, b):
    return lax.dot_general(a, b, (((1,), (1,)), ((), ())), preferred_element_type=F32)


def _dot_tn(a, b):
    return lax.dot_general(a, b, (((0,), (0,)), ((), ())), preferred_element_type=F32)


def _rms(x, g):
    return x * lax.rsqrt(jnp.mean(x * x, axis=-1, keepdims=True) + EPS) * g


def _rope(s, cos, sin, first_half):
    swapped = jnp.where(first_half, pltpu.roll(s, 96, 1), pltpu.roll(s, 32, 1))
    return s * cos + swapped * sin


def _even_in_kernel(x_ref, gin_ref, w_ref, gqa_ref, gkva_ref, wqb_ref, wuk_ref, cos_ref, sin_ref,
                    lat_ref, latb_ref, kpe_ref, kpeb_ref, z_ref, qlat_ref, qpe_ref):
    tm = x_ref.shape[0]
    xn = x_ref[...] if x_ref.dtype == BF16 else _rms(x_ref[...], gin_ref[...]).astype(BF16)
    acc = _dot(xn, w_ref[...])
    o1, o2, o3 = Q_RANK, Q_RANK + KV_RANK, Q_RANK + KV_RANK + POOL_WIDTH
    z_ref[...] = acc[:, o2:o3]
    lat = _rms(acc[:, o1:o2], gkva_ref[...])
    lat_ref[...] = lat
    latb_ref[...] = lat.astype(BF16)
    cos = cos_ref[...]
    sin = sin_ref[...]
    lane = lax.broadcasted_iota(jnp.int32, (tm, LANES), 1)
    first_half = (lane % ROPE_DIM) < (ROPE_DIM // 2)
    kr = _rope(acc[:, o3:o3 + LANES], cos, sin, first_half)
    kpe_ref[...] = kr[:, :ROPE_DIM]
    kpeb_ref[...] = kr[:, :ROPE_DIM].astype(BF16)
    cqn = _rms(acc[:, :o1], gqa_ref[...]).astype(BF16)
    q = _dot(cqn, wqb_ref[...])
    for s in range(MLA_HEADS // 2):
        lo = Q_NOPE_ALL + LANES * s
        r = (_rope(q[:, lo:lo + LANES], cos, sin, first_half) * Q_SCALE).astype(BF16)
        qpe_ref[2 * s] = r[:, :ROPE_DIM]
        qpe_ref[2 * s + 1] = r[:, ROPE_DIM:]
    for h in range(MLA_HEADS):
        ql = _dot(q[:, NOPE_DIM * h:NOPE_DIM * (h + 1)].astype(BF16), wuk_ref[h])
        qlat_ref[h] = (ql * Q_SCALE).astype(BF16)


def _even_in(x, g_in, w_in, g_qa, g_kva, w_qb, w_ukT, cos, sin, li, tm=EVEN_IN_TM):
    T = x.shape[0]
    row = lambda i: (i, 0)
    fix2 = lambda i: (0, 0)
    out_shape = (
        jax.ShapeDtypeStruct((T, KV_RANK), F32), jax.ShapeDtypeStruct((T, KV_RANK), BF16),
        jax.ShapeDtypeStruct((T, ROPE_DIM), F32), jax.ShapeDtypeStruct((T, ROPE_DIM), BF16),
        jax.ShapeDtypeStruct((T, POOL_WIDTH), F32),
        jax.ShapeDtypeStruct((MLA_HEADS, T, KV_RANK), BF16),
        jax.ShapeDtypeStruct((MLA_HEADS, T, ROPE_DIM), BF16),
    )
    return pl.pallas_call(
        _even_in_kernel,
        out_shape=out_shape,
        grid=(T // tm,),
        in_specs=[
            pl.BlockSpec((tm, D_MODEL), row),
            pl.BlockSpec((1, D_MODEL), fix2),
            pl.BlockSpec((None, D_MODEL, IN_EVEN_PAD), lambda i: (li, 0, 0)),
            pl.BlockSpec((None, 1, Q_RANK), lambda i: (li, 0, 0)),
            pl.BlockSpec((None, 1, KV_RANK), lambda i: (li, 0, 0)),
            pl.BlockSpec((None, Q_RANK, MLA_HEADS * (NOPE_DIM + ROPE_DIM)), lambda i: (li, 0, 0)),
            pl.BlockSpec((None, MLA_HEADS, NOPE_DIM, KV_RANK), lambda i: (li, 0, 0, 0)),
            pl.BlockSpec((tm, LANES), row),
            pl.BlockSpec((tm, LANES), row),
        ],
        out_specs=(
            pl.BlockSpec((tm, KV_RANK), row), pl.BlockSpec((tm, KV_RANK), row),
            pl.BlockSpec((tm, ROPE_DIM), row), pl.BlockSpec((tm, ROPE_DIM), row),
            pl.BlockSpec((tm, POOL_WIDTH), row),
            pl.BlockSpec((MLA_HEADS, tm, KV_RANK), lambda i: (0, i, 0)),
            pl.BlockSpec((MLA_HEADS, tm, ROPE_DIM), lambda i: (0, i, 0)),
        ),
        compiler_params=_params("parallel"),
        name="even_in",
    )(x, g_in, w_in, g_qa, g_kva, w_qb, w_ukT, cos, sin)


ATTN_HEAD_GROUP = 4
ATTN_SAMPLE_SUB_KEYS = 512


def _lane_tile(x, width):
    if width <= LANES:
        return x[:, :width]
    return jnp.concatenate([x] * (width // LANES), axis=1)


def _attn_keys_step(qlat_ref, qpe_ref, k, kp, m_ref, l_ref, acc_ref, tq, group, visible=None,
                    kp_transposed=False):
    tk = k.shape[0]
    n = group * tq

    def scores(g):
        ql = qlat_ref[g * group:(g + 1) * group].reshape(n, KV_RANK)
        qp = qpe_ref[g * group:(g + 1) * group].reshape(n, ROPE_DIM)
        s = _dot_nt(ql, k) + (_dot(qp, kp) if kp_transposed else _dot_nt(qp, kp))
        return s if visible is None else jnp.where(visible, s, MASK_NEG)

    s_next = scores(0)
    for g in range(MLA_HEADS // group):
        s = s_next
        if g + 1 < MLA_HEADS // group:
            s_next = scores(g + 1)
        rows = pl.ds(g * n, n)
        m_prev = m_ref[rows, :]
        m_new = jnp.maximum(m_prev, jnp.max(s, axis=1, keepdims=True))
        alpha = jnp.exp2(m_prev - m_new)
        p = jnp.exp2(s - _lane_tile(m_new, tk))
        l_ref[rows, :] = alpha * l_ref[rows, :] + jnp.sum(p, axis=1, keepdims=True)
        acc_ref[rows, :] = _lane_tile(alpha, KV_RANK) * acc_ref[rows, :] + _dot(p.astype(BF16), k)
        m_ref[rows, :] = m_new


def _attn_init(m_ref, l_ref, acc_ref):
    m_ref[...] = jnp.full(m_ref.shape, MASK_NEG, F32)
    l_ref[...] = jnp.zeros(l_ref.shape, F32)
    acc_ref[...] = jnp.zeros(acc_ref.shape, F32)


def _attn_finish(wuv_ref, y_ref, l_ref, acc_ref, tq):
    for h in range(MLA_HEADS):
        rows = pl.ds(h * tq, tq)
        o = (acc_ref[rows, :] / _lane_tile(l_ref[rows, :], KV_RANK)).astype(BF16)
        y_ref[:, V_DIM * h:V_DIM * (h + 1)] = _dot(o, wuv_ref[h]).astype(BF16)


def _attn_prompt_kernel(qi_ref, kj_ref, qlat_ref, qpe_ref, k_ref, kp_ref, wuv_ref, y_ref,
                        m_ref, l_ref, acc_ref, *, tq):
    p = pl.program_id(1)
    i = qi_ref[p]
    j = kj_ref[p]

    @pl.when(j == 0)
    def _():
        _attn_init(m_ref, l_ref, acc_ref)

    @pl.when(j < i)
    def _():
        _attn_keys_step(qlat_ref, qpe_ref, k_ref[...], kp_ref[...], m_ref, l_ref, acc_ref, tq,
                        ATTN_HEAD_GROUP)

    @pl.when(j == i)
    def _():
        n = ATTN_HEAD_GROUP * tq
        row = lax.broadcasted_iota(jnp.int32, (n, tq), 0) % tq
        col = lax.broadcasted_iota(jnp.int32, (n, tq), 1)
        visible = col // CHUNK <= row // CHUNK
        _attn_keys_step(qlat_ref, qpe_ref, k_ref[...], kp_ref[...], m_ref, l_ref, acc_ref, tq,
                        ATTN_HEAD_GROUP, visible)
        _attn_finish(wuv_ref, y_ref, l_ref, acc_ref, tq)


def _attn_prompt(qlat, qpe, latb, kpeb, w_uvT, li, tq=ATTN_TQ):
    assert tq % CHUNK == 0
    T = latb.shape[0]
    nq = SEQ // tq
    pairs = [(i, j) for i in range(nq) for j in range(i + 1)]
    q_of = jnp.asarray([i for i, _ in pairs], jnp.int32)
    k_of = jnp.asarray([j for _, j in pairs], jnp.int32)
    qmap = lambda b, p, qi, kj: (0, b * nq + qi[p], 0)
    kmap = lambda b, p, qi, kj: (b * nq + kj[p], 0)
    rows = MLA_HEADS * tq
    return pl.pallas_call(
        functools.partial(_attn_prompt_kernel, tq=tq),
        out_shape=jax.ShapeDtypeStruct((T, MLA_HEADS * V_DIM), BF16),
        grid_spec=pltpu.PrefetchScalarGridSpec(
            num_scalar_prefetch=2,
            grid=(BATCH, len(pairs)),
            in_specs=[
                pl.BlockSpec((MLA_HEADS, tq, KV_RANK), qmap),
                pl.BlockSpec((MLA_HEADS, tq, ROPE_DIM), qmap),
                pl.BlockSpec((tq, KV_RANK), kmap),
                pl.BlockSpec((tq, ROPE_DIM), kmap),
                pl.BlockSpec((None, MLA_HEADS, KV_RANK, V_DIM), lambda b, p, qi, kj: (li, 0, 0, 0)),
            ],
            out_specs=pl.BlockSpec((tq, MLA_HEADS * V_DIM), lambda b, p, qi, kj: (b * nq + qi[p], 0)),
            scratch_shapes=[pltpu.VMEM((rows, LANES), F32), pltpu.VMEM((rows, LANES), F32),
                            pltpu.VMEM((rows, KV_RANK), F32)],
        ),
        compiler_params=_params("parallel", "arbitrary"),
        name="attn_prompt",
    )(q_of, k_of, qlat, qpe, latb, kpeb, w_uvT)


def _attn_sample_kernel(qlat_ref, qpe_ref, ck_ref, ckp_ref, nk_ref, nkp_ref, wuv_ref, y_ref,
                        m_ref, l_ref, acc_ref):
    j = pl.program_id(1)
    last = pl.num_programs(1) - 1

    @pl.when(j == 0)
    def _():
        _attn_init(m_ref, l_ref, acc_ref)

    def step(k, kp, kp_transposed=False):
        _attn_keys_step(qlat_ref, qpe_ref, k, kp, m_ref, l_ref, acc_ref, DEC_SEQ, MLA_HEADS,
                        kp_transposed=kp_transposed)

    @pl.when(j < last)
    def _():
        for c in range(0, ck_ref.shape[0], ATTN_SAMPLE_SUB_KEYS):
            keys = pl.ds(c, ATTN_SAMPLE_SUB_KEYS)
            step(ck_ref[keys, :].astype(BF16), ckp_ref[:, keys].astype(BF16), kp_transposed=True)

    @pl.when(j == last)
    def _():
        step(nk_ref[...], nkp_ref[...])
        _attn_finish(wuv_ref, y_ref, l_ref, acc_ref, DEC_SEQ)


def _attn_sample(qlat, qpe, cache_lat, cache_kpe_t, latb, kpeb, w_uvT, li, tk=ATTN_SAMPLE_TK):
    assert PAST_LEN % CHUNK == 0 and DEC_SEQ <= CHUNK and PAST_LEN % tk == 0
    T = latb.shape[0]
    nc = PAST_LEN // tk
    qmap = lambda b, j: (0, b, 0)
    cmap = lambda b, j: (li, b, jnp.minimum(j, nc - 1), 0)
    rows = MLA_HEADS * DEC_SEQ
    return pl.pallas_call(
        _attn_sample_kernel,
        out_shape=jax.ShapeDtypeStruct((T, MLA_HEADS * V_DIM), BF16),
        grid=(DEC_BATCH, nc + 1),
        in_specs=[
            pl.BlockSpec((MLA_HEADS, DEC_SEQ, KV_RANK), qmap),
            pl.BlockSpec((MLA_HEADS, DEC_SEQ, ROPE_DIM), qmap),
            pl.BlockSpec((None, None, tk, KV_RANK), cmap),
            pl.BlockSpec((None, None, ROPE_DIM, tk), lambda b, j: (li, b, 0, jnp.minimum(j, nc - 1))),
            pl.BlockSpec((DEC_SEQ, KV_RANK), lambda b, j: (b, 0)),
            pl.BlockSpec((DEC_SEQ, ROPE_DIM), lambda b, j: (b, 0)),
            pl.BlockSpec((None, MLA_HEADS, KV_RANK, V_DIM), lambda b, j: (li, 0, 0, 0)),
        ],
        out_specs=pl.BlockSpec((DEC_SEQ, MLA_HEADS * V_DIM), lambda b, j: (b, 0)),
        scratch_shapes=[pltpu.VMEM((rows, LANES), F32), pltpu.VMEM((rows, LANES), F32),
                        pltpu.VMEM((rows, KV_RANK), F32)],
        compiler_params=_params("parallel", "arbitrary"),
        name="attn_sample",
    )(qlat, qpe, cache_lat, cache_kpe_t, latb, kpeb, w_uvT)


POOL_HALO = 2 * SUBLANES


def _pool_kernel(*refs, tp, from_state):
    if from_state:
        z_ref, past_ref, w_ref, scale_ref, y_ref, ext_ref = refs
        ext_ref[0:1, :] = jnp.zeros((1, POOL_WIDTH), F32)
        ext_ref[1:POOL_HALO, :] = past_ref[...]
    else:
        z_ref, w_ref, scale_ref, y_ref, ext_ref = refs
        t = pl.program_id(1)

        @pl.when(t == 0)
        def _():
            ext_ref[0:POOL_HALO, :] = jnp.zeros((POOL_HALO, POOL_WIDTH), F32)

    ext_ref[POOL_HALO:POOL_HALO + tp, :] = z_ref[...]
    for gi, w in enumerate(POOL_WINDOWS):
        lanes = slice(POOL_GROUP_DIM * gi, POOL_GROUP_DIM * (gi + 1))
        tok = ext_ref[POOL_HALO:POOL_HALO + tp, lanes]
        acc = tok
        for d in range(1, w):
            acc = acc + ext_ref[POOL_HALO - d:POOL_HALO - d + tp, lanes]
        if from_state:
            mean = acc / float(w)
        else:
            pos = t * tp + lax.broadcasted_iota(jnp.int32, (tp, 1), 0)
            mean = acc / jnp.minimum(pos + 1, w).astype(F32)
        p = (mean - tok).astype(BF16)
        y_ref[:, lanes] = (_dot(p, w_ref[gi]) * scale_ref[:, lanes]).astype(BF16)
    if not from_state:
        ext_ref[0:POOL_HALO, :] = ext_ref[tp:tp + POOL_HALO, :]


def _pool_prompt(z, w_pool, pool_scale, li, tp=POOL_TP):
    T = z.shape[0]
    nt = SEQ // tp
    return pl.pallas_call(
        functools.partial(_pool_kernel, tp=tp, from_state=False),
        out_shape=jax.ShapeDtypeStruct((T, POOL_WIDTH), BF16),
        grid=(BATCH, nt),
        in_specs=[
            pl.BlockSpec((tp, POOL_WIDTH), lambda b, t: (b * nt + t, 0)),
            pl.BlockSpec((None, len(POOL_WINDOWS), POOL_GROUP_DIM, POOL_GROUP_DIM), lambda b, t: (li, 0, 0, 0)),
            pl.BlockSpec((None, 1, POOL_WIDTH), lambda b, t: (li, 0, 0)),
        ],
        out_specs=pl.BlockSpec((tp, POOL_WIDTH), lambda b, t: (b * nt + t, 0)),
        scratch_shapes=[pltpu.VMEM((POOL_HALO + tp, POOL_WIDTH), F32)],
        compiler_params=_params("arbitrary", "arbitrary"),
        name="pool_prompt",
    )(z, w_pool, pool_scale)


def _pool_sample(z, state_pool, w_pool, pool_scale, li):
    T = z.shape[0]
    return pl.pallas_call(
        functools.partial(_pool_kernel, tp=DEC_SEQ, from_state=True),
        out_shape=jax.ShapeDtypeStruct((T, POOL_WIDTH), BF16),
        grid=(DEC_BATCH,),
        in_specs=[
            pl.BlockSpec((DEC_SEQ, POOL_WIDTH), lambda b: (b, 0)),
            pl.BlockSpec((None, None, POOL_KEEP, POOL_WIDTH), lambda b: (li, b, 0, 0)),
            pl.BlockSpec((None, len(POOL_WINDOWS), POOL_GROUP_DIM, POOL_GROUP_DIM), lambda b: (li, 0, 0, 0)),
            pl.BlockSpec((None, 1, POOL_WIDTH), lambda b: (li, 0, 0)),
        ],
        out_specs=pl.BlockSpec((DEC_SEQ, POOL_WIDTH), lambda b: (b, 0)),
        scratch_shapes=[pltpu.VMEM((POOL_HALO + DEC_SEQ, POOL_WIDTH), F32)],
        compiler_params=_params("parallel"),
        name="pool_sample",
    )(z, state_pool, w_pool, pool_scale)


def _proj_res_norm_kernel(*refs, n_a, head_major, emit_x):
    a_refs = refs[:n_a]
    w_refs = refs[n_a:2 * n_a]
    x_ref, g_ref = refs[2 * n_a:2 * n_a + 2]
    outs = refs[2 * n_a + 2:]
    if head_major:
        (a_ref,), (w_ref,) = a_refs, w_refs
        xo_ref, xn_ref, cat_ref = outs
        for h in range(C_HEADS):
            cat_ref[:, HEAD_I * h:HEAD_I * (h + 1)] = a_ref[h]
        acc = _dot(cat_ref[...], w_ref[...])
    else:
        xo_ref, xn_ref = outs if emit_x else (None, outs[0])
        acc = _dot(a_refs[0][...], w_refs[0][...])
        for a_ref, w_ref in zip(a_refs[1:], w_refs[1:]):
            acc = acc + _dot(a_ref[...], w_ref[...])
    xnew = x_ref[...] + acc
    if xo_ref is not None:
        xo_ref[...] = xnew
    xn_ref[...] = _rms(xnew, g_ref[...]).astype(xn_ref.dtype)


def _proj_res_norm(a_list, w, x, g, li, head_major=False, tm=PROJ_TM):
    T = x.shape[0]
    row = lambda i: (i, 0)
    n_a = len(a_list)
    in_specs = []
    for a in a_list:
        if head_major:
            in_specs.append(pl.BlockSpec((C_HEADS, tm, HEAD_I), lambda i: (0, i, 0)))
        else:
            in_specs.append(pl.BlockSpec((tm, a.shape[1]), row))
    for kb, a in enumerate(a_list):
        kdim = C_HEADS * HEAD_I if head_major else a.shape[1]
        in_specs.append(pl.BlockSpec((None, kdim, D_MODEL), lambda i, kb=kb: (li, kb, 0)))
    in_specs += [pl.BlockSpec((tm, D_MODEL), row), pl.BlockSpec((1, D_MODEL), lambda i: (0, 0))]
    scratch = [pltpu.VMEM((tm, D_MODEL), BF16)] if head_major else []
    return pl.pallas_call(
        functools.partial(_proj_res_norm_kernel, n_a=n_a, head_major=head_major, emit_x=True),
        out_shape=(jax.ShapeDtypeStruct((T, D_MODEL), F32), jax.ShapeDtypeStruct((T, D_MODEL), BF16)),
        grid=(T // tm,),
        in_specs=in_specs,
        out_specs=(pl.BlockSpec((tm, D_MODEL), row), pl.BlockSpec((tm, D_MODEL), row)),
        scratch_shapes=scratch,
        compiler_params=_params("parallel"),
        name="proj_res_norm",
    )(*a_list, *([w] * n_a), x, g)


def _odd_in_kernel(xn_ref, w_ref, lbp_ref, o_ref, *, li):
    s = pl.program_id(0)

    def emit(epilogue):
        out = epilogue(_dot(xn_ref[...], w_ref[...]))
        for hd in range(C_HEADS):
            o_ref[hd] = out[:, HEAD_I * hd:HEAD_I * (hd + 1)]

    def silu(h):
        half = 0.5 * h
        return half + half * jnp.tanh(half)

    def forget(h):
        p = lbp_ref[...]
        e = jnp.exp(p - jnp.max(p, axis=0, keepdims=True))
        sm = e / jnp.sum(e, axis=0, keepdims=True)
        lb = jnp.clip(jnp.sum(sm[:li + 1], axis=0, keepdims=True) - sm[0:1], 0.0, 1.0)
        gain = 0.5 * (1.0 - lb)
        return (lb + gain) + gain * jnp.tanh(0.5 * h)

    for sec, epilogue in enumerate((silu, forget, lambda h: h, silu)):
        @pl.when(s == sec)
        def _(epilogue=epilogue):
            emit(epilogue)


def _odd_in(xn, w_in_c, lb_param, li, tm=ODD_IN_TM):
    T = xn.shape[0]
    return pl.pallas_call(
        functools.partial(_odd_in_kernel, li=li),
        out_shape=jax.ShapeDtypeStruct((4, C_HEADS, T, HEAD_I), F32),
        grid=(4, T // tm),
        in_specs=[
            pl.BlockSpec((tm, D_MODEL), lambda s, i: (i, 0)),
            pl.BlockSpec((None, D_MODEL, D_MODEL), lambda s, i: (li, 0, s)),
            pl.BlockSpec(lb_param.shape, lambda s, i: (0, 0)),
        ],
        out_specs=pl.BlockSpec((None, C_HEADS, tm, HEAD_I), lambda s, i: (s, 0, i, 0)),
        compiler_params=_params("parallel", "parallel"),
        name="odd_in",
    )(xn, w_in_c, lb_param)


def _gla_kernel(*refs, has_s0):
    if has_s0:
        q_ref, f_ref, v_ref, gs_ref, gon_ref, cls_ref, s0_ref, og_ref, sout_ref, st_ref = refs
    else:
        q_ref, f_ref, v_ref, gs_ref, gon_ref, cls_ref, og_ref, sout_ref, st_ref = refs
    C = GLA_CHUNK
    n = pl.program_id(1)
    last = pl.num_programs(1) - 1

    @pl.when(n == 0)
    def _():
        if has_s0:
            st_ref[...] = s0_ref[...]
        else:
            st_ref[...] = jnp.zeros(st_ref.shape, F32)

    cls = cls_ref[...]
    tri = (cls >= 0).astype(BF16)
    ones = jnp.ones((HEAD_F, LANES), BF16)
    gon = gon_ref[...]
    near = [cls == delta for delta in range(GLA_SUB)]
    levels = [(GLA_SUB << i, cls == GLA_SUB + i) for i in range((C // GLA_SUB).bit_length() - 1)]

    def bcast_row(a, period, r):
        a3 = a.reshape(C // period, period, a.shape[-1])
        return jnp.broadcast_to(a3[:, r:r + 1, :], a3.shape).reshape(C, a.shape[-1])

    def shift_in_group(a, delta):
        if delta == 0:
            return a
        a3 = a.reshape(C // GLA_SUB, GLA_SUB, a.shape[-1])
        return pltpu.roll(a3, delta, 1).reshape(C, a.shape[-1])

    def split3(g):
        g1 = g.astype(BF16)
        r1 = g - g1.astype(F32)
        g2 = r1.astype(BF16)
        return g1, g2, (r1 - g2.astype(F32)).astype(BF16)

    def head_group(hg, row0):
        hs = [hg * GLA_HEADS_PER_ITER + u for u in range(GLA_HEADS_PER_ITER)]
        U = range(len(hs))
        rows = pl.ds(row0, C)
        q = [q_ref[h, rows, :] for h in hs]
        f = [f_ref[h, rows, :] for h in hs]
        v_bf = [v_ref[h, rows, :].astype(BF16) for h in hs]
        fc = [jnp.maximum(x, F_MIN) for x in f]
        kk = [1.0 - x for x in f]
        gp = [split3(jnp.log2(x)) for x in fc]
        b = [_dot(tri, gp[u][0]) + _dot(tri, gp[u][1]) + _dot(tri, gp[u][2]) for u in U]
        st = [st_ref[h] for h in hs]
        o = [_dot((q[u] * jnp.exp2(b[u])).astype(BF16), st[u].astype(BF16)) for u in U]
        b_last = [x[C - 1:C, :] for x in b]
        k_dec = [(kk[u] * jnp.exp2(b_last[u] - b[u])).astype(BF16) for u in U]
        for u in U:
            d_rows = jnp.broadcast_to(jnp.exp2(b_last[u]), (LANES, HEAD_F)).T
            st_ref[hs[u]] = st[u] * d_rows + _dot_tn(k_dec[u], v_bf[u])

        a = [jnp.zeros((C, C), F32) for _ in U]
        for m, keep in levels:
            for u in U:
                d = b[u] - bcast_row(b[u], 2 * m, m - 1)
                qt = (q[u] * jnp.exp2(jnp.minimum(d, 0.0))).astype(BF16)
                kt = (kk[u] * jnp.exp2(jnp.minimum(-d, 0.0))).astype(BF16)
                a[u] = jnp.where(keep, _dot_nt(qt, kt), a[u])
        kd = list(kk)
        for delta in range(GLA_SUB):
            for u in U:
                if delta > 0:
                    kd[u] = shift_in_group(kd[u], 1) * fc[u]
                p = (q[u] * kd[u]).astype(BF16)
                a[u] = jnp.where(near[delta], _dot(p, ones)[:, :C], a[u])
        for u in U:
            y = _rms(o[u] + _dot(a[u].astype(BF16), v_bf[u]), gon) * gs_ref[hs[u], rows, :]
            og_ref[hs[u], rows, :] = y.astype(BF16)
        return row0

    def chunk(c, carry):
        lax.fori_loop(0, C_HEADS // GLA_HEADS_PER_ITER, head_group, pl.multiple_of(c * C, C))
        return carry

    lax.fori_loop(0, q_ref.shape[1] // C, chunk, 0)

    @pl.when(n == last)
    def _():
        sout_ref[...] = st_ref[...]


def _gla_pair_classes():
    t = np.arange(GLA_CHUNK)[:, None]
    s = np.arange(GLA_CHUNK)[None, :]
    x = (t // GLA_SUB) ^ (s // GLA_SUB)
    far = GLA_SUB + np.floor(np.log2(np.maximum(x, 1))).astype(np.int64)
    return jnp.asarray(np.where(s > t, -1, np.where(x == 0, t - s, far)), jnp.int32)


def _gla(qfvg, g_onorm, s0, li, n_streams, seq):
    T = qfvg.shape[2]
    C = min(GLA_STEP_ROWS, seq)
    nc = seq // C
    has_s0 = s0 is not None
    sec = lambda k: pl.BlockSpec((None, C_HEADS, C, HEAD_F), lambda b, n, k=k: (k, 0, b * nc + n, 0))
    in_specs = [sec(0), sec(1), sec(2), sec(3),
                pl.BlockSpec((None, 1, HEAD_I), lambda b, n: (li, 0, 0)),
                pl.BlockSpec((GLA_CHUNK, GLA_CHUNK), lambda b, n: (0, 0))]
    args = [qfvg, qfvg, qfvg, qfvg, g_onorm, _gla_pair_classes()]
    if has_s0:
        in_specs.append(pl.BlockSpec((None, None, C_HEADS, HEAD_F, HEAD_I), lambda b, n: (li, b, 0, 0, 0)))
        args.append(s0)
    return pl.pallas_call(
        functools.partial(_gla_kernel, has_s0=has_s0),
        out_shape=(jax.ShapeDtypeStruct((C_HEADS, T, HEAD_I), BF16),
                   jax.ShapeDtypeStruct((n_streams, C_HEADS, HEAD_F, HEAD_I), F32)),
        grid=(n_streams, nc),
        in_specs=in_specs,
        out_specs=(pl.BlockSpec((C_HEADS, C, HEAD_I), lambda b, n: (0, b * nc + n, 0)),
                   pl.BlockSpec((None, C_HEADS, HEAD_F, HEAD_I), lambda b, n: (b, 0, 0, 0))),
        scratch_shapes=[pltpu.VMEM((C_HEADS, HEAD_I, HEAD_F), F32)],
        compiler_params=_params("parallel", "arbitrary"),
        name="gla",
    )(*args)


FFN_PAD = SUBLANES
FFN_SUB_ROWS = 512


def _ffn_up_kernel(*refs, tm, seg, from_state, tiles_per_stream):
    if from_state:
        (xn_ref, wa_ref, wb_ref, cwa_ref, cwb_ref, cba_ref, cbb_ref, pa_ref, pb_ref,
         act_ref, hla_ref, hlb_ref, w_ref) = refs
        carry_ref = None
    else:
        (xn_ref, wa_ref, wb_ref, cwa_ref, cwb_ref, cba_ref, cbb_ref,
         act_ref, hla_ref, hlb_ref, w_ref, carry_ref) = refs
        pa_ref = pb_ref = None
    j = pl.program_id(0)
    i = pl.program_id(1)
    sub = min(FFN_SUB_ROWS, tm)
    assert (seg % sub == 0 and not from_state) or (sub % seg == 0 and sub > seg)
    halves = ((0, cwa_ref, cba_ref, pa_ref, hla_ref), (1, cwb_ref, cbb_ref, pb_ref, hlb_ref))

    @pl.when(i == 0)
    def _():
        w_ref[0] = wa_ref[...].astype(BF16)
        w_ref[1] = wb_ref[...].astype(BF16)

    def conv(front, h, cw, cb):
        n = h.shape[0]
        ext = jnp.concatenate([front, h], axis=0)
        return cb + (ext[FFN_PAD - 2:FFN_PAD - 2 + n] * cw[0:1] + ext[FFN_PAD - 1:FFN_PAD - 1 + n] * cw[1:2] +
                     h * cw[2:3])

    fronts = [None, None]
    if not from_state:
        @pl.when(i % tiles_per_stream == 0)
        def _():
            carry_ref[j] = jnp.zeros((2, FFN_PAD, FFN_TN), F32)
        fronts = [carry_ref[j, 0], carry_ref[j, 1]]

    for r0 in range(0, tm, sub):
        convs = []
        for hidx, cw_ref, cb_ref, past_ref, hl_ref in halves:
            h = _dot(xn_ref[r0:r0 + sub, :], w_ref[hidx])
            cw = cw_ref[...]
            cb = cb_ref[...]
            if seg >= sub:
                convs.append([conv(fronts[hidx], h, cw, cb)])
                fronts[hidx] = h[sub - FFN_PAD:sub]
                if (r0 + sub) % seg == 0:
                    hl_ref[(r0 + sub) // seg - 1] = h[sub - 2:sub]
            else:
                pieces = []
                for k in range(sub // seg):
                    s = r0 // seg + k
                    hk = h[k * seg:(k + 1) * seg]
                    front = jnp.concatenate([jnp.zeros((FFN_PAD - 2, FFN_TN), F32), past_ref[s]], axis=0)
                    pieces.append(conv(front, hk, cw, cb))
                    hl_ref[s] = hk[seg - 2:seg]
                convs.append(pieces)
        for n, (a, b) in enumerate(zip(*convs)):
            rows = a.shape[0]
            o = r0 + n * rows
            half = 0.5 * a
            act_ref[o:o + rows, :] = ((half + half * jnp.tanh(half)) * b).astype(BF16)

    if not from_state:
        carry_ref[j, 0] = fronts[0]
        carry_ref[j, 1] = fronts[1]


def _ffn_up(xn, w_up, conv_w, conv_b, conv_past, layer, tm, seg):
    T = xn.shape[0]
    tn = FFN_TN
    nj = D_FF // tn
    nseg = tm // seg
    from_state = conv_past is not None
    wmap_a = lambda j, i: (layer, 0, j)
    wmap_b = lambda j, i: (layer, 0, nj + j)
    in_specs = [
        pl.BlockSpec((tm, D_MODEL), lambda j, i: (i, 0)),
        pl.BlockSpec((None, D_MODEL, tn), wmap_a), pl.BlockSpec((None, D_MODEL, tn), wmap_b),
        pl.BlockSpec((None, CONV_W, tn), wmap_a), pl.BlockSpec((None, CONV_W, tn), wmap_b),
        pl.BlockSpec((None, 1, tn), wmap_a), pl.BlockSpec((None, 1, tn), wmap_b),
    ]
    args = [xn, w_up, w_up, conv_w, conv_w, conv_b, conv_b]
    scratch = [pltpu.VMEM((2, D_MODEL, tn), BF16)]
    if from_state:
        in_specs += [pl.BlockSpec((None, nseg, CONV_W - 1, tn), lambda j, i: (layer, i, 0, j)),
                     pl.BlockSpec((None, nseg, CONV_W - 1, tn), lambda j, i: (layer, i, 0, nj + j))]
        args += [conv_past, conv_past]
        tiles_per_stream = 1
    else:
        scratch.append(pltpu.VMEM((nj, 2, FFN_PAD, tn), F32))
        tiles_per_stream = SEQ // tm
    hl_shape = jax.ShapeDtypeStruct((T // seg, CONV_W - 1, D_FF), F32)
    hl_spec = pl.BlockSpec((nseg, CONV_W - 1, tn), lambda j, i: (i, 0, j))
    return pl.pallas_call(
        functools.partial(_ffn_up_kernel, tm=tm, seg=seg, from_state=from_state,
                          tiles_per_stream=tiles_per_stream),
        out_shape=(jax.ShapeDtypeStruct((T, D_FF), BF16), hl_shape, hl_shape),
        grid=(nj, T // tm),
        in_specs=in_specs,
        out_specs=(pl.BlockSpec((tm, tn), lambda j, i: (i, j)), hl_spec, hl_spec),
        scratch_shapes=scratch,
        compiler_params=_params("arbitrary", "arbitrary"),
        name="ffn_up",
    )(*args)


def _ffn_down_kernel(a_ref, w_ref, x_ref, g_ref, *outs, emit_x):
    if emit_x:
        acc_ref, xn_ref = outs
    else:
        xn_ref, acc_ref = outs
    k = pl.program_id(1)

    @pl.when(k == 0)
    def _():
        acc_ref[...] = x_ref[...]

    acc_ref[...] += _dot(a_ref[...], w_ref[...])

    @pl.when(k == pl.num_programs(1) - 1)
    def _():
        xn_ref[...] = _rms(acc_ref[...], g_ref[...]).astype(xn_ref.dtype)


def _ffn_down(act, w_down, x, g, layer, emit_x, xn_dtype, tm=FFN_DOWN_TM, nk=FFN_DOWN_NK):
    T = x.shape[0]
    tk = D_FF // nk
    row = lambda i, k: (i, 0)
    out_shape = [jax.ShapeDtypeStruct((T, D_MODEL), xn_dtype)]
    out_specs = [pl.BlockSpec((tm, D_MODEL), row)]
    scratch = [pltpu.VMEM((tm, D_MODEL), F32)]
    if emit_x:
        out_shape.insert(0, jax.ShapeDtypeStruct((T, D_MODEL), F32))
        out_specs.insert(0, pl.BlockSpec((tm, D_MODEL), row))
        scratch = []
    res = pl.pallas_call(
        functools.partial(_ffn_down_kernel, emit_x=emit_x),
        out_shape=tuple(out_shape),
        grid=(T // tm, nk),
        in_specs=[
            pl.BlockSpec((tm, tk), lambda i, k: (i, k)),
            pl.BlockSpec((None, tk, D_MODEL), lambda i, k: (layer, k, 0)),
            pl.BlockSpec((tm, D_MODEL), row),
            pl.BlockSpec((1, D_MODEL), lambda i, k: (0, 0)),
        ],
        out_specs=tuple(out_specs),
        scratch_shapes=scratch,
        compiler_params=_params("parallel", "arbitrary"),
        name="ffn_down",
    )(act, w_down, x, g)
    return res if emit_x else (None, res[0])


def _rope_tables(pos):
    inv = ROPE_THETA ** (-jnp.arange(0, ROPE_DIM, 2, dtype=F32) / ROPE_DIM)
    ang = pos.astype(F32)[:, None] * inv[None, :]
    cos, sin = jnp.cos(ang), jnp.sin(ang)
    reps = LANES // ROPE_DIM
    return (jnp.tile(jnp.concatenate([cos, cos], axis=-1), (1, reps)),
            jnp.tile(jnp.concatenate([-sin, sin], axis=-1), (1, reps)))


def kernel(x_prompt, x_sample, cache_mla_latent, cache_mla_krope, state_pool, state_hgrn, state_ffn_conv,
           g_mix, g_ffn, g_final, w_in_a, g_qa, w_qb, g_kva, w_uk, w_uv, w_pool, pool_scale, w_out_a,
           w_in_c, lb_param, g_onorm, w_out_c, w_up, conv_w, conv_b, w_down):
    n_even = w_in_a.shape[0]
    n_odd = w_in_c.shape[0]
    Tp, Ts = BATCH * SEQ, DEC_BATCH * DEC_SEQ

    o1, o2, o3 = Q_RANK, Q_RANK + KV_RANK, Q_RANK + KV_RANK + ROPE_DIM
    w_in_e = jnp.concatenate(
        [w_in_a[..., :o2], w_in_a[..., o3:], w_in_a[..., o2:o3],
         jnp.zeros((n_even, D_MODEL, LANES - ROPE_DIM), w_in_a.dtype)], axis=-1).astype(BF16)
    w_qb4 = w_qb.reshape(n_even, Q_RANK, MLA_HEADS, NOPE_DIM + ROPE_DIM)
    w_qb_e = jnp.concatenate(
        [w_qb4[..., :NOPE_DIM].reshape(n_even, Q_RANK, Q_NOPE_ALL),
         w_qb4[..., NOPE_DIM:].reshape(n_even, Q_RANK, MLA_HEADS * ROPE_DIM)], axis=-1).astype(BF16)
    w_ukT = jnp.transpose(w_uk, (0, 2, 3, 1)).astype(BF16)
    w_uvT = jnp.transpose(w_uv, (0, 2, 1, 3)).astype(BF16)
    w_pool_b = w_pool.astype(BF16)
    w_out_a_b = w_out_a.astype(BF16)
    w_in_c_b = w_in_c.astype(BF16)
    w_out_c_b = w_out_c.astype(BF16)
    w_down_b = w_down.astype(BF16)
    g_qa3 = g_qa[:, None, :]
    g_kva3 = g_kva[:, None, :]
    pool_scale3 = pool_scale[:, None, :]
    g_onorm3 = g_onorm[:, None, :]
    conv_b3 = conv_b[:, None, :]

    krope_t = jnp.swapaxes(cache_mla_krope, 2, 3)
    cos_p, sin_p = _rope_tables(jnp.tile(jnp.arange(SEQ), BATCH))
    cos_s, sin_s = _rope_tables(jnp.tile(PAST_LEN + jnp.arange(DEC_SEQ), DEC_BATCH))

    xp = x_prompt.reshape(Tp, D_MODEL)
    xs = x_sample.reshape(Ts, D_MODEL)
    xnp, xns = xp, xs

    lat_p, kpe_p, pool_p, hg_p, cv_p = [], [], [], [], []
    lat_s, kpe_s, pool_s, hg_s, cv_s = [], [], [], [], []
    for layer in range(DEPTH):
        li = layer // 2
        g_next = g_ffn[layer:layer + 1]
        if layer % 2 == 0:
            wa = (w_in_e, g_qa3, g_kva3, w_qb_e, w_ukT)
            lat, latb, kpe, kpeb, z, qlat, qpe = _even_in(xnp, g_mix[layer:layer + 1], *wa, cos_p, sin_p, li)
            y_mla = _attn_prompt(qlat, qpe, latb, kpeb, w_uvT, li)
            y_pool = _pool_prompt(z, w_pool_b, pool_scale3, li)
            xp, xnp = _proj_res_norm([y_mla, y_pool], w_out_a_b, xp, g_next, li)
            lat_p.append(lat.reshape(BATCH, SEQ, KV_RANK))
            kpe_p.append(kpe.reshape(BATCH, SEQ, ROPE_DIM))
            pool_p.append(z.reshape(BATCH, SEQ, POOL_WIDTH)[:, SEQ - POOL_KEEP:])

            lat, latb, kpe, kpeb, z, qlat, qpe = _even_in(xns, g_mix[layer:layer + 1], *wa, cos_s, sin_s, li)
            y_mla = _attn_sample(qlat, qpe, cache_mla_latent, krope_t, latb, kpeb, w_uvT, li)
            y_pool = _pool_sample(z, state_pool, w_pool_b, pool_scale3, li)
            xs, xns = _proj_res_norm([y_mla, y_pool], w_out_a_b, xs, g_next, li)
            lat_s.append(lat.reshape(DEC_BATCH, DEC_SEQ, KV_RANK))
            kpe_s.append(kpe.reshape(DEC_BATCH, DEC_SEQ, ROPE_DIM))
            pool_s.append(z.reshape(DEC_BATCH, DEC_SEQ, POOL_WIDTH)[:, DEC_SEQ - POOL_KEEP:])
        else:
            qfvg = _odd_in(xnp, w_in_c_b, lb_param, li)
            og, s_fin = _gla(qfvg, g_onorm3, None, li, BATCH, SEQ)
            xp, xnp = _proj_res_norm([og], w_out_c_b, xp, g_next, li, head_major=True)
            hg_p.append(s_fin)

            qfvg = _odd_in(xns, w_in_c_b, lb_param, li)
            og, s_fin = _gla(qfvg, g_onorm3, state_hgrn, li, DEC_BATCH, DEC_SEQ)
            xs, xns = _proj_res_norm([og], w_out_c_b, xs, g_next, li, head_major=True)
            hg_s.append(s_fin)

        final = layer == DEPTH - 1
        g_after = g_final[None, :] if final else g_mix[layer + 1:layer + 2]
        xn_dtype = F32 if final else BF16

        act, hla, hlb = _ffn_up(xnp, w_up, conv_w, conv_b3, None, layer, tm=SEQ, seg=SEQ)
        xp, xnp = _ffn_down(act, w_down_b, xp, g_after, layer, not final, xn_dtype)
        cv_p.append(jnp.concatenate([hla, hlb], axis=-1))

        act, hla, hlb = _ffn_up(xns, w_up, conv_w, conv_b3, state_ffn_conv, layer,
                                tm=FFN_UP_TM_SAMPLE, seg=DEC_SEQ)
        xs, xns = _ffn_down(act, w_down_b, xs, g_after, layer, not final, xn_dtype)
        cv_s.append(jnp.concatenate([hla, hlb], axis=-1))

    return (xnp.reshape(BATCH, SEQ, D_MODEL), xns.reshape(DEC_BATCH, DEC_SEQ, D_MODEL),
            jnp.stack(lat_p), jnp.stack(kpe_p), jnp.stack(pool_p), jnp.stack(hg_p), jnp.stack(cv_p),
            jnp.stack(lat_s), jnp.stack(kpe_s), jnp.stack(pool_s), jnp.stack(hg_s), jnp.stack(cv_s))
```

```python
import functools

import jax
import jax.numpy as jnp
import numpy as np
from jax import lax
from jax.experimental import pallas as pl
from jax.experimental.pallas import tpu as pltpu

D_MODEL = 2048
BATCH = 8
SEQ = 2048
DEPTH = 4
DEC_BATCH = 32
DEC_SEQ = 64
PAST_LEN = 4096
CHUNK = 64
MLA_HEADS = 8
Q_RANK = 512
KV_RANK = 512
NOPE_DIM = 128
ROPE_DIM = 64
V_DIM = 128
ROPE_THETA = 10000.0
MLA_SCALE = (NOPE_DIM + ROPE_DIM) ** -0.5
LOG2_E = 1.4426950408889634
Q_SCALE = MLA_SCALE * LOG2_E
MASK_NEG = -1e30
POOL_WINDOWS = (2, 4, 8, 16)
POOL_WIDTH = D_MODEL - MLA_HEADS * V_DIM
POOL_GROUP_DIM = POOL_WIDTH // len(POOL_WINDOWS)
POOL_KEEP = max(POOL_WINDOWS) - 1
C_HEADS = 16
HEAD_F = 128
HEAD_I = D_MODEL // C_HEADS
F_MIN = 1e-30
D_FF = 5632
CONV_W = 3
EPS = 1e-6

F32 = jnp.float32
BF16 = jnp.bfloat16

LANES = 128
SUBLANES = 8
VMEM_LIMIT = 56 * 1024 * 1024

IN_EVEN_PAD = Q_RANK + KV_RANK + POOL_WIDTH + LANES
Q_NOPE_ALL = MLA_HEADS * NOPE_DIM
GLA_CHUNK = 64
GLA_SUB = SUBLANES
GLA_HEADS_PER_ITER = 16
GLA_STEP_ROWS = 64
FFN_TN = 512

EVEN_IN_TM = 256
ATTN_TQ = 256
ATTN_SAMPLE_TK = 2048
POOL_TP = 512
PROJ_TM = 512
ODD_IN_TM = 1024
FFN_UP_TM_SAMPLE = 512
FFN_DOWN_TM = 1024
FFN_DOWN_NK = 11


def _params(*sem):
    return pltpu.CompilerParams(dimension_semantics=sem, vmem_limit_bytes=VMEM_LIMIT)


def _dot(a, b):
    return jnp.dot(a, b, preferred_element_type=F32)


def _dot_nt(a, b):
    return lax.dot_general(a, b, (((1,), (1,)), ((), ())), preferred_element_type=F32)


def _dot_tn(a, b):
    return lax.dot_general(a, b, (((0,), (0,)), ((), ())), preferred_element_type=F32)


def _rms(x, g):
    return x * lax.rsqrt(jnp.mean(x * x, axis=-1, keepdims=True) + EPS) * g


def _rope(s, cos, sin, first_half):
    swapped = jnp.where(first_half, pltpu.roll(s, 96, 1), pltpu.roll(s, 32, 1))
    return s * cos + swapped * sin


def _even_in_kernel(x_ref, gin_ref, w_ref, gqa_ref, gkva_ref, wqb_ref, wuk_ref, cos_ref, sin_ref,
                    lat_ref, latb_ref, kpe_ref, kpeb_ref, z_ref, qlat_ref, qpe_ref):
    tm = x_ref.shape[0]
    xn = x_ref[...] if x_ref.dtype == BF16 else _rms(x_ref[...], gin_ref[...]).astype(BF16)
    acc = _dot(xn, w_ref[...])
    o1, o2, o3 = Q_RANK, Q_RANK + KV_RANK, Q_RANK + KV_RANK + POOL_WIDTH
    z_ref[...] = acc[:, o2:o3]
    lat = _rms(acc[:, o1:o2], gkva_ref[...])
    lat_ref[...] = lat
    latb_ref[...] = lat.astype(BF16)
    cos = cos_ref[...]
    sin = sin_ref[...]
    lane = lax.broadcasted_iota(jnp.int32, (tm, LANES), 1)
    first_half = (lane % ROPE_DIM) < (ROPE_DIM // 2)
    kr = _rope(acc[:, o3:o3 + LANES], cos, sin, first_half)
    kpe_ref[...] = kr[:, :ROPE_DIM]
    kpeb_ref[...] = kr[:, :ROPE_DIM].astype(BF16)
    cqn = _rms(acc[:, :o1], gqa_ref[...]).astype(BF16)
    q = _dot(cqn, wqb_ref[...])
    for s in range(MLA_HEADS // 2):
        lo = Q_NOPE_ALL + LANES * s
        r = (_rope(q[:, lo:lo + LANES], cos, sin, first_half) * Q_SCALE).astype(BF16)
        qpe_ref[2 * s] = r[:, :ROPE_DIM]
        qpe_ref[2 * s + 1] = r[:, ROPE_DIM:]
    for h in range(MLA_HEADS):
        ql = _dot(q[:, NOPE_DIM * h:NOPE_DIM * (h + 1)].astype(BF16), wuk_ref[h])
        qlat_ref[h] = (ql * Q_SCALE).astype(BF16)


def _even_in(x, g_in, w_in, g_qa, g_kva, w_qb, w_ukT, cos, sin, li, tm=EVEN_IN_TM):
    T = x.shape[0]
    row = lambda i: (i, 0)
    fix2 = lambda i: (0, 0)
    out_shape = (
        jax.ShapeDtypeStruct((T, KV_RANK), F32), jax.ShapeDtypeStruct((T, KV_RANK), BF16),
        jax.ShapeDtypeStruct((T, ROPE_DIM), F32), jax.ShapeDtypeStruct((T, ROPE_DIM), BF16),
        jax.ShapeDtypeStruct((T, POOL_WIDTH), F32),
        jax.ShapeDtypeStruct((MLA_HEADS, T, KV_RANK), BF16),
        jax.ShapeDtypeStruct((MLA_HEADS, T, ROPE_DIM), BF16),
    )
    return pl.pallas_call(
        _even_in_kernel,
        out_shape=out_shape,
        grid=(T // tm,),
        in_specs=[
            pl.BlockSpec((tm, D_MODEL), row),
            pl.BlockSpec((1, D_MODEL), fix2),
            pl.BlockSpec((None, D_MODEL, IN_EVEN_PAD), lambda i: (li, 0, 0)),
            pl.BlockSpec((None, 1, Q_RANK), lambda i: (li, 0, 0)),
            pl.BlockSpec((None, 1, KV_RANK), lambda i: (li, 0, 0)),
            pl.BlockSpec((None, Q_RANK, MLA_HEADS * (NOPE_DIM + ROPE_DIM)), lambda i: (li, 0, 0)),
            pl.BlockSpec((None, MLA_HEADS, NOPE_DIM, KV_RANK), lambda i: (li, 0, 0, 0)),
            pl.BlockSpec((tm, LANES), row),
            pl.BlockSpec((tm, LANES), row),
        ],
        out_specs=(
            pl.BlockSpec((tm, KV_RANK), row), pl.BlockSpec((tm, KV_RANK), row),
            pl.BlockSpec((tm, ROPE_DIM), row), pl.BlockSpec((tm, ROPE_DIM), row),
            pl.BlockSpec((tm, POOL_WIDTH), row),
            pl.BlockSpec((MLA_HEADS, tm, KV_RANK), lambda i: (0, i, 0)),
            pl.BlockSpec((MLA_HEADS, tm, ROPE_DIM), lambda i: (0, i, 0)),
        ),
        compiler_params=_params("parallel"),
        name="even_in",
    )(x, g_in, w_in, g_qa, g_kva, w_qb, w_ukT, cos, sin)


ATTN_HEAD_GROUP = 4
ATTN_SAMPLE_SUB_KEYS = 512


def _lane_tile(x, width):
    if width <= LANES:
        return x[:, :width]
    return jnp.concatenate([x] * (width // LANES), axis=1)


def _attn_keys_step(qlat_ref, qpe_ref, k, kp, m_ref, l_ref, acc_ref, tq, group, visible=None,
                    kp_transposed=False):
    tk = k.shape[0]
    n = group * tq

    def scores(g):
        ql = qlat_ref[g * group:(g + 1) * group].reshape(n, KV_RANK)
        qp = qpe_ref[g * group:(g + 1) * group].reshape(n, ROPE_DIM)
        s = _dot_nt(ql, k) + (_dot(qp, kp) if kp_transposed else _dot_nt(qp, kp))
        return s if visible is None else jnp.where(visible, s, MASK_NEG)

    s_next = scores(0)
    for g in range(MLA_HEADS // group):
        s = s_next
        if g + 1 < MLA_HEADS // group:
            s_next = scores(g + 1)
        rows = pl.ds(g * n, n)
        m_prev = m_ref[rows, :]
        m_new = jnp.maximum(m_prev, jnp.max(s, axis=1, keepdims=True))
        alpha = jnp.exp2(m_prev - m_new)
        p = jnp.exp2(s - _lane_tile(m_new, tk))
        l_ref[rows, :] = alpha * l_ref[rows, :] + jnp.sum(p, axis=1, keepdims=True)
        acc_ref[rows, :] = _lane_tile(alpha, KV_RANK) * acc_ref[rows, :] + _dot(p.astype(BF16), k)
        m_ref[rows, :] = m_new


def _attn_init(m_ref, l_ref, acc_ref):
    m_ref[...] = jnp.full(m_ref.shape, MASK_NEG, F32)
    l_ref[...] = jnp.zeros(l_ref.shape, F32)
    acc_ref[...] = jnp.zeros(acc_ref.shape, F32)


def _attn_finish(wuv_ref, y_ref, l_ref, acc_ref, tq):
    for h in range(MLA_HEADS):
        rows = pl.ds(h * tq, tq)
        o = (acc_ref[rows, :] / _lane_tile(l_ref[rows, :], KV_RANK)).astype(BF16)
        y_ref[:, V_DIM * h:V_DIM * (h + 1)] = _dot(o, wuv_ref[h]).astype(BF16)


def _attn_prompt_kernel(qi_ref, kj_ref, qlat_ref, qpe_ref, k_ref, kp_ref, wuv_ref, y_ref,
                        m_ref, l_ref, acc_ref, *, tq):
    p = pl.program_id(1)
    i = qi_ref[p]
    j = kj_ref[p]

    @pl.when(j == 0)
    def _():
        _attn_init(m_ref, l_ref, acc_ref)

    @pl.when(j < i)
    def _():
        _attn_keys_step(qlat_ref, qpe_ref, k_ref[...], kp_ref[...], m_ref, l_ref, acc_ref, tq,
                        ATTN_HEAD_GROUP)

    @pl.when(j == i)
    def _():
        n = ATTN_HEAD_GROUP * tq
        row = lax.broadcasted_iota(jnp.int32, (n, tq), 0) % tq
        col = lax.broadcasted_iota(jnp.int32, (n, tq), 1)
        visible = col // CHUNK <= row // CHUNK
        _attn_keys_step(qlat_ref, qpe_ref, k_ref[...], kp_ref[...], m_ref, l_ref, acc_ref, tq,
                        ATTN_HEAD_GROUP, visible)
        _attn_finish(wuv_ref, y_ref, l_ref, acc_ref, tq)


def _attn_prompt(qlat, qpe, latb, kpeb, w_uvT, li, tq=ATTN_TQ):
    assert tq % CHUNK == 0
    T = latb.shape[0]
    nq = SEQ // tq
    pairs = [(i, j) for i in range(nq) for j in range(i + 1)]
    q_of = jnp.asarray([i for i, _ in pairs], jnp.int32)
    k_of = jnp.asarray([j for _, j in pairs], jnp.int32)
    qmap = lambda b, p, qi, kj: (0, b * nq + qi[p], 0)
    kmap = lambda b, p, qi, kj: (b * nq + kj[p], 0)
    rows = MLA_HEADS * tq
    return pl.pallas_call(
        functools.partial(_attn_prompt_kernel, tq=tq),
        out_shape=jax.ShapeDtypeStruct((T, MLA_HEADS * V_DIM), BF16),
        grid_spec=pltpu.PrefetchScalarGridSpec(
            num_scalar_prefetch=2,
            grid=(BATCH, len(pairs)),
            in_specs=[
                pl.BlockSpec((MLA_HEADS, tq, KV_RANK), qmap),
                pl.BlockSpec((MLA_HEADS, tq, ROPE_DIM), qmap),
                pl.BlockSpec((tq, KV_RANK), kmap),
                pl.BlockSpec((tq, ROPE_DIM), kmap),
                pl.BlockSpec((None, MLA_HEADS, KV_RANK, V_DIM), lambda b, p, qi, kj: (li, 0, 0, 0)),
            ],
            out_specs=pl.BlockSpec((tq, MLA_HEADS * V_DIM), lambda b, p, qi, kj: (b * nq + qi[p], 0)),
            scratch_shapes=[pltpu.VMEM((rows, LANES), F32), pltpu.VMEM((rows, LANES), F32),
                            pltpu.VMEM((rows, KV_RANK), F32)],
        ),
        compiler_params=_params("parallel", "arbitrary"),
        name="attn_prompt",
    )(q_of, k_of, qlat, qpe, latb, kpeb, w_uvT)


def _attn_sample_kernel(qlat_ref, qpe_ref, ck_ref, ckp_ref, nk_ref, nkp_ref, wuv_ref, y_ref,
                        m_ref, l_ref, acc_ref):
    j = pl.program_id(1)
    last = pl.num_programs(1) - 1

    @pl.when(j == 0)
    def _():
        _attn_init(m_ref, l_ref, acc_ref)

    def step(k, kp, kp_transposed=False):
        _attn_keys_step(qlat_ref, qpe_ref, k, kp, m_ref, l_ref, acc_ref, DEC_SEQ, MLA_HEADS,
                        kp_transposed=kp_transposed)

    @pl.when(j < last)
    def _():
        for c in range(0, ck_ref.shape[0], ATTN_SAMPLE_SUB_KEYS):
            keys = pl.ds(c, ATTN_SAMPLE_SUB_KEYS)
            step(ck_ref[keys, :].astype(BF16), ckp_ref[:, keys].astype(BF16), kp_transposed=True)

    @pl.when(j == last)
    def _():
        step(nk_ref[...], nkp_ref[...])
        _attn_finish(wuv_ref, y_ref, l_ref, acc_ref, DEC_SEQ)


def _attn_sample(qlat, qpe, cache_lat, cache_kpe_t, latb, kpeb, w_uvT, li, tk=ATTN_SAMPLE_TK):
    assert PAST_LEN % CHUNK == 0 and DEC_SEQ <= CHUNK and PAST_LEN % tk == 0
    T = latb.shape[0]
    nc = PAST_LEN // tk
    qmap = lambda b, j: (0, b, 0)
    cmap = lambda b, j: (li, b, jnp.minimum(j, nc - 1), 0)
    rows = MLA_HEADS * DEC_SEQ
    return pl.pallas_call(
        _attn_sample_kernel,
        out_shape=jax.ShapeDtypeStruct((T, MLA_HEADS * V_DIM), BF16),
        grid=(DEC_BATCH, nc + 1),
        in_specs=[
            pl.BlockSpec((MLA_HEADS, DEC_SEQ, KV_RANK), qmap),
            pl.BlockSpec((MLA_HEADS, DEC_SEQ, ROPE_DIM), qmap),
            pl.BlockSpec((None, None, tk, KV_RANK), cmap),
            pl.BlockSpec((None, None, ROPE_DIM, tk), lambda b, j: (li, b, 0, jnp.minimum(j, nc - 1))),
            pl.BlockSpec((DEC_SEQ, KV_RANK), lambda b, j: (b, 0)),
            pl.BlockSpec((DEC_SEQ, ROPE_DIM), lambda b, j: (b, 0)),
            pl.BlockSpec((None, MLA_HEADS, KV_RANK, V_DIM), lambda b, j: (li, 0, 0, 0)),
        ],
        out_specs=pl.BlockSpec((DEC_SEQ, MLA_HEADS * V_DIM), lambda b, j: (b, 0)),
        scratch_shapes=[pltpu.VMEM((rows, LANES), F32), pltpu.VMEM((rows, LANES), F32),
                        pltpu.VMEM((rows, KV_RANK), F32)],
        compiler_params=_params("parallel", "arbitrary"),
        name="attn_sample",
    )(qlat, qpe, cache_lat, cache_kpe_t, latb, kpeb, w_uvT)


POOL_HALO = 2 * SUBLANES


def _pool_kernel(*refs, tp, from_state):
    if from_state:
        z_ref, past_ref, w_ref, scale_ref, y_ref, ext_ref = refs
        ext_ref[0:1, :] = jnp.zeros((1, POOL_WIDTH), F32)
        ext_ref[1:POOL_HALO, :] = past_ref[...]
    else:
        z_ref, w_ref, scale_ref, y_ref, ext_ref = refs
        t = pl.program_id(1)

        @pl.when(t == 0)
        def _():
            ext_ref[0:POOL_HALO, :] = jnp.zeros((POOL_HALO, POOL_WIDTH), F32)

    ext_ref[POOL_HALO:POOL_HALO + tp, :] = z_ref[...]
    for gi, w in enumerate(POOL_WINDOWS):
        lanes = slice(POOL_GROUP_DIM * gi, POOL_GROUP_DIM * (gi + 1))
        tok = ext_ref[POOL_HALO:POOL_HALO + tp, lanes]
        acc = tok
        for d in range(1, w):
            acc = acc + ext_ref[POOL_HALO - d:POOL_HALO - d + tp, lanes]
        if from_state:
            mean = acc / float(w)
        else:
            pos = t * tp + lax.broadcasted_iota(jnp.int32, (tp, 1), 0)
            mean = acc / jnp.minimum(pos + 1, w).astype(F32)
        p = (mean - tok).astype(BF16)
        y_ref[:, lanes] = (_dot(p, w_ref[gi]) * scale_ref[:, lanes]).astype(BF16)
    if not from_state:
        ext_ref[0:POOL_HALO, :] = ext_ref[tp:tp + POOL_HALO, :]


def _pool_prompt(z, w_pool, pool_scale, li, tp=POOL_TP):
    T = z.shape[0]
    nt = SEQ // tp
    return pl.pallas_call(
        functools.partial(_pool_kernel, tp=tp, from_state=False),
        out_shape=jax.ShapeDtypeStruct((T, POOL_WIDTH), BF16),
        grid=(BATCH, nt),
        in_specs=[
            pl.BlockSpec((tp, POOL_WIDTH), lambda b, t: (b * nt + t, 0)),
            pl.BlockSpec((None, len(POOL_WINDOWS), POOL_GROUP_DIM, POOL_GROUP_DIM), lambda b, t: (li, 0, 0, 0)),
            pl.BlockSpec((None, 1, POOL_WIDTH), lambda b, t: (li, 0, 0)),
        ],
        out_specs=pl.BlockSpec((tp, POOL_WIDTH), lambda b, t: (b * nt + t, 0)),
        scratch_shapes=[pltpu.VMEM((POOL_HALO + tp, POOL_WIDTH), F32)],
        compiler_params=_params("arbitrary", "arbitrary"),
        name="pool_prompt",
    )(z, w_pool, pool_scale)


def _pool_sample(z, state_pool, w_pool, pool_scale, li):
    T = z.shape[0]
    return pl.pallas_call(
        functools.partial(_pool_kernel, tp=DEC_SEQ, from_state=True),
        out_shape=jax.ShapeDtypeStruct((T, POOL_WIDTH), BF16),
        grid=(DEC_BATCH,),
        in_specs=[
            pl.BlockSpec((DEC_SEQ, POOL_WIDTH), lambda b: (b, 0)),
            pl.BlockSpec((None, None, POOL_KEEP, POOL_WIDTH), lambda b: (li, b, 0, 0)),
            pl.BlockSpec((None, len(POOL_WINDOWS), POOL_GROUP_DIM, POOL_GROUP_DIM), lambda b: (li, 0, 0, 0)),
            pl.BlockSpec((None, 1, POOL_WIDTH), lambda b: (li, 0, 0)),
        ],
        out_specs=pl.BlockSpec((DEC_SEQ, POOL_WIDTH), lambda b: (b, 0)),
        scratch_shapes=[pltpu.VMEM((POOL_HALO + DEC_SEQ, POOL_WIDTH), F32)],
        compiler_params=_params("parallel"),
        name="pool_sample",
    )(z, state_pool, w_pool, pool_scale)


def _proj_res_norm_kernel(*refs, n_a, head_major, emit_x):
    a_refs = refs[:n_a]
    w_refs = refs[n_a:2 * n_a]
    x_ref, g_ref = refs[2 * n_a:2 * n_a + 2]
    outs = refs[2 * n_a + 2:]
    if head_major:
        (a_ref,), (w_ref,) = a_refs, w_refs
        xo_ref, xn_ref, cat_ref = outs
        for h in range(C_HEADS):
            cat_ref[:, HEAD_I * h:HEAD_I * (h + 1)] = a_ref[h]
        acc = _dot(cat_ref[...], w_ref[...])
    else:
        xo_ref, xn_ref = outs if emit_x else (None, outs[0])
        acc = _dot(a_refs[0][...], w_refs[0][...])
        for a_ref, w_ref in zip(a_refs[1:], w_refs[1:]):
            acc = acc + _dot(a_ref[...], w_ref[...])
    xnew = x_ref[...] + acc
    if xo_ref is not None:
        xo_ref[...] = xnew
    xn_ref[...] = _rms(xnew, g_ref[...]).astype(xn_ref.dtype)


def _proj_res_norm(a_list, w, x, g, li, head_major=False, tm=PROJ_TM):
    T = x.shape[0]
    row = lambda i: (i, 0)
    n_a = len(a_list)
    in_specs = []
    for a in a_list:
        if head_major:
            in_specs.append(pl.BlockSpec((C_HEADS, tm, HEAD_I), lambda i: (0, i, 0)))
        else:
            in_specs.append(pl.BlockSpec((tm, a.shape[1]), row))
    for kb, a in enumerate(a_list):
        kdim = C_HEADS * HEAD_I if head_major else a.shape[1]
        in_specs.append(pl.BlockSpec((None, kdim, D_MODEL), lambda i, kb=kb: (li, kb, 0)))
    in_specs += [pl.BlockSpec((tm, D_MODEL), row), pl.BlockSpec((1, D_MODEL), lambda i: (0, 0))]
    scratch = [pltpu.VMEM((tm, D_MODEL), BF16)] if head_major else []
    return pl.pallas_call(
        functools.partial(_proj_res_norm_kernel, n_a=n_a, head_major=head_major, emit_x=True),
        out_shape=(jax.ShapeDtypeStruct((T, D_MODEL), F32), jax.ShapeDtypeStruct((T, D_MODEL), BF16)),
        grid=(T // tm,),
        in_specs=in_specs,
        out_specs=(pl.BlockSpec((tm, D_MODEL), row), pl.BlockSpec((tm, D_MODEL), row)),
        scratch_shapes=scratch,
        compiler_params=_params("parallel"),
        name="proj_res_norm",
    )(*a_list, *([w] * n_a), x, g)


def _odd_in_kernel(xn_ref, w_ref, lbp_ref, o_ref, *, li):
    s = pl.program_id(0)

    def emit(epilogue):
        out = epilogue(_dot(xn_ref[...], w_ref[...]))
        for hd in range(C_HEADS):
            o_ref[hd] = out[:, HEAD_I * hd:HEAD_I * (hd + 1)]

    def silu(h):
        half = 0.5 * h
        return half + half * jnp.tanh(half)

    def forget(h):
        p = lbp_ref[...]
        e = jnp.exp(p - jnp.max(p, axis=0, keepdims=True))
        sm = e / jnp.sum(e, axis=0, keepdims=True)
        lb = jnp.clip(jnp.sum(sm[:li + 1], axis=0, keepdims=True) - sm[0:1], 0.0, 1.0)
        gain = 0.5 * (1.0 - lb)
        return (lb + gain) + gain * jnp.tanh(0.5 * h)

    for sec, epilogue in enumerate((silu, forget, lambda h: h, silu)):
        @pl.when(s == sec)
        def _(epilogue=epilogue):
            emit(epilogue)


def _odd_in(xn, w_in_c, lb_param, li, tm=ODD_IN_TM):
    T = xn.shape[0]
    return pl.pallas_call(
        functools.partial(_odd_in_kernel, li=li),
        out_shape=jax.ShapeDtypeStruct((4, C_HEADS, T, HEAD_I), F32),
        grid=(4, T // tm),
        in_specs=[
            pl.BlockSpec((tm, D_MODEL), lambda s, i: (i, 0)),
            pl.BlockSpec((None, D_MODEL, D_MODEL), lambda s, i: (li, 0, s)),
            pl.BlockSpec(lb_param.shape, lambda s, i: (0, 0)),
        ],
        out_specs=pl.BlockSpec((None, C_HEADS, tm, HEAD_I), lambda s, i: (s, 0, i, 0)),
        compiler_params=_params("parallel", "parallel"),
        name="odd_in",
    )(xn, w_in_c, lb_param)


def _gla_kernel(*refs, has_s0):
    if has_s0:
        q_ref, f_ref, v_ref, gs_ref, gon_ref, cls_ref, s0_ref, og_ref, sout_ref, st_ref = refs
    else:
        q_ref, f_ref, v_ref, gs_ref, gon_ref, cls_ref, og_ref, sout_ref, st_ref = refs
    C = GLA_CHUNK
    n = pl.program_id(1)
    last = pl.num_programs(1) - 1

    @pl.when(n == 0)
    def _():
        if has_s0:
            st_ref[...] = s0_ref[...]
        else:
            st_ref[...] = jnp.zeros(st_ref.shape, F32)

    cls = cls_ref[...]
    tri = (cls >= 0).astype(BF16)
    ones = jnp.ones((HEAD_F, LANES), BF16)
    gon = gon_ref[...]
    near = [cls == delta for delta in range(GLA_SUB)]
    levels = [(GLA_SUB << i, cls == GLA_SUB + i) for i in range((C // GLA_SUB).bit_length() - 1)]

    def bcast_row(a, period, r):
        a3 = a.reshape(C // period, period, a.shape[-1])
        return jnp.broadcast_to(a3[:, r:r + 1, :], a3.shape).reshape(C, a.shape[-1])

    def shift_in_group(a, delta):
        if delta == 0:
            return a
        a3 = a.reshape(C // GLA_SUB, GLA_SUB, a.shape[-1])
        return pltpu.roll(a3, delta, 1).reshape(C, a.shape[-1])

    def split3(g):
        g1 = g.astype(BF16)
        r1 = g - g1.astype(F32)
        g2 = r1.astype(BF16)
        return g1, g2, (r1 - g2.astype(F32)).astype(BF16)

    def head_group(hg, row0):
        hs = [hg * GLA_HEADS_PER_ITER + u for u in range(GLA_HEADS_PER_ITER)]
        U = range(len(hs))
        rows = pl.ds(row0, C)
        q = [q_ref[h, rows, :] for h in hs]
        f = [f_ref[h, rows, :] for h in hs]
        v_bf = [v_ref[h, rows, :].astype(BF16) for h in hs]
        fc = [jnp.maximum(x, F_MIN) for x in f]
        kk = [1.0 - x for x in f]
        gp = [split3(jnp.log2(x)) for x in fc]
        b = [_dot(tri, gp[u][0]) + _dot(tri, gp[u][1]) + _dot(tri, gp[u][2]) for u in U]
        st = [st_ref[h] for h in hs]
        o = [_dot((q[u] * jnp.exp2(b[u])).astype(BF16), st[u].astype(BF16)) for u in U]
        b_last = [x[C - 1:C, :] for x in b]
        k_dec = [(kk[u] * jnp.exp2(b_last[u] - b[u])).astype(BF16) for u in U]
        for u in U:
            d_rows = jnp.broadcast_to(jnp.exp2(b_last[u]), (LANES, HEAD_F)).T
            st_ref[hs[u]] = st[u] * d_rows + _dot_tn(k_dec[u], v_bf[u])

        a = [jnp.zeros((C, C), F32) for _ in U]
        for m, keep in levels:
            for u in U:
                d = b[u] - bcast_row(b[u], 2 * m, m - 1)
                qt = (q[u] * jnp.exp2(jnp.minimum(d, 0.0))).astype(BF16)
                kt = (kk[u] * jnp.exp2(jnp.minimum(-d, 0.0))).astype(BF16)
                a[u] = jnp.where(keep, _dot_nt(qt, kt), a[u])
        kd = list(kk)
        for delta in range(GLA_SUB):
            for u in U:
                if delta > 0:
                    kd[u] = shift_in_group(kd[u], 1) * fc[u]
                p = (q[u] * kd[u]).astype(BF16)
                a[u] = jnp.where(near[delta], _dot(p, ones)[:, :C], a[u])
        for u in U:
            y = _rms(o[u] + _dot(a[u].astype(BF16), v_bf[u]), gon) * gs_ref[hs[u], rows, :]
            og_ref[hs[u], rows, :] = y.astype(BF16)
        return row0

    def chunk(c, carry):
        lax.fori_loop(0, C_HEADS // GLA_HEADS_PER_ITER, head_group, pl.multiple_of(c * C, C))
        return carry

    lax.fori_loop(0, q_ref.shape[1] // C, chunk, 0)

    @pl.when(n == last)
    def _():
        sout_ref[...] = st_ref[...]


def _gla_pair_classes():
    t = np.arange(GLA_CHUNK)[:, None]
    s = np.arange(GLA_CHUNK)[None, :]
    x = (t // GLA_SUB) ^ (s // GLA_SUB)
    far = GLA_SUB + np.floor(np.log2(np.maximum(x, 1))).astype(np.int64)
    return jnp.asarray(np.where(s > t, -1, np.where(x == 0, t - s, far)), jnp.int32)


def _gla(qfvg, g_onorm, s0, li, n_streams, seq):
    T = qfvg.shape[2]
    C = min(GLA_STEP_ROWS, seq)
    nc = seq // C
    has_s0 = s0 is not None
    sec = lambda k: pl.BlockSpec((None, C_HEADS, C, HEAD_F), lambda b, n, k=k: (k, 0, b * nc + n, 0))
    in_specs = [sec(0), sec(1), sec(2), sec(3),
                pl.BlockSpec((None, 1, HEAD_I), lambda b, n: (li, 0, 0)),
                pl.BlockSpec((GLA_CHUNK, GLA_CHUNK), lambda b, n: (0, 0))]
    args = [qfvg, qfvg, qfvg, qfvg, g_onorm, _gla_pair_classes()]
    if has_s0:
        in_specs.append(pl.BlockSpec((None, None, C_HEADS, HEAD_F, HEAD_I), lambda b, n: (li, b, 0, 0, 0)))
        args.append(s0)
    return pl.pallas_call(
        functools.partial(_gla_kernel, has_s0=has_s0),
        out_shape=(jax.ShapeDtypeStruct((C_HEADS, T, HEAD_I), BF16),
                   jax.ShapeDtypeStruct((n_streams, C_HEADS, HEAD_F, HEAD_I), F32)),
        grid=(n_streams, nc),
        in_specs=in_specs,
        out_specs=(pl.BlockSpec((C_HEADS, C, HEAD_I), lambda b, n: (0, b * nc + n, 0)),
                   pl.BlockSpec((None, C_HEADS, HEAD_F, HEAD_I), lambda b, n: (b, 0, 0, 0))),
        scratch_shapes=[pltpu.VMEM((C_HEADS, HEAD_I, HEAD_F), F32)],
        compiler_params=_params("parallel", "arbitrary"),
        name="gla",
    )(*args)


FFN_PAD = SUBLANES
FFN_SUB_ROWS = 1024


def _ffn_up_kernel(*refs, tm, seg, from_state, tiles_per_stream):
    if from_state:
        (xn_ref, wa_ref, wb_ref, cwa_ref, cwb_ref, cba_ref, cbb_ref, pa_ref, pb_ref,
         act_ref, hla_ref, hlb_ref, w_ref) = refs
        carry_ref = None
    else:
        (xn_ref, wa_ref, wb_ref, cwa_ref, cwb_ref, cba_ref, cbb_ref,
         act_ref, hla_ref, hlb_ref, w_ref, carry_ref) = refs
        pa_ref = pb_ref = None
    j = pl.program_id(0)
    i = pl.program_id(1)
    sub = min(FFN_SUB_ROWS, tm)
    assert (seg % sub == 0 and not from_state) or (sub % seg == 0 and sub > seg)
    halves = ((0, cwa_ref, cba_ref, pa_ref, hla_ref), (1, cwb_ref, cbb_ref, pb_ref, hlb_ref))

    @pl.when(i == 0)
    def _():
        w_ref[0] = wa_ref[...].astype(BF16)
        w_ref[1] = wb_ref[...].astype(BF16)

    def conv(front, h, cw, cb):
        n = h.shape[0]
        ext = jnp.concatenate([front, h], axis=0)
        return cb + (ext[FFN_PAD - 2:FFN_PAD - 2 + n] * cw[0:1] + ext[FFN_PAD - 1:FFN_PAD - 1 + n] * cw[1:2] +
                     h * cw[2:3])

    fronts = [None, None]
    if not from_state:
        @pl.when(i % tiles_per_stream == 0)
        def _():
            carry_ref[j] = jnp.zeros((2, FFN_PAD, FFN_TN), F32)
        fronts = [carry_ref[j, 0], carry_ref[j, 1]]

    for r0 in range(0, tm, sub):
        convs = []
        for hidx, cw_ref, cb_ref, past_ref, hl_ref in halves:
            h = _dot(xn_ref[r0:r0 + sub, :], w_ref[hidx])
            cw = cw_ref[...]
            cb = cb_ref[...]
            if seg >= sub:
                convs.append([conv(fronts[hidx], h, cw, cb)])
                fronts[hidx] = h[sub - FFN_PAD:sub]
                if (r0 + sub) % seg == 0:
                    hl_ref[(r0 + sub) // seg - 1] = h[sub - 2:sub]
            else:
                pieces = []
                for k in range(sub // seg):
                    s = r0 // seg + k
                    hk = h[k * seg:(k + 1) * seg]
                    front = jnp.concatenate([jnp.zeros((FFN_PAD - 2, FFN_TN), F32), past_ref[s]], axis=0)
                    pieces.append(conv(front, hk, cw, cb))
                    hl_ref[s] = hk[seg - 2:seg]
                convs.append(pieces)
        for n, (a, b) in enumerate(zip(*convs)):
            rows = a.shape[0]
            o = r0 + n * rows
            half = 0.5 * a
            act_ref[o:o + rows, :] = ((half + half * jnp.tanh(half)) * b).astype(BF16)

    if not from_state:
        carry_ref[j, 0] = fronts[0]
        carry_ref[j, 1] = fronts[1]


def _ffn_up(xn, w_up, conv_w, conv_b, conv_past, layer, tm, seg):
    T = xn.shape[0]
    tn = FFN_TN
    nj = D_FF // tn
    nseg = tm // seg
    from_state = conv_past is not None
    wmap_a = lambda j, i: (layer, 0, j)
    wmap_b = lambda j, i: (layer, 0, nj + j)
    in_specs = [
        pl.BlockSpec((tm, D_MODEL), lambda j, i: (i, 0)),
        pl.BlockSpec((None, D_MODEL, tn), wmap_a), pl.BlockSpec((None, D_MODEL, tn), wmap_b),
        pl.BlockSpec((None, CONV_W, tn), wmap_a), pl.BlockSpec((None, CONV_W, tn), wmap_b),
        pl.BlockSpec((None, 1, tn), wmap_a), pl.BlockSpec((None, 1, tn), wmap_b),
    ]
    args = [xn, w_up, w_up, conv_w, conv_w, conv_b, conv_b]
    scratch = [pltpu.VMEM((2, D_MODEL, tn), BF16)]
    if from_state:
        in_specs += [pl.BlockSpec((None, nseg, CONV_W - 1, tn), lambda j, i: (layer, i, 0, j)),
                     pl.BlockSpec((None, nseg, CONV_W - 1, tn), lambda j, i: (layer, i, 0, nj + j))]
        args += [conv_past, conv_past]
        tiles_per_stream = 1
    else:
        scratch.append(pltpu.VMEM((nj, 2, FFN_PAD, tn), F32))
        tiles_per_stream = SEQ // tm
    hl_shape = jax.ShapeDtypeStruct((T // seg, CONV_W - 1, D_FF), F32)
    hl_spec = pl.BlockSpec((nseg, CONV_W - 1, tn), lambda j, i: (i, 0, j))
    return pl.pallas_call(
        functools.partial(_ffn_up_kernel, tm=tm, seg=seg, from_state=from_state,
                          tiles_per_stream=tiles_per_stream),
        out_shape=(jax.ShapeDtypeStruct((T, D_FF), BF16), hl_shape, hl_shape),
        grid=(nj, T // tm),
        in_specs=in_specs,
        out_specs=(pl.BlockSpec((tm, tn), lambda j, i: (i, j)), hl_spec, hl_spec),
        scratch_shapes=scratch,
        compiler_params=_params("arbitrary", "arbitrary"),
        name="ffn_up",
    )(*args)


def _ffn_down_kernel(a_ref, w_ref, x_ref, g_ref, *outs, emit_x):
    if emit_x:
        acc_ref, xn_ref = outs
    else:
        xn_ref, acc_ref = outs
    k = pl.program_id(1)

    @pl.when(k == 0)
    def _():
        acc_ref[...] = x_ref[...]

    acc_ref[...] += _dot(a_ref[...], w_ref[...])

    @pl.when(k == pl.num_programs(1) - 1)
    def _():
        xn_ref[...] = _rms(acc_ref[...], g_ref[...]).astype(xn_ref.dtype)


def _ffn_down(act, w_down, x, g, layer, emit_x, xn_dtype, tm=FFN_DOWN_TM, nk=FFN_DOWN_NK):
    T = x.shape[0]
    tk = D_FF // nk
    row = lambda i, k: (i, 0)
    out_shape = [jax.ShapeDtypeStruct((T, D_MODEL), xn_dtype)]
    out_specs = [pl.BlockSpec((tm, D_MODEL), row)]
    scratch = [pltpu.VMEM((tm, D_MODEL), F32)]
    if emit_x:
        out_shape.insert(0, jax.ShapeDtypeStruct((T, D_MODEL), F32))
        out_specs.insert(0, pl.BlockSpec((tm, D_MODEL), row))
        scratch = []
    res = pl.pallas_call(
        functools.partial(_ffn_down_kernel, emit_x=emit_x),
        out_shape=tuple(out_shape),
        grid=(T // tm, nk),
        in_specs=[
            pl.BlockSpec((tm, tk), lambda i, k: (i, k)),
            pl.BlockSpec((None, tk, D_MODEL), lambda i, k: (layer, k, 0)),
            pl.BlockSpec((tm, D_MODEL), row),
            pl.BlockSpec((1, D_MODEL), lambda i, k: (0, 0)),
        ],
        out_specs=tuple(out_specs),
        scratch_shapes=scratch,
        compiler_params=_params("parallel", "arbitrary"),
        name="ffn_down",
    )(act, w_down, x, g)
    return res if emit_x else (None, res[0])


def _rope_tables(pos):
    inv = ROPE_THETA ** (-jnp.arange(0, ROPE_DIM, 2, dtype=F32) / ROPE_DIM)
    ang = pos.astype(F32)[:, None] * inv[None, :]
    cos, sin = jnp.cos(ang), jnp.sin(ang)
    reps = LANES // ROPE_DIM
    return (jnp.tile(jnp.concatenate([cos, cos], axis=-1), (1, reps)),
            jnp.tile(jnp.concatenate([-sin, sin], axis=-1), (1, reps)))


def kernel(x_prompt, x_sample, cache_mla_latent, cache_mla_krope, state_pool, state_hgrn, state_ffn_conv,
           g_mix, g_ffn, g_final, w_in_a, g_qa, w_qb, g_kva, w_uk, w_uv, w_pool, pool_scale, w_out_a,
           w_in_c, lb_param, g_onorm, w_out_c, w_up, conv_w, conv_b, w_down):
    n_even = w_in_a.shape[0]
    n_odd = w_in_c.shape[0]
    Tp, Ts = BATCH * SEQ, DEC_BATCH * DEC_SEQ

    o1, o2, o3 = Q_RANK, Q_RANK + KV_RANK, Q_RANK + KV_RANK + ROPE_DIM
    w_in_e = jnp.concatenate(
        [w_in_a[..., :o2], w_in_a[..., o3:], w_in_a[..., o2:o3],
         jnp.zeros((n_even, D_MODEL, LANES - ROPE_DIM), w_in_a.dtype)], axis=-1).astype(BF16)
    w_qb4 = w_qb.reshape(n_even, Q_RANK, MLA_HEADS, NOPE_DIM + ROPE_DIM)
    w_qb_e = jnp.concatenate(
        [w_qb4[..., :NOPE_DIM].reshape(n_even, Q_RANK, Q_NOPE_ALL),
         w_qb4[..., NOPE_DIM:].reshape(n_even, Q_RANK, MLA_HEADS * ROPE_DIM)], axis=-1).astype(BF16)
    w_ukT = jnp.transpose(w_uk, (0, 2, 3, 1)).astype(BF16)
    w_uvT = jnp.transpose(w_uv, (0, 2, 1, 3)).astype(BF16)
    w_pool_b = w_pool.astype(BF16)
    w_out_a_b = w_out_a.astype(BF16)
    w_in_c_b = w_in_c.astype(BF16)
    w_out_c_b = w_out_c.astype(BF16)
    w_down_b = w_down.astype(BF16)
    g_qa3 = g_qa[:, None, :]
    g_kva3 = g_kva[:, None, :]
    pool_scale3 = pool_scale[:, None, :]
    g_onorm3 = g_onorm[:, None, :]
    conv_b3 = conv_b[:, None, :]

    krope_t = jnp.swapaxes(cache_mla_krope, 2, 3)
    cos_p, sin_p = _rope_tables(jnp.tile(jnp.arange(SEQ), BATCH))
    cos_s, sin_s = _rope_tables(jnp.tile(PAST_LEN + jnp.arange(DEC_SEQ), DEC_BATCH))

    xp = x_prompt.reshape(Tp, D_MODEL)
    xs = x_sample.reshape(Ts, D_MODEL)
    xnp, xns = xp, xs

    lat_p, kpe_p, pool_p, hg_p, cv_p = [], [], [], [], []
    lat_s, kpe_s, pool_s, hg_s, cv_s = [], [], [], [], []
    for layer in range(DEPTH):
        li = layer // 2
        g_next = g_ffn[layer:layer + 1]
        if layer % 2 == 0:
            wa = (w_in_e, g_qa3, g_kva3, w_qb_e, w_ukT)
            lat, latb, kpe, kpeb, z, qlat, qpe = _even_in(xnp, g_mix[layer:layer + 1], *wa, cos_p, sin_p, li)
            y_mla = _attn_prompt(qlat, qpe, latb, kpeb, w_uvT, li)
            y_pool = _pool_prompt(z, w_pool_b, pool_scale3, li)
            xp, xnp = _proj_res_norm([y_mla, y_pool], w_out_a_b, xp, g_next, li)
            lat_p.append(lat.reshape(BATCH, SEQ, KV_RANK))
            kpe_p.append(kpe.reshape(BATCH, SEQ, ROPE_DIM))
            pool_p.append(z.reshape(BATCH, SEQ, POOL_WIDTH)[:, SEQ - POOL_KEEP:])

            lat, latb, kpe, kpeb, z, qlat, qpe = _even_in(xns, g_mix[layer:layer + 1], *wa, cos_s, sin_s, li)
            y_mla = _attn_sample(qlat, qpe, cache_mla_latent, krope_t, latb, kpeb, w_uvT, li)
            y_pool = _pool_sample(z, state_pool, w_pool_b, pool_scale3, li)
            xs, xns = _proj_res_norm([y_mla, y_pool], w_out_a_b, xs, g_next, li)
            lat_s.append(lat.reshape(DEC_BATCH, DEC_SEQ, KV_RANK))
            kpe_s.append(kpe.reshape(DEC_BATCH, DEC_SEQ, ROPE_DIM))
            pool_s.append(z.reshape(DEC_BATCH, DEC_SEQ, POOL_WIDTH)[:, DEC_SEQ - POOL_KEEP:])
        else:
            qfvg = _odd_in(xnp, w_in_c_b, lb_param, li)
            og, s_fin = _gla(qfvg, g_onorm3, None, li, BATCH, SEQ)
            xp, xnp = _proj_res_norm([og], w_out_c_b, xp, g_next, li, head_major=True)
            hg_p.append(s_fin)

            qfvg = _odd_in(xns, w_in_c_b, lb_param, li)
            og, s_fin = _gla(qfvg, g_onorm3, state_hgrn, li, DEC_BATCH, DEC_SEQ)
            xs, xns = _proj_res_norm([og], w_out_c_b, xs, g_next, li, head_major=True)
            hg_s.append(s_fin)

        final = layer == DEPTH - 1
        g_after = g_final[None, :] if final else g_mix[layer + 1:layer + 2]
        xn_dtype = F32 if final else BF16

        act, hla, hlb = _ffn_up(xnp, w_up, conv_w, conv_b3, None, layer, tm=SEQ, seg=SEQ)
        xp, xnp = _ffn_down(act, w_down_b, xp, g_after, layer, not final, xn_dtype)
        cv_p.append(jnp.concatenate([hla, hlb], axis=-1))

        act, hla, hlb = _ffn_up(xns, w_up, conv_w, conv_b3, state_ffn_conv, layer,
                                tm=FFN_UP_TM_SAMPLE, seg=DEC_SEQ)
        xs, xns = _ffn_down(act, w_down_b, xs, g_after, layer, not final, xn_dtype)
        cv_s.append(jnp.concatenate([hla, hlb], axis=-1))

    return (xnp.reshape(BATCH, SEQ, D_MODEL), xns.reshape(DEC_BATCH, DEC_SEQ, D_MODEL),
            jnp.stack(lat_p), jnp.stack(kpe_p), jnp.stack(pool_p), jnp.stack(hg_p), jnp.stack(cv_p),
            jnp.stack(lat_s), jnp.stack(kpe_s), jnp.stack(pool_s), jnp.stack(hg_s), jnp.stack(cv_s))
```

```python
import functools

import jax
import jax.numpy as jnp
import numpy as np
from jax import lax
from jax.experimental import pallas as pl
from jax.experimental.pallas import tpu as pltpu

D_MODEL = 2048
BATCH = 8
SEQ = 2048
DEPTH = 4
DEC_BATCH = 32
DEC_SEQ = 64
PAST_LEN = 4096
CHUNK = 64
MLA_HEADS = 8
Q_RANK = 512
KV_RANK = 512
NOPE_DIM = 128
ROPE_DIM = 64
V_DIM = 128
ROPE_THETA = 10000.0
MLA_SCALE = (NOPE_DIM + ROPE_DIM) ** -0.5
LOG2_E = 1.4426950408889634
Q_SCALE = MLA_SCALE * LOG2_E
MASK_NEG = -1e30
POOL_WINDOWS = (2, 4, 8, 16)
POOL_WIDTH = D_MODEL - MLA_HEADS * V_DIM
POOL_GROUP_DIM = POOL_WIDTH // len(POOL_WINDOWS)
POOL_KEEP = max(POOL_WINDOWS) - 1
C_HEADS = 16
HEAD_F = 128
HEAD_I = D_MODEL // C_HEADS
F_MIN = 1e-30
D_FF = 5632
CONV_W = 3
EPS = 1e-6

F32 = jnp.float32
BF16 = jnp.bfloat16

LANES = 128
SUBLANES = 8
VMEM_LIMIT = 56 * 1024 * 1024

IN_EVEN_PAD = Q_RANK + KV_RANK + POOL_WIDTH + LANES
Q_NOPE_ALL = MLA_HEADS * NOPE_DIM
GLA_CHUNK = 64
GLA_SUB = SUBLANES
GLA_HEADS_PER_ITER = 16
GLA_STEP_ROWS = 64
FFN_TN = 512

EVEN_IN_TM = 256
ATTN_TQ = 256
ATTN_SAMPLE_TK = 2048
POOL_TP = 512
PROJ_TM = 512
ODD_IN_TM = 1024
FFN_UP_TM_SAMPLE = 512
FFN_DOWN_TM = 1024
FFN_DOWN_NK = 11


def _params(*sem):
    return pltpu.CompilerParams(dimension_semantics=sem, vmem_limit_bytes=VMEM_LIMIT)


def _dot(a, b):
    return jnp.dot(a, b, preferred_element_type=F32)


def _dot_nt(a, b):
    return lax.dot_general(a, b, (((1,), (1,)), ((), ())), preferred_element_type=F32)


def _dot_tn(a, b):
    return lax.dot_general(a, b, (((0,), (0,)), ((), ())), preferred_element_type=F32)


def _rms(x, g):
    return x * lax.rsqrt(jnp.mean(x * x, axis=-1, keepdims=True) + EPS) * g


def _rope(s, cos, sin, first_half):
    swapped = jnp.where(first_half, pltpu.roll(s, 96, 1), pltpu.roll(s, 32, 1))
    return s * cos + swapped * sin


def _even_in_kernel(x_ref, gin_ref, w_ref, gqa_ref, gkva_ref, wqb_ref, wuk_ref, cos_ref, sin_ref,
                    lat_ref, latb_ref, kpe_ref, kpeb_ref, z_ref, qlat_ref, qpe_ref):
    tm = x_ref.shape[0]
    xn = x_ref[...] if x_ref.dtype == BF16 else _rms(x_ref[...], gin_ref[...]).astype(BF16)
    acc = _dot(xn, w_ref[...])
    o1, o2, o3 = Q_RANK, Q_RANK + KV_RANK, Q_RANK + KV_RANK + POOL_WIDTH
    z_ref[...] = acc[:, o2:o3]
    lat = _rms(acc[:, o1:o2], gkva_ref[...])
    lat_ref[...] = lat
    latb_ref[...] = lat.astype(BF16)
    cos = cos_ref[...]
    sin = sin_ref[...]
    lane = lax.broadcasted_iota(jnp.int32, (tm, LANES), 1)
    first_half = (lane % ROPE_DIM) < (ROPE_DIM // 2)
    kr = _rope(acc[:, o3:o3 + LANES], cos, sin, first_half)
    kpe_ref[...] = kr[:, :ROPE_DIM]
    kpeb_ref[...] = kr[:, :ROPE_DIM].astype(BF16)
    cqn = _rms(acc[:, :o1], gqa_ref[...]).astype(BF16)
    q = _dot(cqn, wqb_ref[...])
    for s in range(MLA_HEADS // 2):
        lo = Q_NOPE_ALL + LANES * s
        r = (_rope(q[:, lo:lo + LANES], cos, sin, first_half) * Q_SCALE).astype(BF16)
        qpe_ref[2 * s] = r[:, :ROPE_DIM]
        qpe_ref[2 * s + 1] = r[:, ROPE_DIM:]
    for h in range(MLA_HEADS):
        ql = _dot(q[:, NOPE_DIM * h:NOPE_DIM * (h + 1)].astype(BF16), wuk_ref[h])
        qlat_ref[h] = (ql * Q_SCALE).astype(BF16)


def _even_in(x, g_in, w_in, g_qa, g_kva, w_qb, w_ukT, cos, sin, li, tm=EVEN_IN_TM):
    T = x.shape[0]
    row = lambda i: (i, 0)
    fix2 = lambda i: (0, 0)
    out_shape = (
        jax.ShapeDtypeStruct((T, KV_RANK), F32), jax.ShapeDtypeStruct((T, KV_RANK), BF16),
        jax.ShapeDtypeStruct((T, ROPE_DIM), F32), jax.ShapeDtypeStruct((T, ROPE_DIM), BF16),
        jax.ShapeDtypeStruct((T, POOL_WIDTH), F32),
        jax.ShapeDtypeStruct((MLA_HEADS, T, KV_RANK), BF16),
        jax.ShapeDtypeStruct((MLA_HEADS, T, ROPE_DIM), BF16),
    )
    return pl.pallas_call(
        _even_in_kernel,
        out_shape=out_shape,
        grid=(T // tm,),
        in_specs=[
            pl.BlockSpec((tm, D_MODEL), row),
            pl.BlockSpec((1, D_MODEL), fix2),
            pl.BlockSpec((None, D_MODEL, IN_EVEN_PAD), lambda i: (li, 0, 0)),
            pl.BlockSpec((None, 1, Q_RANK), lambda i: (li, 0, 0)),
            pl.BlockSpec((None, 1, KV_RANK), lambda i: (li, 0, 0)),
            pl.BlockSpec((None, Q_RANK, MLA_HEADS * (NOPE_DIM + ROPE_DIM)), lambda i: (li, 0, 0)),
            pl.BlockSpec((None, MLA_HEADS, NOPE_DIM, KV_RANK), lambda i: (li, 0, 0, 0)),
            pl.BlockSpec((tm, LANES), row),
            pl.BlockSpec((tm, LANES), row),
        ],
        out_specs=(
            pl.BlockSpec((tm, KV_RANK), row), pl.BlockSpec((tm, KV_RANK), row),
            pl.BlockSpec((tm, ROPE_DIM), row), pl.BlockSpec((tm, ROPE_DIM), row),
            pl.BlockSpec((tm, POOL_WIDTH), row),
            pl.BlockSpec((MLA_HEADS, tm, KV_RANK), lambda i: (0, i, 0)),
            pl.BlockSpec((MLA_HEADS, tm, ROPE_DIM), lambda i: (0, i, 0)),
        ),
        compiler_params=_params("parallel"),
        name="even_in",
    )(x, g_in, w_in, g_qa, g_kva, w_qb, w_ukT, cos, sin)


ATTN_HEAD_GROUP = 4
ATTN_SAMPLE_SUB_KEYS = 512


def _lane_tile(x, width):
    if width <= LANES:
        return x[:, :width]
    return jnp.concatenate([x] * (width // LANES), axis=1)


def _attn_keys_step(qlat_ref, qpe_ref, k, kp, m_ref, l_ref, acc_ref, tq, group, visible=None,
                    kp_transposed=False):
    tk = k.shape[0]
    n = group * tq

    def scores(g):
        ql = qlat_ref[g * group:(g + 1) * group].reshape(n, KV_RANK)
        qp = qpe_ref[g * group:(g + 1) * group].reshape(n, ROPE_DIM)
        s = _dot_nt(ql, k) + (_dot(qp, kp) if kp_transposed else _dot_nt(qp, kp))
        return s if visible is None else jnp.where(visible, s, MASK_NEG)

    s_next = scores(0)
    for g in range(MLA_HEADS // group):
        s = s_next
        if g + 1 < MLA_HEADS // group:
            s_next = scores(g + 1)
        rows = pl.ds(g * n, n)
        m_prev = m_ref[rows, :]
        m_new = jnp.maximum(m_prev, jnp.max(s, axis=1, keepdims=True))
        alpha = jnp.exp2(m_prev - m_new)
        p = jnp.exp2(s - _lane_tile(m_new, tk))
        l_ref[rows, :] = alpha * l_ref[rows, :] + jnp.sum(p, axis=1, keepdims=True)
        acc_ref[rows, :] = _lane_tile(alpha, KV_RANK) * acc_ref[rows, :] + _dot(p.astype(BF16), k)
        m_ref[rows, :] = m_new


def _attn_init(m_ref, l_ref, acc_ref):
    m_ref[...] = jnp.full(m_ref.shape, MASK_NEG, F32)
    l_ref[...] = jnp.zeros(l_ref.shape, F32)
    acc_ref[...] = jnp.zeros(acc_ref.shape, F32)


def _attn_finish(wuv_ref, y_ref, l_ref, acc_ref, tq):
    for h in range(MLA_HEADS):
        rows = pl.ds(h * tq, tq)
        o = (acc_ref[rows, :] / _lane_tile(l_ref[rows, :], KV_RANK)).astype(BF16)
        y_ref[:, V_DIM * h:V_DIM * (h + 1)] = _dot(o, wuv_ref[h]).astype(BF16)


def _attn_prompt_kernel(qi_ref, kj_ref, qlat_ref, qpe_ref, k_ref, kp_ref, wuv_ref, y_ref,
                        m_ref, l_ref, acc_ref, *, tq):
    p = pl.program_id(1)
    i = qi_ref[p]
    j = kj_ref[p]

    @pl.when(j == 0)
    def _():
        _attn_init(m_ref, l_ref, acc_ref)

    @pl.when(j < i)
    def _():
        _attn_keys_step(qlat_ref, qpe_ref, k_ref[...], kp_ref[...], m_ref, l_ref, acc_ref, tq,
                        ATTN_HEAD_GROUP)

    @pl.when(j == i)
    def _():
        n = ATTN_HEAD_GROUP * tq
        row = lax.broadcasted_iota(jnp.int32, (n, tq), 0) % tq
        col = lax.broadcasted_iota(jnp.int32, (n, tq), 1)
        visible = col // CHUNK <= row // CHUNK
        _attn_keys_step(qlat_ref, qpe_ref, k_ref[...], kp_ref[...], m_ref, l_ref, acc_ref, tq,
                        ATTN_HEAD_GROUP, visible)
        _attn_finish(wuv_ref, y_ref, l_ref, acc_ref, tq)


def _attn_prompt(qlat, qpe, latb, kpeb, w_uvT, li, tq=ATTN_TQ):
    assert tq % CHUNK == 0
    T = latb.shape[0]
    nq = SEQ // tq
    pairs = [(i, j) for i in range(nq) for j in range(i + 1)]
    q_of = jnp.asarray([i for i, _ in pairs], jnp.int32)
    k_of = jnp.asarray([j for _, j in pairs], jnp.int32)
    qmap = lambda b, p, qi, kj: (0, b * nq + qi[p], 0)
    kmap = lambda b, p, qi, kj: (b * nq + kj[p], 0)
    rows = MLA_HEADS * tq
    return pl.pallas_call(
        functools.partial(_attn_prompt_kernel, tq=tq),
        out_shape=jax.ShapeDtypeStruct((T, MLA_HEADS * V_DIM), BF16),
        grid_spec=pltpu.PrefetchScalarGridSpec(
            num_scalar_prefetch=2,
            grid=(BATCH, len(pairs)),
            in_specs=[
                pl.BlockSpec((MLA_HEADS, tq, KV_RANK), qmap),
                pl.BlockSpec((MLA_HEADS, tq, ROPE_DIM), qmap),
                pl.BlockSpec((tq, KV_RANK), kmap),
                pl.BlockSpec((tq, ROPE_DIM), kmap),
                pl.BlockSpec((None, MLA_HEADS, KV_RANK, V_DIM), lambda b, p, qi, kj: (li, 0, 0, 0)),
            ],
            out_specs=pl.BlockSpec((tq, MLA_HEADS * V_DIM), lambda b, p, qi, kj: (b * nq + qi[p], 0)),
            scratch_shapes=[pltpu.VMEM((rows, LANES), F32), pltpu.VMEM((rows, LANES), F32),
                            pltpu.VMEM((rows, KV_RANK), F32)],
        ),
        compiler_params=_params("parallel", "arbitrary"),
        name="attn_prompt",
    )(q_of, k_of, qlat, qpe, latb, kpeb, w_uvT)


def _attn_sample_kernel(qlat_ref, qpe_ref, ck_ref, ckp_ref, nk_ref, nkp_ref, wuv_ref, y_ref,
                        m_ref, l_ref, acc_ref):
    j = pl.program_id(1)
    last = pl.num_programs(1) - 1

    @pl.when(j == 0)
    def _():
        _attn_init(m_ref, l_ref, acc_ref)

    def step(k, kp, kp_transposed=False):
        _attn_keys_step(qlat_ref, qpe_ref, k, kp, m_ref, l_ref, acc_ref, DEC_SEQ, MLA_HEADS,
                        kp_transposed=kp_transposed)

    @pl.when(j < last)
    def _():
        for c in range(0, ck_ref.shape[0], ATTN_SAMPLE_SUB_KEYS):
            keys = pl.ds(c, ATTN_SAMPLE_SUB_KEYS)
            step(ck_ref[keys, :].astype(BF16), ckp_ref[:, keys].astype(BF16), kp_transposed=True)

    @pl.when(j == last)
    def _():
        step(nk_ref[...], nkp_ref[...])
        _attn_finish(wuv_ref, y_ref, l_ref, acc_ref, DEC_SEQ)


def _attn_sample(qlat, qpe, cache_lat, cache_kpe_t, latb, kpeb, w_uvT, li, tk=ATTN_SAMPLE_TK):
    assert PAST_LEN % CHUNK == 0 and DEC_SEQ <= CHUNK and PAST_LEN % tk == 0
    T = latb.shape[0]
    nc = PAST_LEN // tk
    qmap = lambda b, j: (0, b, 0)
    cmap = lambda b, j: (li, b, jnp.minimum(j, nc - 1), 0)
    rows = MLA_HEADS * DEC_SEQ
    return pl.pallas_call(
        _attn_sample_kernel,
        out_shape=jax.ShapeDtypeStruct((T, MLA_HEADS * V_DIM), BF16),
        grid=(DEC_BATCH, nc + 1),
        in_specs=[
            pl.BlockSpec((MLA_HEADS, DEC_SEQ, KV_RANK), qmap),
            pl.BlockSpec((MLA_HEADS, DEC_SEQ, ROPE_DIM), qmap),
            pl.BlockSpec((None, None, tk, KV_RANK), cmap),
            pl.BlockSpec((None, None, ROPE_DIM, tk), lambda b, j: (li, b, 0, jnp.minimum(j, nc - 1))),
            pl.BlockSpec((DEC_SEQ, KV_RANK), lambda b, j: (b, 0)),
            pl.BlockSpec((DEC_SEQ, ROPE_DIM), lambda b, j: (b, 0)),
            pl.BlockSpec((None, MLA_HEADS, KV_RANK, V_DIM), lambda b, j: (li, 0, 0, 0)),
        ],
        out_specs=pl.BlockSpec((DEC_SEQ, MLA_HEADS * V_DIM), lambda b, j: (b, 0)),
        scratch_shapes=[pltpu.VMEM((rows, LANES), F32), pltpu.VMEM((rows, LANES), F32),
                        pltpu.VMEM((rows, KV_RANK), F32)],
        compiler_params=_params("parallel", "arbitrary"),
        name="attn_sample",
    )(qlat, qpe, cache_lat, cache_kpe_t, latb, kpeb, w_uvT)


POOL_HALO = 2 * SUBLANES


def _pool_kernel(*refs, tp, from_state):
    if from_state:
        z_ref, past_ref, w_ref, scale_ref, y_ref, ext_ref = refs
        ext_ref[0:1, :] = jnp.zeros((1, POOL_WIDTH), F32)
        ext_ref[1:POOL_HALO, :] = past_ref[...]
    else:
        z_ref, w_ref, scale_ref, y_ref, ext_ref = refs
        t = pl.program_id(1)

        @pl.when(t == 0)
        def _():
            ext_ref[0:POOL_HALO, :] = jnp.zeros((POOL_HALO, POOL_WIDTH), F32)

    ext_ref[POOL_HALO:POOL_HALO + tp, :] = z_ref[...]
    for gi, w in enumerate(POOL_WINDOWS):
        lanes = slice(POOL_GROUP_DIM * gi, POOL_GROUP_DIM * (gi + 1))
        tok = ext_ref[POOL_HALO:POOL_HALO + tp, lanes]
        acc = tok
        for d in range(1, w):
            acc = acc + ext_ref[POOL_HALO - d:POOL_HALO - d + tp, lanes]
        if from_state:
            mean = acc / float(w)
        else:
            pos = t * tp + lax.broadcasted_iota(jnp.int32, (tp, 1), 0)
            mean = acc / jnp.minimum(pos + 1, w).astype(F32)
        p = (mean - tok).astype(BF16)
        y_ref[:, lanes] = (_dot(p, w_ref[gi]) * scale_ref[:, lanes]).astype(BF16)
    if not from_state:
        ext_ref[0:POOL_HALO, :] = ext_ref[tp:tp + POOL_HALO, :]


def _pool_prompt(z, w_pool, pool_scale, li, tp=POOL_TP):
    T = z.shape[0]
    nt = SEQ // tp
    return pl.pallas_call(
        functools.partial(_pool_kernel, tp=tp, from_state=False),
        out_shape=jax.ShapeDtypeStruct((T, POOL_WIDTH), BF16),
        grid=(BATCH, nt),
        in_specs=[
            pl.BlockSpec((tp, POOL_WIDTH), lambda b, t: (b * nt + t, 0)),
            pl.BlockSpec((None, len(POOL_WINDOWS), POOL_GROUP_DIM, POOL_GROUP_DIM), lambda b, t: (li, 0, 0, 0)),
            pl.BlockSpec((None, 1, POOL_WIDTH), lambda b, t: (li, 0, 0)),
        ],
        out_specs=pl.BlockSpec((tp, POOL_WIDTH), lambda b, t: (b * nt + t, 0)),
        scratch_shapes=[pltpu.VMEM((POOL_HALO + tp, POOL_WIDTH), F32)],
        compiler_params=_params("arbitrary", "arbitrary"),
        name="pool_prompt",
    )(z, w_pool, pool_scale)


def _pool_sample(z, state_pool, w_pool, pool_scale, li):
    T = z.shape[0]
    return pl.pallas_call(
        functools.partial(_pool_kernel, tp=DEC_SEQ, from_state=True),
        out_shape=jax.ShapeDtypeStruct((T, POOL_WIDTH), BF16),
        grid=(DEC_BATCH,),
        in_specs=[
            pl.BlockSpec((DEC_SEQ, POOL_WIDTH), lambda b: (b, 0)),
            pl.BlockSpec((None, None, POOL_KEEP, POOL_WIDTH), lambda b: (li, b, 0, 0)),
            pl.BlockSpec((None, len(POOL_WINDOWS), POOL_GROUP_DIM, POOL_GROUP_DIM), lambda b: (li, 0, 0, 0)),
            pl.BlockSpec((None, 1, POOL_WIDTH), lambda b: (li, 0, 0)),
        ],
        out_specs=pl.BlockSpec((DEC_SEQ, POOL_WIDTH), lambda b: (b, 0)),
        scratch_shapes=[pltpu.VMEM((POOL_HALO + DEC_SEQ, POOL_WIDTH), F32)],
        compiler_params=_params("parallel"),
        name="pool_sample",
    )(z, state_pool, w_pool, pool_scale)


def _proj_res_norm_kernel(*refs, n_a, head_major, emit_x):
    a_refs = refs[:n_a]
    w_refs = refs[n_a:2 * n_a]
    x_ref, g_ref = refs[2 * n_a:2 * n_a + 2]
    outs = refs[2 * n_a + 2:]
    if head_major:
        (a_ref,), (w_ref,) = a_refs, w_refs
        xo_ref, xn_ref, cat_ref = outs
        for h in range(C_HEADS):
            cat_ref[:, HEAD_I * h:HEAD_I * (h + 1)] = a_ref[h]
        acc = _dot(cat_ref[...], w_ref[...])
    else:
        xo_ref, xn_ref = outs if emit_x else (None, outs[0])
        acc = _dot(a_refs[0][...], w_refs[0][...])
        for a_ref, w_ref in zip(a_refs[1:], w_refs[1:]):
            acc = acc + _dot(a_ref[...], w_ref[...])
    xnew = x_ref[...] + acc
    if xo_ref is not None:
        xo_ref[...] = xnew
    xn_ref[...] = _rms(xnew, g_ref[...]).astype(xn_ref.dtype)


def _proj_res_norm(a_list, w, x, g, li, head_major=False, tm=PROJ_TM):
    T = x.shape[0]
    row = lambda i: (i, 0)
    n_a = len(a_list)
    in_specs = []
    for a in a_list:
        if head_major:
            in_specs.append(pl.BlockSpec((C_HEADS, tm, HEAD_I), lambda i: (0, i, 0)))
        else:
            in_specs.append(pl.BlockSpec((tm, a.shape[1]), row))
    for kb, a in enumerate(a_list):
        kdim = C_HEADS * HEAD_I if head_major else a.shape[1]
        in_specs.append(pl.BlockSpec((None, kdim, D_MODEL), lambda i, kb=kb: (li, kb, 0)))
    in_specs += [pl.BlockSpec((tm, D_MODEL), row), pl.BlockSpec((1, D_MODEL), lambda i: (0, 0))]
    scratch = [pltpu.VMEM((tm, D_MODEL), BF16)] if head_major else []
    return pl.pallas_call(
        functools.partial(_proj_res_norm_kernel, n_a=n_a, head_major=head_major, emit_x=True),
        out_shape=(jax.ShapeDtypeStruct((T, D_MODEL), F32), jax.ShapeDtypeStruct((T, D_MODEL), BF16)),
        grid=(T // tm,),
        in_specs=in_specs,
        out_specs=(pl.BlockSpec((tm, D_MODEL), row), pl.BlockSpec((tm, D_MODEL), row)),
        scratch_shapes=scratch,
        compiler_params=_params("parallel"),
        name="proj_res_norm",
    )(*a_list, *([w] * n_a), x, g)


def _odd_in_kernel(xn_ref, w_ref, lbp_ref, o_ref, *, li):
    s = pl.program_id(0)

    def emit(epilogue):
        out = epilogue(_dot(xn_ref[...], w_ref[...]))
        for hd in range(C_HEADS):
            o_ref[hd] = out[:, HEAD_I * hd:HEAD_I * (hd + 1)]

    def silu(h):
        half = 0.5 * h
        return half + half * jnp.tanh(half)

    def forget(h):
        p = lbp_ref[...]
        e = jnp.exp(p - jnp.max(p, axis=0, keepdims=True))
        sm = e / jnp.sum(e, axis=0, keepdims=True)
        lb = jnp.clip(jnp.sum(sm[:li + 1], axis=0, keepdims=True) - sm[0:1], 0.0, 1.0)
        gain = 0.5 * (1.0 - lb)
        return (lb + gain) + gain * jnp.tanh(0.5 * h)

    for sec, epilogue in enumerate((silu, forget, lambda h: h, silu)):
        @pl.when(s == sec)
        def _(epilogue=epilogue):
            emit(epilogue)


def _odd_in(xn, w_in_c, lb_param, li, tm=ODD_IN_TM):
    T = xn.shape[0]
    return pl.pallas_call(
        functools.partial(_odd_in_kernel, li=li),
        out_shape=jax.ShapeDtypeStruct((4, C_HEADS, T, HEAD_I), F32),
        grid=(4, T // tm),
        in_specs=[
            pl.BlockSpec((tm, D_MODEL), lambda s, i: (i, 0)),
            pl.BlockSpec((None, D_MODEL, D_MODEL), lambda s, i: (li, 0, s)),
            pl.BlockSpec(lb_param.shape, lambda s, i: (0, 0)),
        ],
        out_specs=pl.BlockSpec((None, C_HEADS, tm, HEAD_I), lambda s, i: (s, 0, i, 0)),
        compiler_params=_params("parallel", "parallel"),
        name="odd_in",
    )(xn, w_in_c, lb_param)


def _gla_kernel(*refs, has_s0):
    if has_s0:
        q_ref, f_ref, v_ref, gs_ref, gon_ref, cls_ref, s0_ref, og_ref, sout_ref, st_ref = refs
    else:
        q_ref, f_ref, v_ref, gs_ref, gon_ref, cls_ref, og_ref, sout_ref, st_ref = refs
    C = GLA_CHUNK
    n = pl.program_id(1)
    last = pl.num_programs(1) - 1

    @pl.when(n == 0)
    def _():
        if has_s0:
            st_ref[...] = s0_ref[...]
        else:
            st_ref[...] = jnp.zeros(st_ref.shape, F32)

    cls = cls_ref[...]
    tri = (cls >= 0).astype(BF16)
    ones = jnp.ones((HEAD_F, LANES), BF16)
    gon = gon_ref[...]
    near = [cls == delta for delta in range(GLA_SUB)]
    levels = [(GLA_SUB << i, cls == GLA_SUB + i) for i in range((C // GLA_SUB).bit_length() - 1)]

    def bcast_row(a, period, r):
        a3 = a.reshape(C // period, period, a.shape[-1])
        return jnp.broadcast_to(a3[:, r:r + 1, :], a3.shape).reshape(C, a.shape[-1])

    def shift_in_group(a, delta):
        if delta == 0:
            return a
        a3 = a.reshape(C // GLA_SUB, GLA_SUB, a.shape[-1])
        return pltpu.roll(a3, delta, 1).reshape(C, a.shape[-1])

    def split3(g):
        g1 = g.astype(BF16)
        r1 = g - g1.astype(F32)
        g2 = r1.astype(BF16)
        return g1, g2, (r1 - g2.astype(F32)).astype(BF16)

    def head_group(hg, row0):
        hs = [hg * GLA_HEADS_PER_ITER + u for u in range(GLA_HEADS_PER_ITER)]
        U = range(len(hs))
        rows = pl.ds(row0, C)
        q = [q_ref[h, rows, :] for h in hs]
        f = [f_ref[h, rows, :] for h in hs]
        v_bf = [v_ref[h, rows, :].astype(BF16) for h in hs]
        fc = [jnp.maximum(x, F_MIN) for x in f]
        kk = [1.0 - x for x in f]
        gp = [split3(jnp.log2(x)) for x in fc]
        b = [_dot(tri, gp[u][0]) + _dot(tri, gp[u][1]) + _dot(tri, gp[u][2]) for u in U]
        st = [st_ref[h] for h in hs]
        o = [_dot((q[u] * jnp.exp2(b[u])).astype(BF16), st[u].astype(BF16)) for u in U]
        b_last = [x[C - 1:C, :] for x in b]
        k_dec = [(kk[u] * jnp.exp2(b_last[u] - b[u])).astype(BF16) for u in U]
        for u in U:
            d_rows = jnp.broadcast_to(jnp.exp2(b_last[u]), (LANES, HEAD_F)).T
            st_ref[hs[u]] = st[u] * d_rows + _dot_tn(k_dec[u], v_bf[u])

        a = [jnp.zeros((C, C), F32) for _ in U]
        for m, keep in levels:
            for u in U:
                d = b[u] - bcast_row(b[u], 2 * m, m - 1)
                qt = (q[u] * jnp.exp2(jnp.minimum(d, 0.0))).astype(BF16)
                kt = (kk[u] * jnp.exp2(jnp.minimum(-d, 0.0))).astype(BF16)
                a[u] = jnp.where(keep, _dot_nt(qt, kt), a[u])
        kd = list(kk)
        for delta in range(GLA_SUB):
            for u in U:
                if delta > 0:
                    kd[u] = shift_in_group(kd[u], 1) * fc[u]
                p = (q[u] * kd[u]).astype(BF16)
                a[u] = jnp.where(near[delta], _dot(p, ones)[:, :C], a[u])
        for u in U:
            y = _rms(o[u] + _dot(a[u].astype(BF16), v_bf[u]), gon) * gs_ref[hs[u], rows, :]
            og_ref[hs[u], rows, :] = y.astype(BF16)
        return row0

    def chunk(c, carry):
        lax.fori_loop(0, C_HEADS // GLA_HEADS_PER_ITER, head_group, pl.multiple_of(c * C, C))
        return carry

    lax.fori_loop(0, q_ref.shape[1] // C, chunk, 0)

    @pl.when(n == last)
    def _():
        sout_ref[...] = st_ref[...]


def _gla_pair_classes():
    t = np.arange(GLA_CHUNK)[:, None]
    s = np.arange(GLA_CHUNK)[None, :]
    x = (t // GLA_SUB) ^ (s // GLA_SUB)
    far = GLA_SUB + np.floor(np.log2(np.maximum(x, 1))).astype(np.int64)
    return jnp.asarray(np.where(s > t, -1, np.where(x == 0, t - s, far)), jnp.int32)


def _gla(qfvg, g_onorm, s0, li, n_streams, seq):
    T = qfvg.shape[2]
    C = min(GLA_STEP_ROWS, seq)
    nc = seq // C
    has_s0 = s0 is not None
    sec = lambda k: pl.BlockSpec((None, C_HEADS, C, HEAD_F), lambda b, n, k=k: (k, 0, b * nc + n, 0))
    in_specs = [sec(0), sec(1), sec(2), sec(3),
                pl.BlockSpec((None, 1, HEAD_I), lambda b, n: (li, 0, 0)),
                pl.BlockSpec((GLA_CHUNK, GLA_CHUNK), lambda b, n: (0, 0))]
    args = [qfvg, qfvg, qfvg, qfvg, g_onorm, _gla_pair_classes()]
    if has_s0:
        in_specs.append(pl.BlockSpec((None, None, C_HEADS, HEAD_F, HEAD_I), lambda b, n: (li, b, 0, 0, 0)))
        args.append(s0)
    return pl.pallas_call(
        functools.partial(_gla_kernel, has_s0=has_s0),
        out_shape=(jax.ShapeDtypeStruct((C_HEADS, T, HEAD_I), BF16),
                   jax.ShapeDtypeStruct((n_streams, C_HEADS, HEAD_F, HEAD_I), F32)),
        grid=(n_streams, nc),
        in_specs=in_specs,
        out_specs=(pl.BlockSpec((C_HEADS, C, HEAD_I), lambda b, n: (0, b * nc + n, 0)),
                   pl.BlockSpec((None, C_HEADS, HEAD_F, HEAD_I), lambda b, n: (b, 0, 0, 0))),
        scratch_shapes=[pltpu.VMEM((C_HEADS, HEAD_I, HEAD_F), F32)],
        compiler_params=_params("parallel", "arbitrary"),
        name="gla",
    )(*args)


FFN_PAD = SUBLANES
FFN_SUB_ROWS = 2048


def _ffn_up_kernel(*refs, tm, seg, from_state, tiles_per_stream):
    if from_state:
        (xn_ref, wa_ref, wb_ref, cwa_ref, cwb_ref, cba_ref, cbb_ref, pa_ref, pb_ref,
         act_ref, hla_ref, hlb_ref, w_ref) = refs
        carry_ref = None
    else:
        (xn_ref, wa_ref, wb_ref, cwa_ref, cwb_ref, cba_ref, cbb_ref,
         act_ref, hla_ref, hlb_ref, w_ref, carry_ref) = refs
        pa_ref = pb_ref = None
    j = pl.program_id(0)
    i = pl.program_id(1)
    sub = min(FFN_SUB_ROWS, tm)
    assert (seg % sub == 0 and not from_state) or (sub % seg == 0 and sub > seg)
    halves = ((0, cwa_ref, cba_ref, pa_ref, hla_ref), (1, cwb_ref, cbb_ref, pb_ref, hlb_ref))

    @pl.when(i == 0)
    def _():
        w_ref[0] = wa_ref[...].astype(BF16)
        w_ref[1] = wb_ref[...].astype(BF16)

    def conv(front, h, cw, cb):
        n = h.shape[0]
        ext = jnp.concatenate([front, h], axis=0)
        return cb + (ext[FFN_PAD - 2:FFN_PAD - 2 + n] * cw[0:1] + ext[FFN_PAD - 1:FFN_PAD - 1 + n] * cw[1:2] +
                     h * cw[2:3])

    fronts = [None, None]
    if not from_state:
        @pl.when(i % tiles_per_stream == 0)
        def _():
            carry_ref[j] = jnp.zeros((2, FFN_PAD, FFN_TN), F32)
        fronts = [carry_ref[j, 0], carry_ref[j, 1]]

    for r0 in range(0, tm, sub):
        convs = []
        for hidx, cw_ref, cb_ref, past_ref, hl_ref in halves:
            h = _dot(xn_ref[r0:r0 + sub, :], w_ref[hidx])
            cw = cw_ref[...]
            cb = cb_ref[...]
            if seg >= sub:
                convs.append([conv(fronts[hidx], h, cw, cb)])
                fronts[hidx] = h[sub - FFN_PAD:sub]
                if (r0 + sub) % seg == 0:
                    hl_ref[(r0 + sub) // seg - 1] = h[sub - 2:sub]
            else:
                pieces = []
                for k in range(sub // seg):
                    s = r0 // seg + k
                    hk = h[k * seg:(k + 1) * seg]
                    front = jnp.concatenate([jnp.zeros((FFN_PAD - 2, FFN_TN), F32), past_ref[s]], axis=0)
                    pieces.append(conv(front, hk, cw, cb))
                    hl_ref[s] = hk[seg - 2:seg]
                convs.append(pieces)
        for n, (a, b) in enumerate(zip(*convs)):
            rows = a.shape[0]
            o = r0 + n * rows
            half = 0.5 * a
            act_ref[o:o + rows, :] = ((half + half * jnp.tanh(half)) * b).astype(BF16)

    if not from_state:
        carry_ref[j, 0] = fronts[0]
        carry_ref[j, 1] = fronts[1]


def _ffn_up(xn, w_up, conv_w, conv_b, conv_past, layer, tm, seg):
    T = xn.shape[0]
    tn = FFN_TN
    nj = D_FF // tn
    nseg = tm // seg
    from_state = conv_past is not None
    wmap_a = lambda j, i: (layer, 0, j)
    wmap_b = lambda j, i: (layer, 0, nj + j)
    in_specs = [
        pl.BlockSpec((tm, D_MODEL), lambda j, i: (i, 0)),
        pl.BlockSpec((None, D_MODEL, tn), wmap_a), pl.BlockSpec((None, D_MODEL, tn), wmap_b),
        pl.BlockSpec((None, CONV_W, tn), wmap_a), pl.BlockSpec((None, CONV_W, tn), wmap_b),
        pl.BlockSpec((None, 1, tn), wmap_a), pl.BlockSpec((None, 1, tn), wmap_b),
    ]
    args = [xn, w_up, w_up, conv_w, conv_w, conv_b, conv_b]
    scratch = [pltpu.VMEM((2, D_MODEL, tn), BF16)]
    if from_state:
        in_specs += [pl.BlockSpec((None, nseg, CONV_W - 1, tn), lambda j, i: (layer, i, 0, j)),
                     pl.BlockSpec((None, nseg, CONV_W - 1, tn), lambda j, i: (layer, i, 0, nj + j))]
        args += [conv_past, conv_past]
        tiles_per_stream = 1
    else:
        scratch.append(pltpu.VMEM((nj, 2, FFN_PAD, tn), F32))
        tiles_per_stream = SEQ // tm
    hl_shape = jax.ShapeDtypeStruct((T // seg, CONV_W - 1, D_FF), F32)
    hl_spec = pl.BlockSpec((nseg, CONV_W - 1, tn), lambda j, i: (i, 0, j))
    return pl.pallas_call(
        functools.partial(_ffn_up_kernel, tm=tm, seg=seg, from_state=from_state,
                          tiles_per_stream=tiles_per_stream),
        out_shape=(jax.ShapeDtypeStruct((T, D_FF), BF16), hl_shape, hl_shape),
        grid=(nj, T // tm),
        in_specs=in_specs,
        out_specs=(pl.BlockSpec((tm, tn), lambda j, i: (i, j)), hl_spec, hl_spec),
        scratch_shapes=scratch,
        compiler_params=_params("arbitrary", "arbitrary"),
        name="ffn_up",
    )(*args)


def _ffn_down_kernel(a_ref, w_ref, x_ref, g_ref, *outs, emit_x):
    if emit_x:
        acc_ref, xn_ref = outs
    else:
        xn_ref, acc_ref = outs
    k = pl.program_id(1)

    @pl.when(k == 0)
    def _():
        acc_ref[...] = x_ref[...]

    acc_ref[...] += _dot(a_ref[...], w_ref[...])

    @pl.when(k == pl.num_programs(1) - 1)
    def _():
        xn_ref[...] = _rms(acc_ref[...], g_ref[...]).astype(xn_ref.dtype)


def _ffn_down(act, w_down, x, g, layer, emit_x, xn_dtype, tm=FFN_DOWN_TM, nk=FFN_DOWN_NK):
    T = x.shape[0]
    tk = D_FF // nk
    row = lambda i, k: (i, 0)
    out_shape = [jax.ShapeDtypeStruct((T, D_MODEL), xn_dtype)]
    out_specs = [pl.BlockSpec((tm, D_MODEL), row)]
    scratch = [pltpu.VMEM((tm, D_MODEL), F32)]
    if emit_x:
        out_shape.insert(0, jax.ShapeDtypeStruct((T, D_MODEL), F32))
        out_specs.insert(0, pl.BlockSpec((tm, D_MODEL), row))
        scratch = []
    res = pl.pallas_call(
        functools.partial(_ffn_down_kernel, emit_x=emit_x),
        out_shape=tuple(out_shape),
        grid=(T // tm, nk),
        in_specs=[
            pl.BlockSpec((tm, tk), lambda i, k: (i, k)),
            pl.BlockSpec((None, tk, D_MODEL), lambda i, k: (layer, k, 0)),
            pl.BlockSpec((tm, D_MODEL), row),
            pl.BlockSpec((1, D_MODEL), lambda i, k: (0, 0)),
        ],
        out_specs=tuple(out_specs),
        scratch_shapes=scratch,
        compiler_params=_params("parallel", "arbitrary"),
        name="ffn_down",
    )(act, w_down, x, g)
    return res if emit_x else (None, res[0])


def _rope_tables(pos):
    inv = ROPE_THETA ** (-jnp.arange(0, ROPE_DIM, 2, dtype=F32) / ROPE_DIM)
    ang = pos.astype(F32)[:, None] * inv[None, :]
    cos, sin = jnp.cos(ang), jnp.sin(ang)
    reps = LANES // ROPE_DIM
    return (jnp.tile(jnp.concatenate([cos, cos], axis=-1), (1, reps)),
            jnp.tile(jnp.concatenate([-sin, sin], axis=-1), (1, reps)))


def kernel(x_prompt, x_sample, cache_mla_latent, cache_mla_krope, state_pool, state_hgrn, state_ffn_conv,
           g_mix, g_ffn, g_final, w_in_a, g_qa, w_qb, g_kva, w_uk, w_uv, w_pool, pool_scale, w_out_a,
           w_in_c, lb_param, g_onorm, w_out_c, w_up, conv_w, conv_b, w_down):
    n_even = w_in_a.shape[0]
    n_odd = w_in_c.shape[0]
    Tp, Ts = BATCH * SEQ, DEC_BATCH * DEC_SEQ

    o1, o2, o3 = Q_RANK, Q_RANK + KV_RANK, Q_RANK + KV_RANK + ROPE_DIM
    w_in_e = jnp.concatenate(
        [w_in_a[..., :o2], w_in_a[..., o3:], w_in_a[..., o2:o3],
         jnp.zeros((n_even, D_MODEL, LANES - ROPE_DIM), w_in_a.dtype)], axis=-1).astype(BF16)
    w_qb4 = w_qb.reshape(n_even, Q_RANK, MLA_HEADS, NOPE_DIM + ROPE_DIM)
    w_qb_e = jnp.concatenate(
        [w_qb4[..., :NOPE_DIM].reshape(n_even, Q_RANK, Q_NOPE_ALL),
         w_qb4[..., NOPE_DIM:].reshape(n_even, Q_RANK, MLA_HEADS * ROPE_DIM)], axis=-1).astype(BF16)
    w_ukT = jnp.transpose(w_uk, (0, 2, 3, 1)).astype(BF16)
    w_uvT = jnp.transpose(w_uv, (0, 2, 1, 3)).astype(BF16)
    w_pool_b = w_pool.astype(BF16)
    w_out_a_b = w_out_a.astype(BF16)
    w_in_c_b = w_in_c.astype(BF16)
    w_out_c_b = w_out_c.astype(BF16)
    w_down_b = w_down.astype(BF16)
    g_qa3 = g_qa[:, None, :]
    g_kva3 = g_kva[:, None, :]
    pool_scale3 = pool_scale[:, None, :]
    g_onorm3 = g_onorm[:, None, :]
    conv_b3 = conv_b[:, None, :]

    krope_t = jnp.swapaxes(cache_mla_krope, 2, 3)
    cos_p, sin_p = _rope_tables(jnp.tile(jnp.arange(SEQ), BATCH))
    cos_s, sin_s = _rope_tables(jnp.tile(PAST_LEN + jnp.arange(DEC_SEQ), DEC_BATCH))

    xp = x_prompt.reshape(Tp, D_MODEL)
    xs = x_sample.reshape(Ts, D_MODEL)
    xnp, xns = xp, xs

    lat_p, kpe_p, pool_p, hg_p, cv_p = [], [], [], [], []
    lat_s, kpe_s, pool_s, hg_s, cv_s = [], [], [], [], []
    for layer in range(DEPTH):
        li = layer // 2
        g_next = g_ffn[layer:layer + 1]
        if layer % 2 == 0:
            wa = (w_in_e, g_qa3, g_kva3, w_qb_e, w_ukT)
            lat, latb, kpe, kpeb, z, qlat, qpe = _even_in(xnp, g_mix[layer:layer + 1], *wa, cos_p, sin_p, li)
            y_mla = _attn_prompt(qlat, qpe, latb, kpeb, w_uvT, li)
            y_pool = _pool_prompt(z, w_pool_b, pool_scale3, li)
            xp, xnp = _proj_res_norm([y_mla, y_pool], w_out_a_b, xp, g_next, li)
            lat_p.append(lat.reshape(BATCH, SEQ, KV_RANK))
            kpe_p.append(kpe.reshape(BATCH, SEQ, ROPE_DIM))
            pool_p.append(z.reshape(BATCH, SEQ, POOL_WIDTH)[:, SEQ - POOL_KEEP:])

            lat, latb, kpe, kpeb, z, qlat, qpe = _even_in(xns, g_mix[layer:layer + 1], *wa, cos_s, sin_s, li)
            y_mla = _attn_sample(qlat, qpe, cache_mla_latent, krope_t, latb, kpeb, w_uvT, li)
            y_pool = _pool_sample(z, state_pool, w_pool_b, pool_scale3, li)
            xs, xns = _proj_res_norm([y_mla, y_pool], w_out_a_b, xs, g_next, li)
            lat_s.append(lat.reshape(DEC_BATCH, DEC_SEQ, KV_RANK))
            kpe_s.append(kpe.reshape(DEC_BATCH, DEC_SEQ, ROPE_DIM))
            pool_s.append(z.reshape(DEC_BATCH, DEC_SEQ, POOL_WIDTH)[:, DEC_SEQ - POOL_KEEP:])
        else:
            qfvg = _odd_in(xnp, w_in_c_b, lb_param, li)
            og, s_fin = _gla(qfvg, g_onorm3, None, li, BATCH, SEQ)
            xp, xnp = _proj_res_norm([og], w_out_c_b, xp, g_next, li, head_major=True)
            hg_p.append(s_fin)

            qfvg = _odd_in(xns, w_in_c_b, lb_param, li)
            og, s_fin = _gla(qfvg, g_onorm3, state_hgrn, li, DEC_BATCH, DEC_SEQ)
            xs, xns = _proj_res_norm([og], w_out_c_b, xs, g_next, li, head_major=True)
            hg_s.append(s_fin)

        final = layer == DEPTH - 1
        g_after = g_final[None, :] if final else g_mix[layer + 1:layer + 2]
        xn_dtype = F32 if final else BF16

        act, hla, hlb = _ffn_up(xnp, w_up, conv_w, conv_b3, None, layer, tm=SEQ, seg=SEQ)
        xp, xnp = _ffn_down(act, w_down_b, xp, g_after, layer, not final, xn_dtype)
        cv_p.append(jnp.concatenate([hla, hlb], axis=-1))

        act, hla, hlb = _ffn_up(xns, w_up, conv_w, conv_b3, state_ffn_conv, layer,
                                tm=FFN_UP_TM_SAMPLE, seg=DEC_SEQ)
        xs, xns = _ffn_down(act, w_down_b, xs, g_after, layer, not final, xn_dtype)
        cv_s.append(jnp.concatenate([hla, hlb], axis=-1))

    return (xnp.reshape(BATCH, SEQ, D_MODEL), xns.reshape(DEC_BATCH, DEC_SEQ, D_MODEL),
            jnp.stack(lat_p), jnp.stack(kpe_p), jnp.stack(pool_p), jnp.stack(hg_p), jnp.stack(cv_p),
            jnp.stack(lat_s), jnp.stack(kpe_s), jnp.stack(pool_s), jnp.stack(hg_s), jnp.stack(cv_s))
```
